```python
import math
import jax, jax.numpy as jnp
from jax import lax
import numpy as np

D_MODEL = 1024
BATCH = 8
SEQ = 8192
DEPTH = 2

N_EVEN = (DEPTH + 1) // 2
N_ODD = DEPTH // 2

BLOCK = 128
RMS_EPS = 1e-6
GN_EPS = 1e-5
NEG = -1e30

A_HEADS = 8
A_KV_HEADS = 2
A_HEAD_DIM = 64
WINDOW = 128
ROPE_THETA = 500000.0
ROPE_DIM = A_HEAD_DIM // 4

B_HEADS = 8
B_HEAD_DIM = 64
RET_THETA = 10000.0

C_HEADS = 8
C_NOPE = 64
C_ROPE = 32
C_V = 64
C_Q_LORA = 512
C_KV_LORA = 256
MLA_THETA = 10000.0

D_HEADS = 4
D_HEAD_DIM = 128
D_CONV = 5

N_EXPERTS = 16
EC_CAPACITY_FACTOR = 2
D_FF_EXPERT = 2048

A_Q_W = A_HEADS * A_HEAD_DIM
A_KV_W = A_KV_HEADS * A_HEAD_DIM
B_W = B_HEADS * B_HEAD_DIM
D_W = D_HEADS * D_HEAD_DIM
EVEN_SPLITS = (A_Q_W, A_KV_W, A_KV_W, B_W, B_W, B_W, B_W)
ODD_SPLITS = (C_Q_LORA, C_KV_LORA, C_ROPE, D_W, D_W, D_W, D_W, 4 * D_HEADS)
EVEN_COLS = sum(EVEN_SPLITS)
ODD_COLS = sum(ODD_SPLITS)
EVEN_MIX_W = A_Q_W + B_W
ODD_MIX_W = C_HEADS * C_V + D_W

kernel_name = 'hybrid_bidir_swa_retention_mla_mlstm_ecmoe'


def rms_norm(x, g):
    xf = x.astype(jnp.float32)
    y = xf * lax.rsqrt(jnp.mean(xf * xf, axis=-1, keepdims=True) + RMS_EPS)
    return (y * g.astype(jnp.float32)).astype(x.dtype)


def head_group_norm(y):
    mu = jnp.mean(y, axis=-1, keepdims=True)
    yc = y - mu
    return yc * lax.rsqrt(jnp.mean(yc * yc, axis=-1, keepdims=True) + GN_EPS)


def split_cols(z, sizes):
    return jnp.split(z, [int(c) for c in np.cumsum(sizes)[:-1]], axis=-1)


def rope(x, positions, theta):
    d = x.shape[-1]
    half = d // 2
    inv_freq = theta ** (-jnp.arange(half, dtype=jnp.float32) * 2.0 / d)
    ang = positions.astype(jnp.float32)[:, :, None] * inv_freq
    cos = jnp.cos(ang)[:, :, None, :]
    sin = jnp.sin(ang)[:, :, None, :]
    xf = x.astype(jnp.float32)
    x1, x2 = xf[..., :half], xf[..., half:]
    out = jnp.concatenate([x1 * cos - x2 * sin, x2 * cos + x1 * sin], axis=-1)
    return out.astype(x.dtype)


def partial_rope(x, positions):
    return jnp.concatenate([rope(x[..., :ROPE_DIM], positions, ROPE_THETA), x[..., ROPE_DIM:]], axis=-1)


def window_gqa_sink(q, k, v, sink):
    B, S, Hq, d = q.shape
    Hkv = k.shape[2]
    G = Hq // Hkv
    N = S // BLOCK
    qb = q.reshape(B, N, BLOCK, Hkv, G, d)
    pad = ((0, 0), (BLOCK, BLOCK), (0, 0), (0, 0))
    kp = jnp.pad(k, pad).reshape(B, N + 2, BLOCK, Hkv, d)
    vp = jnp.pad(v, pad).reshape(B, N + 2, BLOCK, Hkv, d)
    kw = jnp.concatenate([kp[:, :-2], kp[:, 1:-1], kp[:, 2:]], axis=2)
    vw = jnp.concatenate([vp[:, :-2], vp[:, 1:-1], vp[:, 2:]], axis=2)
    s = jnp.einsum('bnqhgd,bnkhd->bhgnqk', qb, kw).astype(jnp.float32) * (d ** -0.5)
    qpos = jnp.arange(N)[:, None, None] * BLOCK + jnp.arange(BLOCK)[None, :, None]
    kpos = jnp.arange(N)[:, None, None] * BLOCK - BLOCK + jnp.arange(3 * BLOCK)[None, None, :]
    valid = (jnp.abs(qpos - kpos) <= WINDOW) & (kpos >= 0) & (kpos < S)
    s = jnp.where(valid, s, NEG)
    sink_col = jnp.broadcast_to(sink.astype(jnp.float32).reshape(1, Hkv, G, 1, 1, 1), s.shape[:-1] + (1,))
    p = jax.nn.softmax(jnp.concatenate([s, sink_col], axis=-1), axis=-1)[..., :-1]
    o = jnp.einsum('bhgnqk,bnkhd->bnqhgd', p.astype(v.dtype), vw)
    return o.reshape(B, S, Hq * d)


def retention_scan(q, k, v, log_gamma, strict):
    B, S, H, d = q.shape
    N = S // BLOCK
    L = BLOCK
    qc = q.reshape(B, N, L, H, d)
    kc = k.reshape(B, N, L, H, d)
    vc = v.reshape(B, N, L, H, d)
    idx = jnp.arange(L, dtype=jnp.float32)
    rel = idx[:, None] - idx[None, :]
    mask = (rel > 0) if strict else (rel >= 0)
    decay = jnp.where(mask[None], jnp.exp(jnp.maximum(rel, 0.0)[None] * log_gamma[:, None, None]), 0.0)
    s = jnp.einsum('bnihd,bnjhd->bnhij', qc, kc) * decay
    intra = jnp.einsum('bnhij,bnjhd->bnihd', s, vc)
    zeta = jnp.exp((L - 1 - idx)[:, None] * log_gamma[None, :])
    xi = jnp.exp((idx + 1)[:, None] * log_gamma[None, :])
    U = jnp.einsum('bnjhk,bnjhv->nbhkv', kc * zeta[:, :, None], vc)
    chunk_decay = jnp.exp(L * log_gamma)[:, None, None]

    def step(R, u):
        return chunk_decay * R + u, R

    _, R_prev = lax.scan(step, jnp.zeros((B, H, d, d), jnp.float32), U)
    inter = jnp.einsum('bnihk,nbhkv->bnihv', qc * xi[:, :, None], R_prev)
    return (intra + inter).reshape(B, S, H, d)


def bidirectional_retention(q, k, v, g, positions, decay_logit):
    B, S, _ = q.shape
    shp = (B, S, B_HEADS, B_HEAD_DIM)
    q = rope(q.reshape(shp), positions, RET_THETA).astype(jnp.float32)
    k = rope(k.reshape(shp), positions, RET_THETA).astype(jnp.float32) * (B_HEAD_DIM ** -0.5)
    v = v.reshape(shp).astype(jnp.float32)
    lg = jax.nn.log_sigmoid(decay_logit.astype(jnp.float32))
    y_f = retention_scan(q, k, v, lg[0], strict=False)
    y_b = jnp.flip(retention_scan(jnp.flip(q, 1), jnp.flip(k, 1), jnp.flip(v, 1), lg[1], strict=True), 1)
    y = head_group_norm(y_f + y_b).reshape(B, S, B_W)
    return jax.nn.silu(g.astype(jnp.float32)) * y


def dense_block_attention(q, k, v):
    B, S, H, dq = q.shape
    dv = v.shape[-1]
    N = S // BLOCK
    scale = dq ** -0.5
    qb = q.reshape(B, N, BLOCK, H, dq).transpose(1, 0, 2, 3, 4)

    def one_block(qblk):
        s = jnp.einsum('bqhd,bkhd->bhqk', qblk, k).astype(jnp.float32) * scale
        p = jax.nn.softmax(s, axis=-1)
        return jnp.einsum('bhqk,bkhd->bqhd', p.astype(v.dtype), v)

    o = lax.map(one_block, qb)
    return o.transpose(1, 0, 2, 3, 4).reshape(B, S, H * dv)


def mla_attention(cq, ckv, kr, positions, norm_q, norm_kv, w_uq, w_ukv):
    B, S, _ = cq.shape
    q = (rms_norm(cq, norm_q) @ w_uq).reshape(B, S, C_HEADS, C_NOPE + C_ROPE)
    kv = (rms_norm(ckv, norm_kv) @ w_ukv).reshape(B, S, C_HEADS, C_NOPE + C_V)
    q_rope = rope(q[..., C_NOPE:], positions, MLA_THETA)
    k_rope = rope(kr.reshape(B, S, 1, C_ROPE), positions, MLA_THETA)
    q_full = jnp.concatenate([q[..., :C_NOPE], q_rope], axis=-1)
    k_full = jnp.concatenate([kv[..., :C_NOPE], jnp.broadcast_to(k_rope, (B, S, C_HEADS, C_ROPE))], axis=-1)
    return dense_block_attention(q_full, k_full, kv[..., C_NOPE:])


def mlstm_scan(q, k, v, i_pre, f_pre, strict):
    B, S, H, d = q.shape
    N = S // BLOCK
    L = BLOCK
    qc = q.reshape(B, N, L, H, d)
    kc = k.reshape(B, N, L, H, d)
    vc = v.reshape(B, N, L, H, d)
    ic = i_pre.reshape(B, N, L, H)
    b = jnp.cumsum(jax.nn.log_sigmoid(f_pre).reshape(B, N, L, H), axis=2)
    g = b[:, :, -1]
    a = g[:, :, None] - b + ic
    a_max = jnp.max(a, axis=2)
    w = jnp.exp(a - a_max[:, :, None])
    U = jnp.einsum('bnjh,bnjhk,bnjhv->nbhkv', w, kc, vc)
    u = jnp.einsum('bnjh,bnjhk->nbhk', w, kc)

    def step(carry, xs):
        C, n, m = carry
        U_c, u_c, g_c, amax_c = xs
        m_new = jnp.maximum(g_c + m, amax_c)
        sp = jnp.exp(g_c + m - m_new)
        sc = jnp.exp(amax_c - m_new)
        C_new = sp[..., None, None] * C + sc[..., None, None] * U_c
        n_new = sp[..., None] * n + sc[..., None] * u_c
        return (C_new, n_new, m_new), (C, n, m)

    init = (jnp.zeros((B, H, d, d), jnp.float32), jnp.zeros((B, H, d), jnp.float32), jnp.zeros((B, H), jnp.float32))
    _, (C_prev, n_prev, m_prev) = lax.scan(step, init, (U, u, g.transpose(1, 0, 2), a_max.transpose(1, 0, 2)))
    m_prev = m_prev.transpose(1, 0, 2)
    bt = b.transpose(0, 1, 3, 2)
    it = ic.transpose(0, 1, 3, 2)
    logD = bt[..., :, None] - bt[..., None, :] + it[..., None, :]
    ti = jnp.arange(L)
    mask = (ti[:, None] > ti[None, :]) if strict else (ti[:, None] >= ti[None, :])
    logD = jnp.where(mask, logD, NEG)
    log_inter = bt + m_prev[..., None]
    m_t = jnp.maximum(jnp.max(logD, axis=-1), log_inter)
    Dw = jnp.exp(logD - m_t[..., None])
    s = jnp.einsum('bnthd,bnjhd->bnhtj', qc, kc) * Dw
    inter_w = jnp.exp(log_inter - m_t)
    inter_w_t = inter_w.transpose(0, 1, 3, 2)[..., None]
    num = jnp.einsum('bnhtj,bnjhd->bnthd', s, vc) + jnp.einsum('bnthk,nbhkv->bnthv', qc, C_prev) * inter_w_t
    den = jnp.sum(s, axis=-1) + inter_w * jnp.einsum('bnthk,nbhk->bnht', qc, n_prev)
    denom = jnp.maximum(jnp.abs(den), jnp.exp(-m_t)).transpose(0, 1, 3, 2)[..., None]
    return (num / denom).reshape(B, S, H, d)


def centred_depthwise_conv(x, w):
    W, C = w.shape
    return lax.conv_general_dilated(x, w[:, None, :].astype(x.dtype), window_strides=(1,),
                                    padding=((W // 2, W // 2),), dimension_numbers=('NWC', 'WIO', 'NWC'),
                                    feature_group_count=C)


def bidirectional_mlstm(q, k, v, o, gates, conv_w, gate_bias):
    B, S, _ = q.shape
    qk = jax.nn.silu(centred_depthwise_conv(jnp.concatenate([q, k], axis=-1), conv_w))
    shp = (B, S, D_HEADS, D_HEAD_DIM)
    qh = qk[..., :D_W].reshape(shp).astype(jnp.float32)
    kh = qk[..., D_W:].reshape(shp).astype(jnp.float32) * (D_HEAD_DIM ** -0.5)
    vh = v.reshape(shp).astype(jnp.float32)
    gt = gates.reshape(B, S, 4, D_HEADS).astype(jnp.float32) + gate_bias.astype(jnp.float32)
    h_f = mlstm_scan(qh, kh, vh, gt[:, :, 0], gt[:, :, 1], strict=False)
    h_b = jnp.flip(mlstm_scan(jnp.flip(qh, 1), jnp.flip(kh, 1), jnp.flip(vh, 1),
                              jnp.flip(gt[:, :, 2], 1), jnp.flip(gt[:, :, 3], 1), strict=True), 1)
    return jax.nn.sigmoid(o.astype(jnp.float32)) * (h_f + h_b).reshape(B, S, D_W)


def even_mixer(h, positions, w_in, w_out, sink, decay_logit):
    B, S, _ = h.shape
    aq, ak, av, bq, bk, bv, bg = split_cols(h @ w_in, EVEN_SPLITS)
    aq = partial_rope(aq.reshape(B, S, A_HEADS, A_HEAD_DIM), positions)
    ak = partial_rope(ak.reshape(B, S, A_KV_HEADS, A_HEAD_DIM), positions)
    av = av.reshape(B, S, A_KV_HEADS, A_HEAD_DIM)
    o_a = window_gqa_sink(aq, ak, av, sink)
    o_b = bidirectional_retention(bq, bk, bv, bg, positions, decay_logit)
    return jnp.concatenate([o_a.astype(h.dtype), o_b.astype(h.dtype)], axis=-1) @ w_out


def odd_mixer(h, positions, w_in, w_out, norm_q, norm_kv, w_uq, w_ukv, conv_w, gate_bias):
    cq, ckv, kr, dq, dk, dv, do, dg = split_cols(h @ w_in, ODD_SPLITS)
    o_c = mla_attention(cq, ckv, kr, positions, norm_q, norm_kv, w_uq, w_ukv)
    o_d = bidirectional_mlstm(dq, dk, dv, do, dg, conv_w, gate_bias)
    return jnp.concatenate([o_c.astype(h.dtype), o_d.astype(h.dtype)], axis=-1) @ w_out


def expert_choice_ffn(h, w_router, w_gate, w_up, w_down):
    B, S, D = h.shape
    cap = EC_CAPACITY_FACTOR * S // N_EXPERTS
    aff = jax.nn.softmax(jnp.einsum('bsd,de->bse', h, w_router).astype(jnp.float32), axis=-1)
    gate, idx = lax.top_k(aff.transpose(0, 2, 1), cap)
    bidx = jnp.arange(B)[:, None, None]
    xin = h[bidx, idx]
    hid = jax.nn.silu(jnp.einsum('becd,edf->becf', xin, w_gate)) * jnp.einsum('becd,edf->becf', xin, w_up)
    y = jnp.einsum('becf,efd->becd', hid, w_down) * gate[..., None].astype(h.dtype)
    return jnp.zeros((B, S, D), y.dtype).at[bidx, idx].add(y)


def setup_inputs(seed: int = 0) -> dict:
    key = jax.random.key(seed)
    ks = jax.random.split(key, 24)
    f32 = jnp.float32

    def nrm(k, shape, scale):
        return jax.random.normal(k, shape, f32) * scale

    x = jax.random.normal(ks[0], (BATCH, SEQ, D_MODEL), f32)
    positions = jnp.broadcast_to(jnp.arange(SEQ, dtype=jnp.int32), (BATCH, SEQ))
    norm_mix = 1.0 + nrm(ks[1], (DEPTH, D_MODEL), 0.05)
    norm_ffn = 1.0 + nrm(ks[2], (DEPTH, D_MODEL), 0.05)
    norm_final = 1.0 + nrm(ks[3], (D_MODEL,), 0.05)
    ev_w_in = nrm(ks[4], (N_EVEN, D_MODEL, EVEN_COLS), D_MODEL ** -0.5)
    ev_w_out = nrm(ks[5], (N_EVEN, EVEN_MIX_W, D_MODEL), EVEN_MIX_W ** -0.5)
    attn_sink = nrm(ks[6], (N_EVEN, A_HEADS), 0.5)
    gam = 1.0 - 2.0 ** (-5.0 - jnp.arange(B_HEADS, dtype=f32))
    base_logit = jnp.log(gam) - jnp.log1p(-gam)
    ret_decay_logit = base_logit[None, None, :] + nrm(ks[7], (N_EVEN, 2, B_HEADS), 0.1)
    od_w_in = nrm(ks[8], (N_ODD, D_MODEL, ODD_COLS), D_MODEL ** -0.5)
    od_w_out = nrm(ks[9], (N_ODD, ODD_MIX_W, D_MODEL), ODD_MIX_W ** -0.5)
    mla_norm_q = 1.0 + nrm(ks[10], (N_ODD, C_Q_LORA), 0.05)
    mla_norm_kv = 1.0 + nrm(ks[11], (N_ODD, C_KV_LORA), 0.05)
    mla_w_uq = nrm(ks[12], (N_ODD, C_Q_LORA, C_HEADS * (C_NOPE + C_ROPE)), C_Q_LORA ** -0.5)
    mla_w_ukv = nrm(ks[13], (N_ODD, C_KV_LORA, C_HEADS * (C_NOPE + C_V)), C_KV_LORA ** -0.5)
    mlstm_conv = nrm(ks[14], (N_ODD, D_CONV, 2 * D_W), D_CONV ** -0.5)
    fb = jnp.linspace(3.0, 6.0, D_HEADS, dtype=f32)
    zb = jnp.zeros((D_HEADS,), f32)
    mlstm_gate_bias = jnp.stack([zb, fb, zb, fb])[None] + nrm(ks[15], (N_ODD, 4, D_HEADS), 0.1)
    moe_router = nrm(ks[16], (DEPTH, D_MODEL, N_EXPERTS), D_MODEL ** -0.5)
    moe_w_gate = nrm(ks[17], (DEPTH, N_EXPERTS, D_MODEL, D_FF_EXPERT), D_MODEL ** -0.5)
    moe_w_up = nrm(ks[18], (DEPTH, N_EXPERTS, D_MODEL, D_FF_EXPERT), D_MODEL ** -0.5)
    moe_w_down = nrm(ks[19], (DEPTH, N_EXPERTS, D_FF_EXPERT, D_MODEL), D_FF_EXPERT ** -0.5)
    return {'x': x, 'positions': positions, 'norm_mix': norm_mix, 'norm_ffn': norm_ffn,
            'norm_final': norm_final, 'ev_w_in': ev_w_in, 'ev_w_out': ev_w_out, 'attn_sink': attn_sink,
            'ret_decay_logit': ret_decay_logit, 'od_w_in': od_w_in, 'od_w_out': od_w_out,
            'mla_norm_q': mla_norm_q, 'mla_norm_kv': mla_norm_kv, 'mla_w_uq': mla_w_uq,
            'mla_w_ukv': mla_w_ukv, 'mlstm_conv': mlstm_conv, 'mlstm_gate_bias': mlstm_gate_bias,
            'moe_router': moe_router, 'moe_w_gate': moe_w_gate, 'moe_w_up': moe_w_up,
            'moe_w_down': moe_w_down}


def reference(x, positions, norm_mix, norm_ffn, norm_final, ev_w_in, ev_w_out, attn_sink,
              ret_decay_logit, od_w_in, od_w_out, mla_norm_q, mla_norm_kv, mla_w_uq, mla_w_ukv,
              mlstm_conv, mlstm_gate_bias, moe_router, moe_w_gate, moe_w_up, moe_w_down):
    for l in range(DEPTH):
        j = l // 2
        h = rms_norm(x, norm_mix[l])
        if l % 2 == 0:
            mix = even_mixer(h, positions, ev_w_in[j], ev_w_out[j], attn_sink[j], ret_decay_logit[j])
        else:
            mix = odd_mixer(h, positions, od_w_in[j], od_w_out[j], mla_norm_q[j], mla_norm_kv[j],
                            mla_w_uq[j], mla_w_ukv[j], mlstm_conv[j], mlstm_gate_bias[j])
        x = x + mix.astype(x.dtype)
        ffn = expert_choice_ffn(rms_norm(x, norm_ffn[l]), moe_router[l], moe_w_gate[l], moe_w_up[l], moe_w_down[l])
        x = x + ffn.astype(x.dtype)
    return rms_norm(x, norm_final)
```

```python
import functools

import jax
import jax.numpy as jnp
import numpy as np
from jax import lax
from jax.experimental import pallas as pl
from jax.experimental.pallas import tpu as pltpu

F32 = jnp.float32
BF16 = jnp.bfloat16
I32 = jnp.int32

LANES = 128
BLOCK = 128
RMS_EPS = 1e-6
GN_EPS = 1e-5
NEG = -1e30
VMEM_LIMIT = 56 * 1024 * 1024

A_HEADS, A_KV_HEADS, A_HEAD_DIM = 8, 2, 64
ROPE_THETA = 500000.0
ROPE_DIM = A_HEAD_DIM // 4
B_HEADS, B_HEAD_DIM = 8, 64
RET_THETA = 10000.0
C_HEADS, C_NOPE, C_ROPE, C_V = 8, 64, 32, 64
C_Q_LORA, C_KV_LORA = 512, 256
MLA_THETA = 10000.0
D_HEADS, D_HEAD_DIM, D_CONV = 4, 128, 5
N_EXPERTS = 16
EC_CAPACITY_FACTOR = 2

A_Q_W = A_HEADS * A_HEAD_DIM
A_KV_W = A_KV_HEADS * A_HEAD_DIM
B_W = B_HEADS * B_HEAD_DIM
D_W = D_HEADS * D_HEAD_DIM
C_SLOT = 128


def _params(n_axes, vmem=VMEM_LIMIT):
    return pltpu.CompilerParams(dimension_semantics=("arbitrary",) * n_axes,
                                vmem_limit_bytes=vmem)


def _rms(x, g):
    return x * lax.rsqrt(jnp.mean(x * x, axis=-1, keepdims=True) + RMS_EPS) * g


def _dot(a, b):
    return jnp.dot(a, b, preferred_element_type=F32)


def _dot_nt(a, b):
    return lax.dot_general(a, b, (((1,), (1,)), ((), ())), preferred_element_type=F32)


def _dot_tn(a, b):
    return lax.dot_general(a, b, (((0,), (0,)), ((), ())), preferred_element_type=F32)


def _rope_slab(z, cos, sin, half, first_half):
    partner = jnp.where(first_half, pltpu.roll(z, LANES - half, 1), pltpu.roll(z, half, 1))
    return z * cos + partner * sin


def _rope_tables(positions, theta, head_dim, rot_start, rot_dim):
    half = rot_dim // 2
    assert head_dim % half == 0 and rot_start % half == 0
    inv_freq = theta ** (-jnp.arange(half, dtype=F32) * 2.0 / rot_dim)
    d = np.arange(LANES) % head_dim - rot_start
    rot = (d >= 0) & (d < rot_dim)
    sign = np.where(d < half, -1.0, 1.0).astype(np.float32)
    ang = positions.astype(F32).reshape(-1, 1) * inv_freq
    ang = jnp.tile(ang, (1, LANES // half))
    cos = jnp.where(rot[None, :], jnp.cos(ang), 1.0)
    sin = jnp.where(rot[None, :], jnp.sin(ang) * sign[None, :], 0.0)
    return cos, sin


def _inproj_even_kernel(x_ref, g_ref, w_ref, ca_ref, sa_ref, cb_ref, sb_ref,
                        aq_ref, akv_ref, bq_ref, bk_ref, bv_ref, bg_ref):
    h = _rms(x_ref[...], g_ref[...]).astype(BF16)
    lane = lax.broadcasted_iota(I32, (1, LANES), 1)
    first_a = (lane % A_HEAD_DIM) < (ROPE_DIM // 2)
    first_b = (lane % B_HEAD_DIM) < (B_HEAD_DIM // 2)
    ca, sa, cb, sb = ca_ref[...], sa_ref[...], cb_ref[...], sb_ref[...]

    def rope_a(z):
        return _rope_slab(z, ca, sa, ROPE_DIM // 2, first_a)

    def rope_b(z):
        return _rope_slab(z, cb, sb, B_HEAD_DIM // 2, first_b)

    off = 0
    z = _dot(h, w_ref[:, off:off + A_Q_W])
    for s in range(A_Q_W // LANES):
        aq_ref[:, s * LANES:(s + 1) * LANES] = rope_a(z[:, s * LANES:(s + 1) * LANES]).astype(BF16)
    off += A_Q_W
    z = _dot(h, w_ref[:, off:off + 2 * A_KV_W])
    akv_ref[:, :A_KV_W] = rope_a(z[:, :A_KV_W]).astype(BF16)
    akv_ref[:, A_KV_W:] = z[:, A_KV_W:].astype(BF16)
    off += 2 * A_KV_W
    z = _dot(h, w_ref[:, off:off + B_W])
    for s in range(B_W // LANES):
        bq_ref[:, s * LANES:(s + 1) * LANES] = rope_b(z[:, s * LANES:(s + 1) * LANES])
    off += B_W
    z = _dot(h, w_ref[:, off:off + B_W])
    for s in range(B_W // LANES):
        bk_ref[:, s * LANES:(s + 1) * LANES] = rope_b(z[:, s * LANES:(s + 1) * LANES]) * (B_HEAD_DIM ** -0.5)
    off += B_W
    bv_ref[...] = _dot(h, w_ref[:, off:off + B_W]).astype(BF16)
    off += B_W
    bg_ref[...] = _dot(h, w_ref[:, off:off + B_W])


def _inproj_even(x2, g, w, tabs, tm):
    T, D = x2.shape
    ncol = w.shape[1]
    row = lambda width: pl.BlockSpec((tm, width), lambda i: (i, 0))
    full = lambda a: pl.BlockSpec(a.shape, lambda i: (0, 0))
    out_shape = (jax.ShapeDtypeStruct((T, A_Q_W), BF16), jax.ShapeDtypeStruct((T, 2 * A_KV_W), BF16),
                 jax.ShapeDtypeStruct((T, B_W), F32), jax.ShapeDtypeStruct((T, B_W), F32),
                 jax.ShapeDtypeStruct((T, B_W), BF16), jax.ShapeDtypeStruct((T, B_W), F32))
    return pl.pallas_call(
        _inproj_even_kernel, out_shape=out_shape, grid=(T // tm,),
        in_specs=[row(D), full(g), pl.BlockSpec((D, ncol), lambda i: (0, 0))] + [row(LANES)] * 4,
        out_specs=(row(A_Q_W), row(2 * A_KV_W), row(B_W), row(B_W), row(B_W), row(B_W)),
        compiler_params=_params(1), name="inproj_even")(x2, g, w, *tabs)


def _win_attn_kernel(sink_ref, q_ref, kvp_ref, kvc_ref, kvn_ref, o_ref, *, n_blocks):
    n = pl.program_id(1)
    group = A_HEADS // A_KV_HEADS
    rows = group * BLOCK
    qi = lax.broadcasted_iota(I32, (rows, BLOCK), 0) % BLOCK
    kj = lax.broadcasted_iota(I32, (rows, BLOCK), 1)
    ok_prev = (kj >= qi) & (n > 0)
    ok_next = (kj <= qi) & (n < n_blocks - 1)
    q = q_ref[...]
    scale = A_HEAD_DIM ** -0.5
    for g in range(A_KV_HEADS):
        ks = slice(g * A_HEAD_DIM, (g + 1) * A_HEAD_DIM)
        vs = slice(A_KV_W + g * A_HEAD_DIM, A_KV_W + (g + 1) * A_HEAD_DIM)
        qg = jnp.concatenate([q[:, (g * group + i) * A_HEAD_DIM:(g * group + i + 1) * A_HEAD_DIM]
                              for i in range(group)], axis=0)
        sink = jnp.concatenate([jnp.full((BLOCK, 1), sink_ref[g * group + i], F32)
                                for i in range(group)], axis=0)
        sp = jnp.where(ok_prev, _dot_nt(qg, kvp_ref[:, ks]) * scale, NEG)
        sc = _dot_nt(qg, kvc_ref[:, ks]) * scale
        sn = jnp.where(ok_next, _dot_nt(qg, kvn_ref[:, ks]) * scale, NEG)
        m = jnp.maximum(jnp.maximum(jnp.max(sp, -1, keepdims=True), jnp.max(sc, -1, keepdims=True)),
                        jnp.maximum(jnp.max(sn, -1, keepdims=True), sink))
        pp, pc, pn = jnp.exp(sp - m), jnp.exp(sc - m), jnp.exp(sn - m)
        den = (jnp.sum(pp, -1, keepdims=True) + jnp.sum(pc, -1, keepdims=True)
               + jnp.sum(pn, -1, keepdims=True) + jnp.exp(sink - m))
        o = (_dot(pp.astype(BF16), kvp_ref[:, vs]) + _dot(pc.astype(BF16), kvc_ref[:, vs])
             + _dot(pn.astype(BF16), kvn_ref[:, vs])) / den
        for i in range(group):
            hh = g * group + i
            o_ref[:, hh * A_HEAD_DIM:(hh + 1) * A_HEAD_DIM] = o[i * BLOCK:(i + 1) * BLOCK].astype(BF16)


def _win_attn(aq, akv, sink, B, S):
    N = S // BLOCK
    T = B * S
    kv = lambda shift: pl.BlockSpec(
        (BLOCK, 2 * A_KV_W), lambda b, n: (b * N + jnp.clip(n + shift, 0, N - 1), 0))
    return pl.pallas_call(
        functools.partial(_win_attn_kernel, n_blocks=N),
        out_shape=jax.ShapeDtypeStruct((T, A_Q_W), BF16), grid=(B, N),
        in_specs=[pl.BlockSpec(memory_space=pltpu.SMEM),
                  pl.BlockSpec((BLOCK, A_Q_W), lambda b, n: (b * N + n, 0)), kv(-1), kv(0), kv(1)],
        out_specs=pl.BlockSpec((BLOCK, A_Q_W), lambda b, n: (b * N + n, 0)),
        compiler_params=_params(2), name="win_attn")(sink, aq, akv, akv, akv)


def _ret_tables(lg_ref, lgl_ref, d_ref, xif_ref, xib_ref, zf_ref, zb_ref):
    i = lax.broadcasted_iota(I32, (BLOCK, BLOCK), 0)
    j = lax.broadcasted_iota(I32, (BLOCK, BLOCK), 1)
    rel = (i - j).astype(F32)
    for h in range(B_HEADS):
        d_ref[h] = jnp.where(i >= j, jnp.exp(rel * lg_ref[0, h]), jnp.exp(-rel * lg_ref[1, h]))
    t = lax.broadcasted_iota(I32, (BLOCK, B_W), 0).astype(F32)
    lgf, lgb = lgl_ref[0:1, :], lgl_ref[1:2, :]
    xif_ref[...] = jnp.exp((t + 1.0) * lgf)
    zf_ref[...] = jnp.exp((BLOCK - 1.0 - t) * lgf)
    xib_ref[...] = jnp.exp((BLOCK - t) * lgb)
    zb_ref[...] = jnp.exp(t * lgb)


def _ret_bwd_state_kernel(lgl_ref, k_ref, v_ref, r_out_ref, r_ref):
    n = pl.program_id(1)

    @pl.when(n == 0)
    def _():
        r_ref[...] = jnp.zeros_like(r_ref)

    r_out_ref[...] = r_ref[...]
    lgb = lgl_ref[1:2, :]
    t = lax.broadcasted_iota(I32, (BLOCK, B_W), 0).astype(F32)
    kz = (k_ref[...] * jnp.exp(t * lgb)).astype(BF16)
    cd = jnp.exp(BLOCK * lgb)
    v = v_ref[...]
    for h in range(B_HEADS):
        hs = slice(h * B_HEAD_DIM, (h + 1) * B_HEAD_DIM)
        r_ref[:, hs] = cd[:, hs] * r_ref[:, hs] + _dot_tn(kz[:, hs], v[:, hs])


def _ret_bwd_state(lgl, bk, bv, B, S):
    N = S // BLOCK
    blk = lambda: pl.BlockSpec((BLOCK, B_W), lambda b, n: (b * N + N - 1 - n, 0))
    return pl.pallas_call(
        _ret_bwd_state_kernel,
        out_shape=jax.ShapeDtypeStruct((B, N, B_HEAD_DIM, B_W), F32), grid=(B, N),
        in_specs=[pl.BlockSpec(lgl.shape, lambda b, n: (0, 0)), blk(), blk()],
        out_specs=pl.BlockSpec((None, None, B_HEAD_DIM, B_W), lambda b, n: (b, N - 1 - n, 0, 0)),
        scratch_shapes=[pltpu.VMEM((B_HEAD_DIM, B_W), F32)],
        compiler_params=_params(2), name="ret_bwd_state")(lgl, bk, bv)


def _ret_main_kernel(lg_ref, lgl_ref, q_ref, k_ref, v_ref, g_ref, rb_ref, o_ref,
                     rf_ref, d_ref, xif_ref, xib_ref, zf_ref, zb_ref):
    b, n = pl.program_id(0), pl.program_id(1)

    @pl.when((b == 0) & (n == 0))
    def _():
        _ret_tables(lg_ref, lgl_ref, d_ref, xif_ref, xib_ref, zf_ref, zb_ref)

    @pl.when(n == 0)
    def _():
        rf_ref[...] = jnp.zeros_like(rf_ref)

    q, k, v, gate = q_ref[...], k_ref[...], v_ref[...], g_ref[...]
    qb, kb = q.astype(BF16), k.astype(BF16)
    qxf = (q * xif_ref[...]).astype(BF16)
    qxb = (q * xib_ref[...]).astype(BF16)
    kzf = (k * zf_ref[...]).astype(BF16)
    cdf = jnp.exp(BLOCK * lgl_ref[0:1, :])
    for h in range(B_HEADS):
        hs = slice(h * B_HEAD_DIM, (h + 1) * B_HEAD_DIM)
        s = _dot_nt(qb[:, hs], kb[:, hs]) * d_ref[h]
        y = (_dot(s.astype(BF16), v[:, hs])
             + _dot(qxf[:, hs], rf_ref[:, hs].astype(BF16))
             + _dot(qxb[:, hs], rb_ref[:, hs].astype(BF16)))
        yc = y - jnp.mean(y, axis=-1, keepdims=True)
        yn = yc * lax.rsqrt(jnp.mean(yc * yc, axis=-1, keepdims=True) + GN_EPS)
        gh = gate[:, hs]
        o_ref[:, hs] = (gh * jax.nn.sigmoid(gh) * yn).astype(BF16)
        rf_ref[:, hs] = cdf[:, hs] * rf_ref[:, hs] + _dot_tn(kzf[:, hs], v[:, hs])


def _ret_main(lg, lgl, bq, bk, bv, bg, rb, B, S):
    N = S // BLOCK
    T = B * S
    blk = lambda: pl.BlockSpec((BLOCK, B_W), lambda b, n: (b * N + n, 0))
    tab = lambda: pltpu.VMEM((BLOCK, B_W), F32)
    return pl.pallas_call(
        _ret_main_kernel, out_shape=jax.ShapeDtypeStruct((T, B_W), BF16), grid=(B, N),
        in_specs=[pl.BlockSpec(memory_space=pltpu.SMEM), pl.BlockSpec(lgl.shape, lambda b, n: (0, 0)),
                  blk(), blk(), blk(), blk(),
                  pl.BlockSpec((None, None, B_HEAD_DIM, B_W), lambda b, n: (b, n, 0, 0))],
        out_specs=blk(),
        scratch_shapes=[pltpu.VMEM((B_HEAD_DIM, B_W), F32), pltpu.VMEM((B_HEADS, BLOCK, BLOCK), F32),
                        tab(), tab(), tab(), tab()],
        compiler_params=_params(2), name="ret_main")(lg, lgl, bq, bk, bv, bg, rb)


def even_mixer_parts(x2, positions, g_mix, w_in, sink, decay_logit, B, S, tm=512):
    ca, sa = _rope_tables(positions, ROPE_THETA, A_HEAD_DIM, 0, ROPE_DIM)
    cb, sb = _rope_tables(positions, RET_THETA, B_HEAD_DIM, 0, B_HEAD_DIM)
    aq, akv, bq, bk, bv, bg = _inproj_even(x2, g_mix.reshape(1, -1), w_in.astype(BF16), (ca, sa, cb, sb), tm)
    o_a = _win_attn(aq, akv, sink.astype(F32), B, S)
    lg = jax.nn.log_sigmoid(decay_logit.astype(F32))
    lgl = jnp.repeat(lg, B_HEAD_DIM, axis=1)
    rb = _ret_bwd_state(lgl, bk, bv, B, S)
    o_b = _ret_main(lg, lgl, bq, bk, bv, bg, rb, B, S)
    return o_a, o_b


def _outproj_kernel(o1_ref, o2_ref, w_ref, x_ref, g_ref, wr_ref, x1_ref, hn_ref, aff_ref):
    half = o1_ref.shape[1]
    x1 = x_ref[...] + (_dot(o1_ref[...], w_ref[:half, :]) + _dot(o2_ref[...], w_ref[half:, :]))
    x1_ref[...] = x1
    hn = _rms(x1, g_ref[...])
    hn_ref[...] = hn
    logits = _dot_nt(wr_ref[...], hn.astype(BF16))
    e = jnp.exp(logits - jnp.max(logits, axis=0, keepdims=True))
    aff_ref[...] = e / jnp.sum(e, axis=0, keepdims=True)


def _outproj_router(o1, o2, w_out, x2, g_ffn, w_router, B, S, tm):
    T, D = x2.shape
    E = w_router.shape[1]
    per = S // tm
    row = lambda width: pl.BlockSpec((tm, width), lambda i: (i, 0))
    full = lambda a: pl.BlockSpec(a.shape, lambda i: (0, 0))
    w = w_out.astype(BF16)
    g = g_ffn.reshape(1, D)
    wr = w_router.T.astype(BF16)
    return pl.pallas_call(
        _outproj_kernel,
        out_shape=(jax.ShapeDtypeStruct((T, D), F32), jax.ShapeDtypeStruct((T, D), F32),
                   jax.ShapeDtypeStruct((B, E, S), F32)),
        grid=(T // tm,),
        in_specs=[row(o1.shape[1]), row(o2.shape[1]), full(w), row(D), full(g), full(wr)],
        out_specs=(row(D), row(D), pl.BlockSpec((None, E, tm), lambda i: (i // per, 0, i % per))),
        compiler_params=_params(1), name="outproj_router")(o1, o2, w, x2, g, wr)


def _split3(x):
    x1 = x.astype(BF16)
    r = x - x1.astype(F32)
    x2 = r.astype(BF16)
    x3 = (r - x2.astype(F32)).astype(BF16)
    return x1, x2, x3


def _topk_kernel(aff_ref, idx_ref, gate_ref, thr_ref, *, cap):
    E, R, _ = aff_ref.shape
    bits = lax.bitcast_convert_type(aff_ref[...], I32)

    def count(mask):
        return jnp.sum(jnp.sum(mask.astype(F32), axis=2, keepdims=True), axis=1, keepdims=True)

    def bit_body(i, prefix):
        cand = prefix | jnp.left_shift(jnp.int32(1), 30 - i)
        return jnp.where(count(bits >= cand) >= cap, cand, prefix)

    thr = lax.fori_loop(0, 31, bit_body, jnp.zeros((E, 1, 1), I32))
    thr_ref[...] = jnp.broadcast_to(thr, thr_ref.shape)

    li = lax.broadcasted_iota(I32, (LANES, LANES), 0)
    lj = lax.broadcasted_iota(I32, (LANES, LANES), 1)
    tri = (li <= lj).astype(BF16)
    ri = lax.broadcasted_iota(I32, (R, R), 0)
    rj = lax.broadcasted_iota(I32, (R, R), 1)
    below = (rj < ri).astype(BF16)
    slot = lax.broadcasted_iota(I32, (1, cap), 1).astype(F32)
    tok = (lax.broadcasted_iota(I32, (R, LANES), 0) * LANES
           + lax.broadcasted_iota(I32, (R, LANES), 1)).astype(F32)

    def prefix_counts(m):
        within = _dot(m.astype(BF16), tri)
        total = jnp.broadcast_to(within[:, LANES - 1:LANES], (R, LANES))
        before = _dot(below, total.astype(BF16))
        return within, total, before

    def expert(e, carry):
        a = aff_ref[e]
        b = lax.bitcast_convert_type(a, I32)
        t = thr_ref[e]
        gt, eq = b > t, b == t
        n_gt = jnp.sum(jnp.sum(gt.astype(F32), axis=1, keepdims=True), axis=0, keepdims=True)
        eqf = eq.astype(F32)
        within, _, before = prefix_counts(eqf)
        sel = gt | (eq & (before + within - eqf < cap - n_gt))
        self_ = sel.astype(F32)
        within, total, before = prefix_counts(self_)
        rank = before + within
        first, count_r = before[:, 0:1], total[:, 0:1]
        owner = ((first <= slot) & (slot < first + count_r)).astype(BF16)

        def row_of_slot(x):
            return sum(_dot_tn(p, owner) for p in _split3(x))

        hit = (row_of_slot(self_) > 0.5) & (row_of_slot(rank) == slot + 1.0)
        idx_ref[e] = jnp.sum(jnp.where(hit, row_of_slot(tok), 0.0), axis=0, keepdims=True).astype(I32)
        gate_ref[e] = jnp.sum(jnp.where(hit, row_of_slot(a), 0.0), axis=0, keepdims=True)
        return carry

    lax.fori_loop(0, E, expert, 0)


def _topk(aff, cap):
    B, E, S = aff.shape
    R = S // LANES
    aff4 = aff.reshape(B, E, R, LANES)
    out = jax.ShapeDtypeStruct((B, E, 1, cap), I32), jax.ShapeDtypeStruct((B, E, 1, cap), F32)
    spec = pl.BlockSpec((None, E, 1, cap), lambda b: (b, 0, 0, 0))
    return pl.pallas_call(
        functools.partial(_topk_kernel, cap=cap), out_shape=out, grid=(B,),
        in_specs=[pl.BlockSpec((None, E, R, LANES), lambda b: (b, 0, 0, 0))],
        out_specs=(spec, spec), scratch_shapes=[pltpu.VMEM((E, 1, LANES), I32)],
        compiler_params=_params(1), name="topk")(aff4)


def _moe_kernel(idx_ref, gate_ref, wg_ref, wu_ref, wd_ref, hn_hbm, x_alias, out_hbm,
                xin_buf, row_buf, sems, *, seq, cap, f_chunk):
    del x_alias
    base = pl.program_id(1) * seq

    def gather_copies(j):
        t = base + idx_ref[0, j]
        return (pltpu.make_async_copy(hn_hbm.at[pl.ds(t, 1)], xin_buf.at[pl.ds(j, 1)], sems.at[0]),
                pltpu.make_async_copy(out_hbm.at[pl.ds(t, 1)], row_buf.at[pl.ds(j, 1)], sems.at[1]))

    def scatter_copy(j):
        t = base + idx_ref[0, j]
        return pltpu.make_async_copy(row_buf.at[pl.ds(j, 1)], out_hbm.at[pl.ds(t, 1)], sems.at[2])

    @pl.loop(0, cap)
    def _(j):
        for c in gather_copies(j):
            c.start()

    @pl.loop(0, cap)
    def _(j):
        for c in gather_copies(j):
            c.wait()

    xin = xin_buf[...].astype(BF16)
    acc = jnp.zeros(row_buf.shape, F32)
    for f in range(wg_ref.shape[1] // f_chunk):
        fs = slice(f * f_chunk, (f + 1) * f_chunk)
        g = _dot(xin, wg_ref[:, fs])
        hid = (g * jax.nn.sigmoid(g) * _dot(xin, wu_ref[:, fs])).astype(BF16)
        acc = acc + _dot(hid, wd_ref[fs, :])
    diag = (lax.broadcasted_iota(I32, (cap, cap), 0) == lax.broadcasted_iota(I32, (cap, cap), 1))
    gate_col = jnp.sum(jnp.where(diag, gate_ref[...], 0.0), axis=1, keepdims=True)
    row_buf[...] = row_buf[...] + acc * gate_col

    @pl.loop(0, cap)
    def _(j):
        scatter_copy(j).start()

    @pl.loop(0, cap)
    def _(j):
        scatter_copy(j).wait()


def _moe_ffn(hn, x1, idx, gate, w_gate, w_up, w_down, B, S, f_chunk=512):
    T, D = x1.shape
    E, _, F = w_gate.shape
    cap = idx.shape[-1]
    f_chunk = min(f_chunk, F)
    wspec = lambda a: pl.BlockSpec((None,) + a.shape[1:], lambda e, b: (e, 0, 0))
    any_spec = pl.BlockSpec(memory_space=pl.ANY)
    return pl.pallas_call(
        functools.partial(_moe_kernel, seq=S, cap=cap, f_chunk=f_chunk),
        out_shape=jax.ShapeDtypeStruct((T, D), F32), grid=(E, B),
        in_specs=[pl.BlockSpec((None, None, 1, cap), lambda e, b: (b, e, 0, 0), memory_space=pltpu.SMEM),
                  pl.BlockSpec((None, None, 1, cap), lambda e, b: (b, e, 0, 0)),
                  wspec(w_gate), wspec(w_up), wspec(w_down), any_spec, any_spec],
        out_specs=any_spec,
        scratch_shapes=[pltpu.VMEM((cap, D), F32), pltpu.VMEM((cap, D), F32), pltpu.SemaphoreType.DMA((3,))],
        input_output_aliases={6: 0},
        compiler_params=_params(2), name="moe_ffn")(idx, gate, w_gate, w_up, w_down, hn, x1)


def moe_layer(o1, o2, w_out, x2, g_ffn, w_router, w_gate, w_up, w_down, B, S, tm=512):
    x1, hn, aff = _outproj_router(o1, o2, w_out, x2, g_ffn, w_router, B, S, tm)
    cap = EC_CAPACITY_FACTOR * S // w_router.shape[1]
    idx, gate = _topk(aff, cap)
    return _moe_ffn(hn, x1, idx, gate, w_gate.astype(BF16), w_up.astype(BF16), w_down.astype(BF16), B, S)


def _final_norm_kernel(x_ref, g_ref, o_ref):
    o_ref[...] = _rms(x_ref[...], g_ref[...])


def _final_norm(x2, g, tm):
    T, D = x2.shape
    row = pl.BlockSpec((tm, D), lambda i: (i, 0))
    return pl.pallas_call(
        _final_norm_kernel, out_shape=jax.ShapeDtypeStruct((T, D), F32), grid=(T // tm,),
        in_specs=[row, pl.BlockSpec((1, D), lambda i: (0, 0))], out_specs=row,
        compiler_params=_params(1), name="final_norm")(x2, g.reshape(1, D))


ODD_SPLITS = (C_Q_LORA, C_KV_LORA, C_ROPE, D_W, D_W, D_W, D_W, 4 * D_HEADS)
N_GATES = 4 * D_HEADS
I_FWD, F_FWD, I_BWD, F_BWD = 0, D_HEADS, 2 * D_HEADS, 3 * D_HEADS


def _odd_weights(w_in):
    cq, ckv, kr, dq, dk, dv, do, dg = jnp.split(w_in, [int(c) for c in np.cumsum(ODD_SPLITS)[:-1]], axis=1)
    D = w_in.shape[0]
    zeros = lambda n: jnp.zeros((D, n), w_in.dtype)
    kr_slot = jnp.concatenate([zeros(C_NOPE), kr, zeros(C_SLOT - C_NOPE - C_ROPE)], axis=1)
    dg_slot = jnp.concatenate([dg, zeros(LANES - N_GATES)], axis=1)
    w = jnp.concatenate([cq, ckv, kr_slot, dq, dk, dv, do, dg_slot], axis=1)
    return w.astype(BF16), dg.T.astype(BF16)


def _inproj_odd_kernel(x_ref, g_ref, w_ref, wgt_ref,
                       cq_ref, ckv_ref, kr_ref, dqk_ref, dv_ref, do_ref, gc_ref, gt_ref):
    h = _rms(x_ref[...], g_ref[...]).astype(BF16)
    off = 0
    for ref, width in ((cq_ref, C_Q_LORA), (ckv_ref, C_KV_LORA), (kr_ref, C_SLOT), (dqk_ref, 2 * D_W),
                       (dv_ref, D_W), (do_ref, D_W), (gc_ref, LANES)):
        ref[...] = _dot(h, w_ref[:, off:off + width]).astype(ref.dtype)
        off += width
    gt_ref[...] = _dot_nt(wgt_ref[...], h)


def _inproj_odd(x2, g, w, wgt, tm):
    T, D = x2.shape
    row = lambda width: pl.BlockSpec((tm, width), lambda i: (i, 0))
    full = lambda a: pl.BlockSpec(a.shape, lambda i: (0, 0))
    widths = (C_Q_LORA, C_KV_LORA, C_SLOT, 2 * D_W, D_W, D_W, LANES)
    dtypes = (F32, F32, F32, F32, BF16, F32, F32)
    out_shape = tuple(jax.ShapeDtypeStruct((T, wd), dt) for wd, dt in zip(widths, dtypes))
    out_shape += (jax.ShapeDtypeStruct((N_GATES, T), F32),)
    return pl.pallas_call(
        _inproj_odd_kernel, out_shape=out_shape, grid=(T // tm,),
        in_specs=[row(D), full(g), full(w), full(wgt)],
        out_specs=tuple(row(wd) for wd in widths) + (pl.BlockSpec((N_GATES, tm), lambda i: (0, i)),),
        compiler_params=_params(1), name="inproj_odd")(x2, g, w, wgt)


def _mla_prep_kernel(cq_ref, ckv_ref, kr_ref, nq_ref, nkv_ref, wq_ref, wk_ref, wv_ref, cos_ref, sin_ref,
                     q_ref, k_ref, v_ref):
    lane = lax.broadcasted_iota(I32, (1, LANES), 1)
    first = lane < C_NOPE + C_ROPE // 2
    cos, sin = cos_ref[...], sin_ref[...]
    rope = lambda z: _rope_slab(z, cos, sin, C_ROPE // 2, first)
    q = _dot(_rms(cq_ref[...], nq_ref[...]).astype(BF16), wq_ref[...])
    hkv = _rms(ckv_ref[...], nkv_ref[...]).astype(BF16)
    kn = _dot(hkv, wk_ref[...])
    kr = rope(kr_ref[...])
    for hh in range(C_HEADS):
        slab = slice(hh * C_SLOT, (hh + 1) * C_SLOT)
        q_ref[:, slab] = rope(q[:, slab]).astype(BF16)
        k_ref[:, slab] = (kn[:, slab] + kr).astype(BF16)
    v_ref[...] = _dot(hkv, wv_ref[...]).astype(BF16)


def _mla_prep(cq, ckv, kr, nq, nkv, w_uq, w_ukv, cos, sin, tm):
    T = cq.shape[0]
    pad_q = C_SLOT - C_NOPE - C_ROPE
    wq = jnp.pad(w_uq.reshape(C_Q_LORA, C_HEADS, C_NOPE + C_ROPE), ((0, 0), (0, 0), (0, pad_q)))
    wq = wq.reshape(C_Q_LORA, C_HEADS * C_SLOT).astype(BF16)
    wkv = w_ukv.reshape(C_KV_LORA, C_HEADS, C_NOPE + C_V)
    wk = jnp.pad(wkv[:, :, :C_NOPE], ((0, 0), (0, 0), (0, C_SLOT - C_NOPE)))
    wk = wk.reshape(C_KV_LORA, C_HEADS * C_SLOT).astype(BF16)
    wv = wkv[:, :, C_NOPE:].reshape(C_KV_LORA, C_HEADS * C_V).astype(BF16)
    nq, nkv = nq.reshape(1, -1), nkv.reshape(1, -1)
    row = lambda width: pl.BlockSpec((tm, width), lambda i: (i, 0))
    full = lambda a: pl.BlockSpec(a.shape, lambda i: (0, 0))
    out_shape = (jax.ShapeDtypeStruct((T, C_HEADS * C_SLOT), BF16), jax.ShapeDtypeStruct((T, C_HEADS * C_SLOT), BF16),
                 jax.ShapeDtypeStruct((T, C_HEADS * C_V), BF16))
    return pl.pallas_call(
        _mla_prep_kernel, out_shape=out_shape, grid=(T // tm,),
        in_specs=[row(C_Q_LORA), row(C_KV_LORA), row(C_SLOT), full(nq), full(nkv), full(wq), full(wk), full(wv),
                  row(LANES), row(LANES)],
        out_specs=(row(C_HEADS * C_SLOT), row(C_HEADS * C_SLOT), row(C_HEADS * C_V)),
        compiler_params=_params(1), name="mla_prep")(cq, ckv, kr, nq, nkv, wq, wk, wv, cos, sin)


C_PAIR = 2


def _mla_attn_kernel(q_ref, k_ref, v_ref, o_ref, *, key_chunk):
    tq, seq = q_ref.shape[0], k_ref.shape[0]
    scale = (C_NOPE + C_ROPE) ** -0.5
    for hh in range(C_PAIR):
        q = q_ref[:, hh * C_SLOT:(hh + 1) * C_SLOT]

        def body(c, carry):
            m, l, acc = carry
            rows = pl.ds(pl.multiple_of(c * key_chunk, key_chunk), key_chunk)
            s = _dot_nt(q, k_ref[rows, hh * C_SLOT:(hh + 1) * C_SLOT]) * scale
            m_new = jnp.maximum(m, jnp.max(s, axis=-1, keepdims=True))
            alpha = jnp.exp(m - m_new)
            p = jnp.exp(s - m_new)
            l = alpha * l + jnp.sum(p, axis=-1, keepdims=True)
            acc = alpha * acc + _dot(p.astype(BF16), v_ref[rows, hh * C_V:(hh + 1) * C_V])
            return m_new, l, acc

        init = (jnp.full((tq, 1), NEG, F32), jnp.zeros((tq, 1), F32), jnp.zeros((tq, C_V), F32))
        _, l, acc = lax.fori_loop(0, seq // key_chunk, body, init)
        o_ref[:, hh * C_V:(hh + 1) * C_V] = (acc / l).astype(BF16)


def _mla_attn(q, k, v, B, S, tq, key_chunk):
    T = B * S
    nq = S // tq
    return pl.pallas_call(
        functools.partial(_mla_attn_kernel, key_chunk=key_chunk),
        out_shape=jax.ShapeDtypeStruct((T, C_HEADS * C_V), BF16), grid=(B, C_HEADS // C_PAIR, nq),
        in_specs=[pl.BlockSpec((tq, C_PAIR * C_SLOT), lambda b, p, i: (b * nq + i, p)),
                  pl.BlockSpec((S, C_PAIR * C_SLOT), lambda b, p, i: (b, p)),
                  pl.BlockSpec((S, C_PAIR * C_V), lambda b, p, i: (b, p))],
        out_specs=pl.BlockSpec((tq, C_PAIR * C_V), lambda b, p, i: (b * nq + i, p)),
        compiler_params=_params(3), name="mla_attn")(q, k, v)


HALO = 8


def _conv_kernel(xp_ref, x_ref, xn_ref, w_ref, o_ref, *, n_tiles):
    j = pl.program_id(1)
    tc = x_ref.shape[0]
    prev = jnp.where(j > 0, xp_ref[...], 0.0)
    nxt = jnp.where(j < n_tiles - 1, xn_ref[...], 0.0)
    ext = jnp.concatenate([prev, x_ref[...], nxt], axis=0)
    rows = tc + 2 * HALO
    y = jnp.zeros(x_ref.shape, F32)
    for w in range(D_CONV):
        first = HALO - D_CONV // 2 + w
        y = y + pltpu.roll(ext, (rows - first) % rows, 0)[:tc] * w_ref[w:w + 1, :]
    y = y * jax.nn.sigmoid(y)
    o_ref[:, :D_W] = y[:, :D_W]
    o_ref[:, D_W:] = y[:, D_W:] * (D_HEAD_DIM ** -0.5)


def _conv_prep(dqk, conv_w, B, S, tc):
    T, C = dqk.shape
    n_tiles = S // tc
    per, last = tc // HALO, T // HALO - 1
    cur = lambda b, j: (b * n_tiles + j, 0)
    return pl.pallas_call(
        functools.partial(_conv_kernel, n_tiles=n_tiles),
        out_shape=jax.ShapeDtypeStruct((T, C), F32), grid=(B, n_tiles),
        in_specs=[pl.BlockSpec((HALO, C), lambda b, j: (jnp.maximum((b * n_tiles + j) * per - 1, 0), 0)),
                  pl.BlockSpec((tc, C), cur),
                  pl.BlockSpec((HALO, C), lambda b, j: (jnp.minimum((b * n_tiles + j + 1) * per, last), 0)),
                  pl.BlockSpec(conv_w.shape, lambda b, j: (0, 0))],
        out_specs=pl.BlockSpec((tc, C), cur),
        compiler_params=_params(2), name="conv_prep")(dqk, dqk, dqk, conv_w)


def _log_sigmoid(x):
    return -(jnp.maximum(-x, 0.0) + jnp.log1p(jnp.exp(-jnp.abs(x))))


def _tri(lower):
    a = lax.broadcasted_iota(I32, (BLOCK, BLOCK), 0)
    b = lax.broadcasted_iota(I32, (BLOCK, BLOCK), 1)
    return ((b <= a) if lower else (b >= a)).astype(BF16)


def _mlstm_update(s_ref, m_ref, h, g, a, k, v_ones):
    a_max = jnp.max(a, axis=0, keepdims=True)
    w = jnp.exp(a - a_max)
    upd = _dot_tn((k * w).astype(BF16), v_ones)
    m = m_ref[h:h + 1, 0:1]
    m_new = jnp.maximum(g + m, a_max)
    cols = slice(h * 2 * D_HEAD_DIM, (h + 1) * 2 * D_HEAD_DIM)
    s_ref[:, cols] = jnp.exp(g + m - m_new) * s_ref[:, cols] + jnp.exp(a_max - m_new) * upd
    m_ref[h:h + 1, :] = jnp.broadcast_to(m_new, (1, LANES))


def _mlstm_bwd_state_kernel(bias_c_ref, qk_ref, v_ref, gc_ref, s_out_ref, m_out_ref, s_ref, m_ref):
    @pl.when(pl.program_id(1) == 0)
    def _():
        s_ref[...] = jnp.zeros_like(s_ref)
        m_ref[...] = jnp.zeros_like(m_ref)

    s_out_ref[...] = s_ref[...]
    m_out_ref[...] = m_ref[...]
    gc = gc_ref[...] + bias_c_ref[...]
    suffix = sum(_dot(_tri(False), p) for p in _split3(_log_sigmoid(gc)))
    ones = jnp.ones((BLOCK, D_HEAD_DIM), BF16)
    for h in range(D_HEADS):
        hs = slice(h * D_HEAD_DIM, (h + 1) * D_HEAD_DIM)
        sb = suffix[:, F_BWD + h:F_BWD + h + 1]
        g = sb[0:1, :]
        a = g - sb + gc[:, I_BWD + h:I_BWD + h + 1]
        k = qk_ref[:, D_W + h * D_HEAD_DIM:D_W + (h + 1) * D_HEAD_DIM]
        _mlstm_update(s_ref, m_ref, h, g, a, k, jnp.concatenate([v_ref[:, hs], ones], axis=1))


def _mlstm_bwd_state(bias_c, qk, dv, gc, B, S):
    N = S // BLOCK
    rev = lambda b, n: (b * N + N - 1 - n, 0)
    out_shape = (jax.ShapeDtypeStruct((B, N, D_HEAD_DIM, 2 * D_W), F32), jax.ShapeDtypeStruct((B, N, 8, LANES), F32))
    return pl.pallas_call(
        _mlstm_bwd_state_kernel, out_shape=out_shape, grid=(B, N),
        in_specs=[pl.BlockSpec(bias_c.shape, lambda b, n: (0, 0)), pl.BlockSpec((BLOCK, 2 * D_W), rev),
                  pl.BlockSpec((BLOCK, D_W), rev), pl.BlockSpec((BLOCK, LANES), rev)],
        out_specs=(pl.BlockSpec((None, None, D_HEAD_DIM, 2 * D_W), lambda b, n: (b, N - 1 - n, 0, 0)),
                   pl.BlockSpec((None, None, 8, LANES), lambda b, n: (b, N - 1 - n, 0, 0))),
        scratch_shapes=[pltpu.VMEM((D_HEAD_DIM, 2 * D_W), F32), pltpu.VMEM((8, LANES), F32)],
        compiler_params=_params(2), name="mlstm_bwd_state")(bias_c, qk, dv, gc)


def _mlstm_main_kernel(bias_c_ref, bias_r_ref, qk_ref, v_ref, og_ref, gc_ref, gt_ref, sb_ref, mb_ref,
                       o_ref, s_ref, m_ref):
    @pl.when(pl.program_id(1) == 0)
    def _():
        s_ref[...] = jnp.zeros_like(s_ref)
        m_ref[...] = jnp.zeros_like(m_ref)

    gc = gc_ref[...] + bias_c_ref[...]
    gr = gt_ref[...] + bias_r_ref[...]
    lower, upper = _tri(True), _tri(False)
    lfc, lfr = _split3(_log_sigmoid(gc)), _split3(_log_sigmoid(gr))
    pre_c = sum(_dot(lower, p) for p in lfc)
    suf_c = sum(_dot(upper, p) for p in lfc)
    pre_r = sum(_dot(p, upper) for p in lfr)
    suf_r = sum(_dot(p, lower) for p in lfr)
    ti = lax.broadcasted_iota(I32, (BLOCK, BLOCK), 0)
    tj = lax.broadcasted_iota(I32, (BLOCK, BLOCK), 1)
    ones = jnp.ones((BLOCK, D_HEAD_DIM), BF16)
    og = og_ref[...]
    for h in range(D_HEADS):
        hs = slice(h * D_HEAD_DIM, (h + 1) * D_HEAD_DIM)
        cols = slice(h * 2 * D_HEAD_DIM, (h + 1) * 2 * D_HEAD_DIM)
        k = qk_ref[:, D_W + h * D_HEAD_DIM:D_W + (h + 1) * D_HEAD_DIM]
        qb, v = qk_ref[:, hs].astype(BF16), v_ref[:, hs]
        qkt = _dot_nt(qb, k.astype(BF16))

        def direction(b_col, b_row, i_row, mask, state, m_prev):
            logd = jnp.where(mask, b_col - b_row + i_row, NEG)
            log_inter = b_col + m_prev
            m_t = jnp.maximum(jnp.max(logd, axis=1, keepdims=True), log_inter)
            s = qkt * jnp.exp(logd - m_t)
            inter_w = jnp.exp(log_inter - m_t)
            qs = _dot(qb, state.astype(BF16))
            num = _dot(s.astype(BF16), v) + inter_w * qs[:, :D_HEAD_DIM]
            den = jnp.sum(s, axis=1, keepdims=True) + inter_w * qs[:, D_HEAD_DIM:D_HEAD_DIM + 1]
            return num / jnp.maximum(jnp.abs(den), jnp.exp(-m_t))

        f, bk = F_FWD + h, F_BWD + h
        h_f = direction(pre_c[:, f:f + 1], pre_r[f:f + 1, :], gr[I_FWD + h:I_FWD + h + 1, :], tj <= ti,
                        s_ref[:, cols], m_ref[h:h + 1, 0:1])
        h_b = direction(suf_c[:, bk:bk + 1], suf_r[bk:bk + 1, :], gr[I_BWD + h:I_BWD + h + 1, :], tj > ti,
                        sb_ref[:, cols], mb_ref[h:h + 1, 0:1])
        o_ref[:, hs] = (jax.nn.sigmoid(og[:, hs]) * (h_f + h_b)).astype(BF16)
        b_col = pre_c[:, f:f + 1]
        g = b_col[BLOCK - 1:BLOCK, :]
        a = g - b_col + gc[:, I_FWD + h:I_FWD + h + 1]
        _mlstm_update(s_ref, m_ref, h, g, a, k, jnp.concatenate([v, ones], axis=1))


def _mlstm_main(bias_c, bias_r, qk, dv, og, gc, gt, sb, mb, B, S):
    N = S // BLOCK
    T = B * S
    cur = lambda b, n: (b * N + n, 0)
    return pl.pallas_call(
        _mlstm_main_kernel, out_shape=jax.ShapeDtypeStruct((T, D_W), BF16), grid=(B, N),
        in_specs=[pl.BlockSpec(bias_c.shape, lambda b, n: (0, 0)), pl.BlockSpec(bias_r.shape, lambda b, n: (0, 0)),
                  pl.BlockSpec((BLOCK, 2 * D_W), cur), pl.BlockSpec((BLOCK, D_W), cur), pl.BlockSpec((BLOCK, D_W), cur),
                  pl.BlockSpec((BLOCK, LANES), cur), pl.BlockSpec((N_GATES, BLOCK), lambda b, n: (0, b * N + n)),
                  pl.BlockSpec((None, None, D_HEAD_DIM, 2 * D_W), lambda b, n: (b, n, 0, 0)),
                  pl.BlockSpec((None, None, 8, LANES), lambda b, n: (b, n, 0, 0))],
        out_specs=pl.BlockSpec((BLOCK, D_W), cur),
        scratch_shapes=[pltpu.VMEM((D_HEAD_DIM, 2 * D_W), F32), pltpu.VMEM((8, LANES), F32)],
        compiler_params=_params(2), name="mlstm_main")(bias_c, bias_r, qk, dv, og, gc, gt, sb, mb)


def odd_mixer_parts(x2, positions, g_mix, w_in, norm_q, norm_kv, w_uq, w_ukv, conv_w, gate_bias, B, S,
                    tm=512, tq=512, key_chunk=1024):
    w, wgt = _odd_weights(w_in)
    cq, ckv, kr, dqk, dv, og, gc, gt = _inproj_odd(x2, g_mix.reshape(1, -1), w, wgt, tm)
    cos, sin = _rope_tables(positions, MLA_THETA, C_SLOT, C_NOPE, C_ROPE)
    q, k, v = _mla_prep(cq, ckv, kr, norm_q, norm_kv, w_uq, w_ukv, cos, sin, tm)
    o_c = _mla_attn(q, k, v, B, S, min(tq, S), min(key_chunk, S))
    qk = _conv_prep(dqk, conv_w.astype(F32), B, S, min(tm, S))
    bias = gate_bias.astype(F32).reshape(1, N_GATES)
    bias_c = jnp.pad(bias, ((0, 0), (0, LANES - N_GATES)))
    bias_r = bias.reshape(N_GATES, 1)
    sb, mb = _mlstm_bwd_state(bias_c, qk, dv, gc, B, S)
    o_d = _mlstm_main(bias_c, bias_r, qk, dv, og, gc, gt, sb, mb, B, S)
    return o_c, o_d


def kernel(x, positions, norm_mix, norm_ffn, norm_final, ev_w_in, ev_w_out, attn_sink, ret_decay_logit,
           od_w_in, od_w_out, mla_norm_q, mla_norm_kv, mla_w_uq, mla_w_ukv, mlstm_conv, mlstm_gate_bias,
           moe_router, moe_w_gate, moe_w_up, moe_w_down):
    B, S, D = x.shape
    tm = min(512, S)
    x2 = x.reshape(B * S, D)
    for layer in range(norm_mix.shape[0]):
        j = layer // 2
        if layer % 2 == 0:
            o1, o2 = even_mixer_parts(x2, positions, norm_mix[layer], ev_w_in[j], attn_sink[j],
                                      ret_decay_logit[j], B, S, tm)
            w_out = ev_w_out[j]
        else:
            o1, o2 = odd_mixer_parts(x2, positions, norm_mix[layer], od_w_in[j], mla_norm_q[j], mla_norm_kv[j],
                                     mla_w_uq[j], mla_w_ukv[j], mlstm_conv[j], mlstm_gate_bias[j], B, S, tm)
            w_out = od_w_out[j]
        x2 = moe_layer(o1, o2, w_out, x2, norm_ffn[layer], moe_router[layer], moe_w_gate[layer],
                       moe_w_up[layer], moe_w_down[layer], B, S, tm)
    return _final_norm(x2, norm_final, tm).reshape(B, S, D)
```

```python
import functools

import jax
import jax.numpy as jnp
import numpy as np
from jax import lax
from jax.experimental import pallas as pl
from jax.experimental.pallas import tpu as pltpu

F32 = jnp.float32
BF16 = jnp.bfloat16
I32 = jnp.int32

LANES = 128
BLOCK = 128
RMS_EPS = 1e-6
GN_EPS = 1e-5
NEG = -1e30
VMEM_LIMIT = 56 * 1024 * 1024

A_HEADS, A_KV_HEADS, A_HEAD_DIM = 8, 2, 64
ROPE_THETA = 500000.0
ROPE_DIM = A_HEAD_DIM // 4
B_HEADS, B_HEAD_DIM = 8, 64
RET_THETA = 10000.0
C_HEADS, C_NOPE, C_ROPE, C_V = 8, 64, 32, 64
C_Q_LORA, C_KV_LORA = 512, 256
MLA_THETA = 10000.0
D_HEADS, D_HEAD_DIM, D_CONV = 4, 128, 5
N_EXPERTS = 16
EC_CAPACITY_FACTOR = 2

A_Q_W = A_HEADS * A_HEAD_DIM
A_KV_W = A_KV_HEADS * A_HEAD_DIM
B_W = B_HEADS * B_HEAD_DIM
D_W = D_HEADS * D_HEAD_DIM
C_SLOT = 128


def _params(n_axes, vmem=VMEM_LIMIT):
    return pltpu.CompilerParams(dimension_semantics=("arbitrary",) * n_axes,
                                vmem_limit_bytes=vmem)


def _rms(x, g):
    return x * lax.rsqrt(jnp.mean(x * x, axis=-1, keepdims=True) + RMS_EPS) * g


def _dot(a, b):
    return jnp.dot(a, b, preferred_element_type=F32)


def _dot_nt(a, b):
    return lax.dot_general(a, b, (((1,), (1,)), ((), ())), preferred_element_type=F32)


def _dot_tn(a, b):
    return lax.dot_general(a, b, (((0,), (0,)), ((), ())), preferred_element_type=F32)


def _rope_slab(z, cos, sin, half, first_half):
    partner = jnp.where(first_half, pltpu.roll(z, LANES - half, 1), pltpu.roll(z, half, 1))
    return z * cos + partner * sin


def _rope_tables(positions, theta, head_dim, rot_start, rot_dim):
    half = rot_dim // 2
    assert head_dim % half == 0 and rot_start % half == 0
    inv_freq = theta ** (-jnp.arange(half, dtype=F32) * 2.0 / rot_dim)
    d = np.arange(LANES) % head_dim - rot_start
    rot = (d >= 0) & (d < rot_dim)
    sign = np.where(d < half, -1.0, 1.0).astype(np.float32)
    ang = positions.astype(F32).reshape(-1, 1) * inv_freq
    ang = jnp.tile(ang, (1, LANES // half))
    cos = jnp.where(rot[None, :], jnp.cos(ang), 1.0)
    sin = jnp.where(rot[None, :], jnp.sin(ang) * sign[None, :], 0.0)
    return cos, sin


def _inproj_even_kernel(x_ref, g_ref, w_ref, ca_ref, sa_ref, cb_ref, sb_ref,
                        aq_ref, akv_ref, bq_ref, bk_ref, bv_ref, bg_ref):
    h = _rms(x_ref[...], g_ref[...]).astype(BF16)
    lane = lax.broadcasted_iota(I32, (1, LANES), 1)
    first_a = (lane % A_HEAD_DIM) < (ROPE_DIM // 2)
    first_b = (lane % B_HEAD_DIM) < (B_HEAD_DIM // 2)
    ca, sa, cb, sb = ca_ref[...], sa_ref[...], cb_ref[...], sb_ref[...]

    def rope_a(z):
        return _rope_slab(z, ca, sa, ROPE_DIM // 2, first_a)

    def rope_b(z):
        return _rope_slab(z, cb, sb, B_HEAD_DIM // 2, first_b)

    off = 0
    z = _dot(h, w_ref[:, off:off + A_Q_W])
    for s in range(A_Q_W // LANES):
        aq_ref[:, s * LANES:(s + 1) * LANES] = rope_a(z[:, s * LANES:(s + 1) * LANES]).astype(BF16)
    off += A_Q_W
    z = _dot(h, w_ref[:, off:off + 2 * A_KV_W])
    akv_ref[:, :A_KV_W] = rope_a(z[:, :A_KV_W]).astype(BF16)
    akv_ref[:, A_KV_W:] = z[:, A_KV_W:].astype(BF16)
    off += 2 * A_KV_W
    z = _dot(h, w_ref[:, off:off + B_W])
    for s in range(B_W // LANES):
        bq_ref[:, s * LANES:(s + 1) * LANES] = rope_b(z[:, s * LANES:(s + 1) * LANES])
    off += B_W
    z = _dot(h, w_ref[:, off:off + B_W])
    for s in range(B_W // LANES):
        bk_ref[:, s * LANES:(s + 1) * LANES] = rope_b(z[:, s * LANES:(s + 1) * LANES]) * (B_HEAD_DIM ** -0.5)
    off += B_W
    bv_ref[...] = _dot(h, w_ref[:, off:off + B_W]).astype(BF16)
    off += B_W
    bg_ref[...] = _dot(h, w_ref[:, off:off + B_W])


def _inproj_even(x2, g, w, tabs, tm):
    T, D = x2.shape
    ncol = w.shape[1]
    row = lambda width: pl.BlockSpec((tm, width), lambda i: (i, 0))
    full = lambda a: pl.BlockSpec(a.shape, lambda i: (0, 0))
    out_shape = (jax.ShapeDtypeStruct((T, A_Q_W), BF16), jax.ShapeDtypeStruct((T, 2 * A_KV_W), BF16),
                 jax.ShapeDtypeStruct((T, B_W), F32), jax.ShapeDtypeStruct((T, B_W), F32),
                 jax.ShapeDtypeStruct((T, B_W), BF16), jax.ShapeDtypeStruct((T, B_W), F32))
    return pl.pallas_call(
        _inproj_even_kernel, out_shape=out_shape, grid=(T // tm,),
        in_specs=[row(D), full(g), pl.BlockSpec((D, ncol), lambda i: (0, 0))] + [row(LANES)] * 4,
        out_specs=(row(A_Q_W), row(2 * A_KV_W), row(B_W), row(B_W), row(B_W), row(B_W)),
        compiler_params=_params(1), name="inproj_even")(x2, g, w, *tabs)


def _win_attn_kernel(sink_ref, q_ref, kvp_ref, kvc_ref, kvn_ref, o_ref, *, n_blocks):
    n = pl.program_id(1)
    group = A_HEADS // A_KV_HEADS
    rows = group * BLOCK
    qi = lax.broadcasted_iota(I32, (rows, BLOCK), 0) % BLOCK
    kj = lax.broadcasted_iota(I32, (rows, BLOCK), 1)
    ok_prev = (kj >= qi) & (n > 0)
    ok_next = (kj <= qi) & (n < n_blocks - 1)
    q = q_ref[...]
    scale = A_HEAD_DIM ** -0.5
    for g in range(A_KV_HEADS):
        ks = slice(g * A_HEAD_DIM, (g + 1) * A_HEAD_DIM)
        vs = slice(A_KV_W + g * A_HEAD_DIM, A_KV_W + (g + 1) * A_HEAD_DIM)
        qg = jnp.concatenate([q[:, (g * group + i) * A_HEAD_DIM:(g * group + i + 1) * A_HEAD_DIM]
                              for i in range(group)], axis=0)
        sink = jnp.concatenate([jnp.full((BLOCK, 1), sink_ref[g * group + i], F32)
                                for i in range(group)], axis=0)
        sp = jnp.where(ok_prev, _dot_nt(qg, kvp_ref[:, ks]) * scale, NEG)
        sc = _dot_nt(qg, kvc_ref[:, ks]) * scale
        sn = jnp.where(ok_next, _dot_nt(qg, kvn_ref[:, ks]) * scale, NEG)
        m = jnp.maximum(jnp.maximum(jnp.max(sp, -1, keepdims=True), jnp.max(sc, -1, keepdims=True)),
                        jnp.maximum(jnp.max(sn, -1, keepdims=True), sink))
        pp, pc, pn = jnp.exp(sp - m), jnp.exp(sc - m), jnp.exp(sn - m)
        den = (jnp.sum(pp, -1, keepdims=True) + jnp.sum(pc, -1, keepdims=True)
               + jnp.sum(pn, -1, keepdims=True) + jnp.exp(sink - m))
        o = (_dot(pp.astype(BF16), kvp_ref[:, vs]) + _dot(pc.astype(BF16), kvc_ref[:, vs])
             + _dot(pn.astype(BF16), kvn_ref[:, vs])) / den
        for i in range(group):
            hh = g * group + i
            o_ref[:, hh * A_HEAD_DIM:(hh + 1) * A_HEAD_DIM] = o[i * BLOCK:(i + 1) * BLOCK].astype(BF16)


def _win_attn(aq, akv, sink, B, S):
    N = S // BLOCK
    T = B * S
    kv = lambda shift: pl.BlockSpec(
        (BLOCK, 2 * A_KV_W), lambda b, n: (b * N + jnp.clip(n + shift, 0, N - 1), 0))
    return pl.pallas_call(
        functools.partial(_win_attn_kernel, n_blocks=N),
        out_shape=jax.ShapeDtypeStruct((T, A_Q_W), BF16), grid=(B, N),
        in_specs=[pl.BlockSpec(memory_space=pltpu.SMEM),
                  pl.BlockSpec((BLOCK, A_Q_W), lambda b, n: (b * N + n, 0)), kv(-1), kv(0), kv(1)],
        out_specs=pl.BlockSpec((BLOCK, A_Q_W), lambda b, n: (b * N + n, 0)),
        compiler_params=_params(2), name="win_attn")(sink, aq, akv, akv, akv)


def _ret_tables(lg_ref, lgl_ref, d_ref, xif_ref, xib_ref, zf_ref, zb_ref):
    i = lax.broadcasted_iota(I32, (BLOCK, BLOCK), 0)
    j = lax.broadcasted_iota(I32, (BLOCK, BLOCK), 1)
    rel = (i - j).astype(F32)
    for h in range(B_HEADS):
        d_ref[h] = jnp.where(i >= j, jnp.exp(rel * lg_ref[0, h]), jnp.exp(-rel * lg_ref[1, h]))
    t = lax.broadcasted_iota(I32, (BLOCK, B_W), 0).astype(F32)
    lgf, lgb = lgl_ref[0:1, :], lgl_ref[1:2, :]
    xif_ref[...] = jnp.exp((t + 1.0) * lgf)
    zf_ref[...] = jnp.exp((BLOCK - 1.0 - t) * lgf)
    xib_ref[...] = jnp.exp((BLOCK - t) * lgb)
    zb_ref[...] = jnp.exp(t * lgb)


def _ret_bwd_state_kernel(lgl_ref, k_ref, v_ref, r_out_ref, r_ref):
    n = pl.program_id(1)

    @pl.when(n == 0)
    def _():
        r_ref[...] = jnp.zeros_like(r_ref)

    r_out_ref[...] = r_ref[...]
    lgb = lgl_ref[1:2, :]
    t = lax.broadcasted_iota(I32, (BLOCK, B_W), 0).astype(F32)
    kz = (k_ref[...] * jnp.exp(t * lgb)).astype(BF16)
    cd = jnp.exp(BLOCK * lgb)
    v = v_ref[...]
    for h in range(B_HEADS):
        hs = slice(h * B_HEAD_DIM, (h + 1) * B_HEAD_DIM)
        r_ref[:, hs] = cd[:, hs] * r_ref[:, hs] + _dot_tn(kz[:, hs], v[:, hs])


def _ret_bwd_state(lgl, bk, bv, B, S):
    N = S // BLOCK
    blk = lambda: pl.BlockSpec((BLOCK, B_W), lambda b, n: (b * N + N - 1 - n, 0))
    return pl.pallas_call(
        _ret_bwd_state_kernel,
        out_shape=jax.ShapeDtypeStruct((B, N, B_HEAD_DIM, B_W), F32), grid=(B, N),
        in_specs=[pl.BlockSpec(lgl.shape, lambda b, n: (0, 0)), blk(), blk()],
        out_specs=pl.BlockSpec((None, None, B_HEAD_DIM, B_W), lambda b, n: (b, N - 1 - n, 0, 0)),
        scratch_shapes=[pltpu.VMEM((B_HEAD_DIM, B_W), F32)],
        compiler_params=_params(2), name="ret_bwd_state")(lgl, bk, bv)


def _ret_main_kernel(lg_ref, lgl_ref, q_ref, k_ref, v_ref, g_ref, rb_ref, o_ref,
                     rf_ref, d_ref, xif_ref, xib_ref, zf_ref, zb_ref):
    b, n = pl.program_id(0), pl.program_id(1)

    @pl.when((b == 0) & (n == 0))
    def _():
        _ret_tables(lg_ref, lgl_ref, d_ref, xif_ref, xib_ref, zf_ref, zb_ref)

    @pl.when(n == 0)
    def _():
        rf_ref[...] = jnp.zeros_like(rf_ref)

    q, k, v, gate = q_ref[...], k_ref[...], v_ref[...], g_ref[...]
    qb, kb = q.astype(BF16), k.astype(BF16)
    qxf = (q * xif_ref[...]).astype(BF16)
    qxb = (q * xib_ref[...]).astype(BF16)
    kzf = (k * zf_ref[...]).astype(BF16)
    cdf = jnp.exp(BLOCK * lgl_ref[0:1, :])
    for h in range(B_HEADS):
        hs = slice(h * B_HEAD_DIM, (h + 1) * B_HEAD_DIM)
        s = _dot_nt(qb[:, hs], kb[:, hs]) * d_ref[h]
        y = (_dot(s.astype(BF16), v[:, hs])
             + _dot(qxf[:, hs], rf_ref[:, hs].astype(BF16))
             + _dot(qxb[:, hs], rb_ref[:, hs].astype(BF16)))
        yc = y - jnp.mean(y, axis=-1, keepdims=True)
        yn = yc * lax.rsqrt(jnp.mean(yc * yc, axis=-1, keepdims=True) + GN_EPS)
        gh = gate[:, hs]
        o_ref[:, hs] = (gh * jax.nn.sigmoid(gh) * yn).astype(BF16)
        rf_ref[:, hs] = cdf[:, hs] * rf_ref[:, hs] + _dot_tn(kzf[:, hs], v[:, hs])


def _ret_main(lg, lgl, bq, bk, bv, bg, rb, B, S):
    N = S // BLOCK
    T = B * S
    blk = lambda: pl.BlockSpec((BLOCK, B_W), lambda b, n: (b * N + n, 0))
    tab = lambda: pltpu.VMEM((BLOCK, B_W), F32)
    return pl.pallas_call(
        _ret_main_kernel, out_shape=jax.ShapeDtypeStruct((T, B_W), BF16), grid=(B, N),
        in_specs=[pl.BlockSpec(memory_space=pltpu.SMEM), pl.BlockSpec(lgl.shape, lambda b, n: (0, 0)),
                  blk(), blk(), blk(), blk(),
                  pl.BlockSpec((None, None, B_HEAD_DIM, B_W), lambda b, n: (b, n, 0, 0))],
        out_specs=blk(),
        scratch_shapes=[pltpu.VMEM((B_HEAD_DIM, B_W), F32), pltpu.VMEM((B_HEADS, BLOCK, BLOCK), F32),
                        tab(), tab(), tab(), tab()],
        compiler_params=_params(2), name="ret_main")(lg, lgl, bq, bk, bv, bg, rb)


def even_mixer_parts(x2, positions, g_mix, w_in, sink, decay_logit, B, S, tm=512):
    ca, sa = _rope_tables(positions, ROPE_THETA, A_HEAD_DIM, 0, ROPE_DIM)
    cb, sb = _rope_tables(positions, RET_THETA, B_HEAD_DIM, 0, B_HEAD_DIM)
    aq, akv, bq, bk, bv, bg = _inproj_even(x2, g_mix.reshape(1, -1), w_in.astype(BF16), (ca, sa, cb, sb), tm)
    o_a = _win_attn(aq, akv, sink.astype(F32), B, S)
    lg = jax.nn.log_sigmoid(decay_logit.astype(F32))
    lgl = jnp.repeat(lg, B_HEAD_DIM, axis=1)
    rb = _ret_bwd_state(lgl, bk, bv, B, S)
    o_b = _ret_main(lg, lgl, bq, bk, bv, bg, rb, B, S)
    return o_a, o_b


def _outproj_kernel(o1_ref, o2_ref, w_ref, x_ref, g_ref, wr_ref, x1_ref, hn_ref, aff_ref):
    half = o1_ref.shape[1]
    x1 = x_ref[...] + (_dot(o1_ref[...], w_ref[:half, :]) + _dot(o2_ref[...], w_ref[half:, :]))
    x1_ref[...] = x1
    hn = _rms(x1, g_ref[...])
    hn_ref[...] = hn
    logits = _dot_nt(wr_ref[...], hn.astype(BF16))
    e = jnp.exp(logits - jnp.max(logits, axis=0, keepdims=True))
    aff_ref[...] = e / jnp.sum(e, axis=0, keepdims=True)


def _outproj_router(o1, o2, w_out, x2, g_ffn, w_router, B, S, tm):
    T, D = x2.shape
    E = w_router.shape[1]
    per = S // tm
    row = lambda width: pl.BlockSpec((tm, width), lambda i: (i, 0))
    full = lambda a: pl.BlockSpec(a.shape, lambda i: (0, 0))
    w = w_out.astype(BF16)
    g = g_ffn.reshape(1, D)
    wr = w_router.T.astype(BF16)
    return pl.pallas_call(
        _outproj_kernel,
        out_shape=(jax.ShapeDtypeStruct((T, D), F32), jax.ShapeDtypeStruct((T, D), F32),
                   jax.ShapeDtypeStruct((B, E, S), F32)),
        grid=(T // tm,),
        in_specs=[row(o1.shape[1]), row(o2.shape[1]), full(w), row(D), full(g), full(wr)],
        out_specs=(row(D), row(D), pl.BlockSpec((None, E, tm), lambda i: (i // per, 0, i % per))),
        compiler_params=_params(1), name="outproj_router")(o1, o2, w, x2, g, wr)


def _split3(x):
    x1 = x.astype(BF16)
    r = x - x1.astype(F32)
    x2 = r.astype(BF16)
    x3 = (r - x2.astype(F32)).astype(BF16)
    return x1, x2, x3


def _topk_kernel(aff_ref, idx_ref, gate_ref, thr_ref, *, cap):
    E, R, _ = aff_ref.shape
    bits = lax.bitcast_convert_type(aff_ref[...], I32)

    def count(mask):
        return jnp.sum(jnp.sum(mask.astype(F32), axis=2, keepdims=True), axis=1, keepdims=True)

    def bit_body(i, prefix):
        cand = prefix | jnp.left_shift(jnp.int32(1), 30 - i)
        return jnp.where(count(bits >= cand) >= cap, cand, prefix)

    thr = lax.fori_loop(0, 31, bit_body, jnp.zeros((E, 1, 1), I32))
    thr_ref[...] = jnp.broadcast_to(thr, thr_ref.shape)

    li = lax.broadcasted_iota(I32, (LANES, LANES), 0)
    lj = lax.broadcasted_iota(I32, (LANES, LANES), 1)
    tri = (li <= lj).astype(BF16)
    ri = lax.broadcasted_iota(I32, (R, R), 0)
    rj = lax.broadcasted_iota(I32, (R, R), 1)
    below = (rj < ri).astype(BF16)
    slot = lax.broadcasted_iota(I32, (1, cap), 1).astype(F32)
    tok = (lax.broadcasted_iota(I32, (R, LANES), 0) * LANES
           + lax.broadcasted_iota(I32, (R, LANES), 1)).astype(F32)

    def prefix_counts(m):
        within = _dot(m.astype(BF16), tri)
        total = jnp.broadcast_to(within[:, LANES - 1:LANES], (R, LANES))
        before = _dot(below, total.astype(BF16))
        return within, total, before

    def expert(e, carry):
        a = aff_ref[e]
        b = lax.bitcast_convert_type(a, I32)
        t = thr_ref[e]
        gt, eq = b > t, b == t
        n_gt = jnp.sum(jnp.sum(gt.astype(F32), axis=1, keepdims=True), axis=0, keepdims=True)
        eqf = eq.astype(F32)
        within, _, before = prefix_counts(eqf)
        sel = gt | (eq & (before + within - eqf < cap - n_gt))
        self_ = sel.astype(F32)
        within, total, before = prefix_counts(self_)
        rank = before + within
        first, count_r = before[:, 0:1], total[:, 0:1]
        owner = ((first <= slot) & (slot < first + count_r)).astype(BF16)

        def row_of_slot(x):
            return sum(_dot_tn(p, owner) for p in _split3(x))

        hit = (row_of_slot(self_) > 0.5) & (row_of_slot(rank) == slot + 1.0)
        idx_ref[e] = jnp.sum(jnp.where(hit, row_of_slot(tok), 0.0), axis=0, keepdims=True).astype(I32)
        gate_ref[e] = jnp.sum(jnp.where(hit, row_of_slot(a), 0.0), axis=0, keepdims=True)
        return carry

    lax.fori_loop(0, E, expert, 0)


def _topk(aff, cap):
    B, E, S = aff.shape
    R = S // LANES
    aff4 = aff.reshape(B, E, R, LANES)
    out = jax.ShapeDtypeStruct((B, E, 1, cap), I32), jax.ShapeDtypeStruct((B, E, 1, cap), F32)
    spec = pl.BlockSpec((None, E, 1, cap), lambda b: (b, 0, 0, 0))
    return pl.pallas_call(
        functools.partial(_topk_kernel, cap=cap), out_shape=out, grid=(B,),
        in_specs=[pl.BlockSpec((None, E, R, LANES), lambda b: (b, 0, 0, 0))],
        out_specs=(spec, spec), scratch_shapes=[pltpu.VMEM((E, 1, LANES), I32)],
        compiler_params=_params(1), name="topk")(aff4)


ROW_UNROLL = 8
SEM_X, SEM_ROW, SEM_OUT = 0, 1, 2


def _moe_kernel(idx_ref, idx_next_ref, gate_ref, wg_ref, wu_ref, wd_ref, hn_hbm, x_alias, out_hbm,
                xin_buf, row_buf, sems, *, seq, cap, f_chunk):
    del x_alias
    n_batch = pl.num_programs(1)
    step = pl.program_id(0) * n_batch + pl.program_id(1)
    n_steps = pl.num_programs(0) * n_batch
    slot = step % 2
    other = 1 - slot

    def for_rows(fn):
        @pl.loop(0, cap // ROW_UNROLL)
        def _(g):
            for u in range(ROW_UNROLL):
                fn(g * ROW_UNROLL + u)

    def start_gathers(ids, batch, s):
        def one(j):
            t = batch * seq + ids[0, j]
            pltpu.make_async_copy(hn_hbm.at[pl.ds(t, 1)], xin_buf.at[s, pl.ds(j, 1)], sems.at[SEM_X, s]).start()
            pltpu.make_async_copy(out_hbm.at[pl.ds(t, 1)], row_buf.at[s, pl.ds(j, 1)], sems.at[SEM_ROW, s]).start()
        for_rows(one)

    def wait_all(kind, buf, s):
        pltpu.make_async_copy(hn_hbm.at[pl.ds(0, cap)], buf.at[s], sems.at[kind, s]).wait()

    @pl.when(step == 0)
    def _():
        start_gathers(idx_ref, pl.program_id(1), slot)

    @pl.when(step > 0)
    def _():
        wait_all(SEM_OUT, row_buf, other)

    @pl.when(step + 1 < n_steps)
    def _():
        start_gathers(idx_next_ref, (step + 1) % n_batch, other)

    wait_all(SEM_X, xin_buf, slot)
    wait_all(SEM_ROW, row_buf, slot)

    xin = xin_buf[slot].astype(BF16)
    acc = jnp.zeros((cap, xin.shape[1]), F32)
    for f in range(wg_ref.shape[1] // f_chunk):
        fs = slice(f * f_chunk, (f + 1) * f_chunk)
        g = _dot(xin, wg_ref[:, fs])
        hid = (g * jax.nn.sigmoid(g) * _dot(xin, wu_ref[:, fs])).astype(BF16)
        acc = acc + _dot(hid, wd_ref[fs, :])
    diag = (lax.broadcasted_iota(I32, (cap, cap), 0) == lax.broadcasted_iota(I32, (cap, cap), 1))
    gate_col = jnp.sum(jnp.where(diag, gate_ref[...], 0.0), axis=1, keepdims=True)
    row_buf[slot] = row_buf[slot] + acc * gate_col

    def write_back(j):
        t = pl.program_id(1) * seq + idx_ref[0, j]
        pltpu.make_async_copy(row_buf.at[slot, pl.ds(j, 1)], out_hbm.at[pl.ds(t, 1)], sems.at[SEM_OUT, slot]).start()
    for_rows(write_back)

    @pl.when(step == n_steps - 1)
    def _():
        wait_all(SEM_OUT, row_buf, slot)


def _moe_ffn(hn, x1, idx, gate, w_gate, w_up, w_down, B, S, f_chunk=512):
    T, D = x1.shape
    E, _, F = w_gate.shape
    cap = idx.shape[-1]
    f_chunk = min(f_chunk, F)
    assert B >= 2 and cap % ROW_UNROLL == 0
    wspec = lambda a: pl.BlockSpec((None,) + a.shape[1:], lambda e, b: (e, 0, 0))
    any_spec = pl.BlockSpec(memory_space=pl.ANY)
    ids = lambda index_map: pl.BlockSpec((None, None, 1, cap), index_map, memory_space=pltpu.SMEM)
    nxt = lambda e, b: ((e * B + b + 1) % B, jnp.minimum((e * B + b + 1) // B, E - 1), 0, 0)
    return pl.pallas_call(
        functools.partial(_moe_kernel, seq=S, cap=cap, f_chunk=f_chunk),
        out_shape=jax.ShapeDtypeStruct((T, D), F32), grid=(E, B),
        in_specs=[ids(lambda e, b: (b, e, 0, 0)), ids(nxt),
                  pl.BlockSpec((None, None, 1, cap), lambda e, b: (b, e, 0, 0)),
                  wspec(w_gate), wspec(w_up), wspec(w_down), any_spec, any_spec],
        out_specs=any_spec,
        scratch_shapes=[pltpu.VMEM((2, cap, D), F32), pltpu.VMEM((2, cap, D), F32),
                        pltpu.SemaphoreType.DMA((3, 2))],
        input_output_aliases={7: 0},
        compiler_params=_params(2), name="moe_ffn")(idx, idx, gate, w_gate, w_up, w_down, hn, x1)


def moe_layer(o1, o2, w_out, x2, g_ffn, w_router, w_gate, w_up, w_down, B, S, tm=512):
    x1, hn, aff = _outproj_router(o1, o2, w_out, x2, g_ffn, w_router, B, S, tm)
    cap = EC_CAPACITY_FACTOR * S // w_router.shape[1]
    idx, gate = _topk(aff, cap)
    return _moe_ffn(hn, x1, idx, gate, w_gate.astype(BF16), w_up.astype(BF16), w_down.astype(BF16), B, S)


def _final_norm_kernel(x_ref, g_ref, o_ref):
    o_ref[...] = _rms(x_ref[...], g_ref[...])


def _final_norm(x2, g, tm):
    T, D = x2.shape
    row = pl.BlockSpec((tm, D), lambda i: (i, 0))
    return pl.pallas_call(
        _final_norm_kernel, out_shape=jax.ShapeDtypeStruct((T, D), F32), grid=(T // tm,),
        in_specs=[row, pl.BlockSpec((1, D), lambda i: (0, 0))], out_specs=row,
        compiler_params=_params(1), name="final_norm")(x2, g.reshape(1, D))


ODD_SPLITS = (C_Q_LORA, C_KV_LORA, C_ROPE, D_W, D_W, D_W, D_W, 4 * D_HEADS)
N_GATES = 4 * D_HEADS
I_FWD, F_FWD, I_BWD, F_BWD = 0, D_HEADS, 2 * D_HEADS, 3 * D_HEADS


def _odd_weights(w_in):
    cq, ckv, kr, dq, dk, dv, do, dg = jnp.split(w_in, [int(c) for c in np.cumsum(ODD_SPLITS)[:-1]], axis=1)
    D = w_in.shape[0]
    zeros = lambda n: jnp.zeros((D, n), w_in.dtype)
    kr_slot = jnp.concatenate([zeros(C_NOPE), kr, zeros(C_SLOT - C_NOPE - C_ROPE)], axis=1)
    dg_slot = jnp.concatenate([dg, zeros(LANES - N_GATES)], axis=1)
    w = jnp.concatenate([cq, ckv, kr_slot, dq, dk, dv, do, dg_slot], axis=1)
    return w.astype(BF16), dg.T.astype(BF16)


def _inproj_odd_kernel(x_ref, g_ref, w_ref, wgt_ref,
                       cq_ref, ckv_ref, kr_ref, dqk_ref, dv_ref, do_ref, gc_ref, gt_ref):
    h = _rms(x_ref[...], g_ref[...]).astype(BF16)
    off = 0
    for ref, width in ((cq_ref, C_Q_LORA), (ckv_ref, C_KV_LORA), (kr_ref, C_SLOT), (dqk_ref, 2 * D_W),
                       (dv_ref, D_W), (do_ref, D_W), (gc_ref, LANES)):
        ref[...] = _dot(h, w_ref[:, off:off + width]).astype(ref.dtype)
        off += width
    gt_ref[...] = _dot_nt(wgt_ref[...], h)


def _inproj_odd(x2, g, w, wgt, tm):
    T, D = x2.shape
    row = lambda width: pl.BlockSpec((tm, width), lambda i: (i, 0))
    full = lambda a: pl.BlockSpec(a.shape, lambda i: (0, 0))
    widths = (C_Q_LORA, C_KV_LORA, C_SLOT, 2 * D_W, D_W, D_W, LANES)
    dtypes = (F32, F32, F32, F32, BF16, F32, F32)
    out_shape = tuple(jax.ShapeDtypeStruct((T, wd), dt) for wd, dt in zip(widths, dtypes))
    out_shape += (jax.ShapeDtypeStruct((N_GATES, T), F32),)
    return pl.pallas_call(
        _inproj_odd_kernel, out_shape=out_shape, grid=(T // tm,),
        in_specs=[row(D), full(g), full(w), full(wgt)],
        out_specs=tuple(row(wd) for wd in widths) + (pl.BlockSpec((N_GATES, tm), lambda i: (0, i)),),
        compiler_params=_params(1), name="inproj_odd")(x2, g, w, wgt)


def _mla_prep_kernel(cq_ref, ckv_ref, kr_ref, nq_ref, nkv_ref, wq_ref, wk_ref, wv_ref, cos_ref, sin_ref,
                     q_ref, k_ref, vt_ref):
    lane = lax.broadcasted_iota(I32, (1, LANES), 1)
    first = lane < C_NOPE + C_ROPE // 2
    cos, sin = cos_ref[...], sin_ref[...]
    rope = lambda z: _rope_slab(z, cos, sin, C_ROPE // 2, first)
    q = _dot(_rms(cq_ref[...], nq_ref[...]).astype(BF16), wq_ref[...])
    hkv = _rms(ckv_ref[...], nkv_ref[...]).astype(BF16)
    kn = _dot(hkv, wk_ref[...])
    kr = rope(kr_ref[...])
    for hh in range(C_HEADS):
        slab = slice(hh * C_SLOT, (hh + 1) * C_SLOT)
        q_ref[:, slab] = rope(q[:, slab]).astype(BF16)
        k_ref[:, slab] = (kn[:, slab] + kr).astype(BF16)
    vt = _dot_nt(wv_ref[...], hkv)
    row = lax.broadcasted_iota(I32, vt.shape, 0)
    vt_ref[...] = jnp.where(row % C_SLOT < C_V, vt, 1.0).astype(BF16)


def _mla_prep(cq, ckv, kr, nq, nkv, w_uq, w_ukv, cos, sin, tm):
    T = cq.shape[0]
    pad_q = C_SLOT - C_NOPE - C_ROPE
    wq = jnp.pad(w_uq.reshape(C_Q_LORA, C_HEADS, C_NOPE + C_ROPE), ((0, 0), (0, 0), (0, pad_q)))
    wq = wq.reshape(C_Q_LORA, C_HEADS * C_SLOT).astype(BF16)
    wkv = w_ukv.reshape(C_KV_LORA, C_HEADS, C_NOPE + C_V)
    wk = jnp.pad(wkv[:, :, :C_NOPE], ((0, 0), (0, 0), (0, C_SLOT - C_NOPE)))
    wk = wk.reshape(C_KV_LORA, C_HEADS * C_SLOT).astype(BF16)
    wv = jnp.pad(wkv[:, :, C_NOPE:], ((0, 0), (0, 0), (0, C_SLOT - C_V)))
    wv = wv.reshape(C_KV_LORA, C_HEADS * C_SLOT).T.astype(BF16)
    nq, nkv = nq.reshape(1, -1), nkv.reshape(1, -1)
    row = lambda width: pl.BlockSpec((tm, width), lambda i: (i, 0))
    full = lambda a: pl.BlockSpec(a.shape, lambda i: (0, 0))
    slots = jax.ShapeDtypeStruct((T, C_HEADS * C_SLOT), BF16)
    vt_shape = jax.ShapeDtypeStruct((T // tm, C_HEADS * C_SLOT, tm), BF16)
    return pl.pallas_call(
        _mla_prep_kernel, out_shape=(slots, slots, vt_shape), grid=(T // tm,),
        in_specs=[row(C_Q_LORA), row(C_KV_LORA), row(C_SLOT), full(nq), full(nkv), full(wq), full(wk), full(wv),
                  row(LANES), row(LANES)],
        out_specs=(row(C_HEADS * C_SLOT), row(C_HEADS * C_SLOT),
                   pl.BlockSpec((None, C_HEADS * C_SLOT, tm), lambda i: (i, 0, 0))),
        compiler_params=_params(1), name="mla_prep")(cq, ckv, kr, nq, nkv, wq, wk, wv, cos, sin)


C_PAIR = 2


def _mla_attn_kernel(q_ref, k_ref, vt_ref, o_ref):
    tq = q_ref.shape[0]
    n_chunks, _, key_chunk = vt_ref.shape
    mult = (C_NOPE + C_ROPE) ** -0.5 * np.log2(np.e)
    for hh in range(C_PAIR):
        slab = slice(hh * C_SLOT, (hh + 1) * C_SLOT)
        q = q_ref[:, slab]

        def scores(c):
            return _dot_nt(k_ref[c * key_chunk:(c + 1) * key_chunk, slab], q)

        m = jnp.full((1, tq), NEG, F32)
        acc = jnp.zeros((C_SLOT, tq), F32)
        st = scores(0)
        for c in range(n_chunks):
            st_next = scores(c + 1) if c + 1 < n_chunks else None
            m_new = jnp.maximum(m, jnp.max(st, axis=0, keepdims=True))
            pt = jnp.exp2(((st - m_new) * mult).astype(BF16))
            acc = jnp.exp2((m - m_new) * mult) * acc + _dot(vt_ref[c, slab, :], pt)
            m, st = m_new, st_next
        acc = acc.T
        o_ref[:, hh * C_V:(hh + 1) * C_V] = (acc[:, :C_V] / acc[:, C_V:C_V + 1]).astype(BF16)


def _mla_attn(q, k, vt, B, S, tq):
    T = B * S
    nq = S // tq
    key_chunk = vt.shape[-1]
    vt = vt.reshape(B, S // key_chunk, C_HEADS * C_SLOT, key_chunk)
    return pl.pallas_call(
        _mla_attn_kernel,
        out_shape=jax.ShapeDtypeStruct((T, C_HEADS * C_V), BF16), grid=(B, C_HEADS // C_PAIR, nq),
        in_specs=[pl.BlockSpec((tq, C_PAIR * C_SLOT), lambda b, p, i: (b * nq + i, p)),
                  pl.BlockSpec((S, C_PAIR * C_SLOT), lambda b, p, i: (b, p)),
                  pl.BlockSpec((None, S // key_chunk, C_PAIR * C_SLOT, key_chunk), lambda b, p, i: (b, 0, p, 0))],
        out_specs=pl.BlockSpec((tq, C_PAIR * C_V), lambda b, p, i: (b * nq + i, p)),
        compiler_params=_params(3), name="mla_attn")(q, k, vt)


HALO = 8


def _conv_kernel(xp_ref, x_ref, xn_ref, w_ref, o_ref, *, n_tiles):
    j = pl.program_id(1)
    tc = x_ref.shape[0]
    prev = jnp.where(j > 0, xp_ref[...], 0.0)
    nxt = jnp.where(j < n_tiles - 1, xn_ref[...], 0.0)
    ext = jnp.concatenate([prev, x_ref[...], nxt], axis=0)
    rows = tc + 2 * HALO
    y = jnp.zeros(x_ref.shape, F32)
    for w in range(D_CONV):
        first = HALO - D_CONV // 2 + w
        y = y + pltpu.roll(ext, (rows - first) % rows, 0)[:tc] * w_ref[w:w + 1, :]
    y = y * jax.nn.sigmoid(y)
    o_ref[:, :D_W] = y[:, :D_W]
    o_ref[:, D_W:] = y[:, D_W:] * (D_HEAD_DIM ** -0.5)


def _conv_prep(dqk, conv_w, B, S, tc):
    T, C = dqk.shape
    n_tiles = S // tc
    per, last = tc // HALO, T // HALO - 1
    cur = lambda b, j: (b * n_tiles + j, 0)
    return pl.pallas_call(
        functools.partial(_conv_kernel, n_tiles=n_tiles),
        out_shape=jax.ShapeDtypeStruct((T, C), F32), grid=(B, n_tiles),
        in_specs=[pl.BlockSpec((HALO, C), lambda b, j: (jnp.maximum((b * n_tiles + j) * per - 1, 0), 0)),
                  pl.BlockSpec((tc, C), cur),
                  pl.BlockSpec((HALO, C), lambda b, j: (jnp.minimum((b * n_tiles + j + 1) * per, last), 0)),
                  pl.BlockSpec(conv_w.shape, lambda b, j: (0, 0))],
        out_specs=pl.BlockSpec((tc, C), cur),
        compiler_params=_params(2), name="conv_prep")(dqk, dqk, dqk, conv_w)


def _log_sigmoid(x):
    return -(jnp.maximum(-x, 0.0) + jnp.log1p(jnp.exp(-jnp.abs(x))))


def _tri(lower):
    a = lax.broadcasted_iota(I32, (BLOCK, BLOCK), 0)
    b = lax.broadcasted_iota(I32, (BLOCK, BLOCK), 1)
    return ((b <= a) if lower else (b >= a)).astype(BF16)


def _mlstm_update(s_ref, m_ref, h, g, a, k, v_ones):
    a_max = jnp.max(a, axis=0, keepdims=True)
    w = jnp.exp(a - a_max)
    upd = _dot_tn((k * w).astype(BF16), v_ones)
    m = m_ref[h:h + 1, 0:1]
    m_new = jnp.maximum(g + m, a_max)
    cols = slice(h * 2 * D_HEAD_DIM, (h + 1) * 2 * D_HEAD_DIM)
    s_ref[:, cols] = jnp.exp(g + m - m_new) * s_ref[:, cols] + jnp.exp(a_max - m_new) * upd
    m_ref[h:h + 1, :] = jnp.broadcast_to(m_new, (1, LANES))


def _mlstm_bwd_state_kernel(bias_c_ref, qk_ref, v_ref, gc_ref, s_out_ref, m_out_ref, s_ref, m_ref):
    @pl.when(pl.program_id(1) == 0)
    def _():
        s_ref[...] = jnp.zeros_like(s_ref)
        m_ref[...] = jnp.zeros_like(m_ref)

    s_out_ref[...] = s_ref[...]
    m_out_ref[...] = m_ref[...]
    gc = gc_ref[...] + bias_c_ref[...]
    suffix = sum(_dot(_tri(False), p) for p in _split3(_log_sigmoid(gc)))
    ones = jnp.ones((BLOCK, D_HEAD_DIM), BF16)
    for h in range(D_HEADS):
        hs = slice(h * D_HEAD_DIM, (h + 1) * D_HEAD_DIM)
        sb = suffix[:, F_BWD + h:F_BWD + h + 1]
        g = sb[0:1, :]
        a = g - sb + gc[:, I_BWD + h:I_BWD + h + 1]
        k = qk_ref[:, D_W + h * D_HEAD_DIM:D_W + (h + 1) * D_HEAD_DIM]
        _mlstm_update(s_ref, m_ref, h, g, a, k, jnp.concatenate([v_ref[:, hs], ones], axis=1))


def _mlstm_bwd_state(bias_c, qk, dv, gc, B, S):
    N = S // BLOCK
    rev = lambda b, n: (b * N + N - 1 - n, 0)
    out_shape = (jax.ShapeDtypeStruct((B, N, D_HEAD_DIM, 2 * D_W), F32), jax.ShapeDtypeStruct((B, N, 8, LANES), F32))
    return pl.pallas_call(
        _mlstm_bwd_state_kernel, out_shape=out_shape, grid=(B, N),
        in_specs=[pl.BlockSpec(bias_c.shape, lambda b, n: (0, 0)), pl.BlockSpec((BLOCK, 2 * D_W), rev),
                  pl.BlockSpec((BLOCK, D_W), rev), pl.BlockSpec((BLOCK, LANES), rev)],
        out_specs=(pl.BlockSpec((None, None, D_HEAD_DIM, 2 * D_W), lambda b, n: (b, N - 1 - n, 0, 0)),
                   pl.BlockSpec((None, None, 8, LANES), lambda b, n: (b, N - 1 - n, 0, 0))),
        scratch_shapes=[pltpu.VMEM((D_HEAD_DIM, 2 * D_W), F32), pltpu.VMEM((8, LANES), F32)],
        compiler_params=_params(2), name="mlstm_bwd_state")(bias_c, qk, dv, gc)


def _mlstm_main_kernel(bias_c_ref, bias_r_ref, qk_ref, v_ref, og_ref, gc_ref, gt_ref, sb_ref, mb_ref,
                       o_ref, s_ref, m_ref):
    @pl.when(pl.program_id(1) == 0)
    def _():
        s_ref[...] = jnp.zeros_like(s_ref)
        m_ref[...] = jnp.zeros_like(m_ref)

    gc = gc_ref[...] + bias_c_ref[...]
    gr = gt_ref[...] + bias_r_ref[...]
    lower, upper = _tri(True), _tri(False)
    lfc, lfr = _split3(_log_sigmoid(gc)), _split3(_log_sigmoid(gr))
    pre_c = sum(_dot(lower, p) for p in lfc)
    suf_c = sum(_dot(upper, p) for p in lfc)
    pre_r = sum(_dot(p, upper) for p in lfr)
    suf_r = sum(_dot(p, lower) for p in lfr)
    ti = lax.broadcasted_iota(I32, (BLOCK, BLOCK), 0)
    tj = lax.broadcasted_iota(I32, (BLOCK, BLOCK), 1)
    ones = jnp.ones((BLOCK, D_HEAD_DIM), BF16)
    og = og_ref[...]
    for h in range(D_HEADS):
        hs = slice(h * D_HEAD_DIM, (h + 1) * D_HEAD_DIM)
        cols = slice(h * 2 * D_HEAD_DIM, (h + 1) * 2 * D_HEAD_DIM)
        k = qk_ref[:, D_W + h * D_HEAD_DIM:D_W + (h + 1) * D_HEAD_DIM]
        qb, v = qk_ref[:, hs].astype(BF16), v_ref[:, hs]
        qkt = _dot_nt(qb, k.astype(BF16))

        def direction(b_col, b_row, i_row, mask, state, m_prev):
            logd = jnp.where(mask, b_col - b_row + i_row, NEG)
            log_inter = b_col + m_prev
            m_t = jnp.maximum(jnp.max(logd, axis=1, keepdims=True), log_inter)
            s = qkt * jnp.exp(logd - m_t)
            inter_w = jnp.exp(log_inter - m_t)
            qs = _dot(qb, state.astype(BF16))
            num = _dot(s.astype(BF16), v) + inter_w * qs[:, :D_HEAD_DIM]
            den = jnp.sum(s, axis=1, keepdims=True) + inter_w * qs[:, D_HEAD_DIM:D_HEAD_DIM + 1]
            return num / jnp.maximum(jnp.abs(den), jnp.exp(-m_t))

        f, bk = F_FWD + h, F_BWD + h
        h_f = direction(pre_c[:, f:f + 1], pre_r[f:f + 1, :], gr[I_FWD + h:I_FWD + h + 1, :], tj <= ti,
                        s_ref[:, cols], m_ref[h:h + 1, 0:1])
        h_b = direction(suf_c[:, bk:bk + 1], suf_r[bk:bk + 1, :], gr[I_BWD + h:I_BWD + h + 1, :], tj > ti,
                        sb_ref[:, cols], mb_ref[h:h + 1, 0:1])
        o_ref[:, hs] = (jax.nn.sigmoid(og[:, hs]) * (h_f + h_b)).astype(BF16)
        b_col = pre_c[:, f:f + 1]
        g = b_col[BLOCK - 1:BLOCK, :]
        a = g - b_col + gc[:, I_FWD + h:I_FWD + h + 1]
        _mlstm_update(s_ref, m_ref, h, g, a, k, jnp.concatenate([v, ones], axis=1))


def _mlstm_main(bias_c, bias_r, qk, dv, og, gc, gt, sb, mb, B, S):
    N = S // BLOCK
    T = B * S
    cur = lambda b, n: (b * N + n, 0)
    return pl.pallas_call(
        _mlstm_main_kernel, out_shape=jax.ShapeDtypeStruct((T, D_W), BF16), grid=(B, N),
        in_specs=[pl.BlockSpec(bias_c.shape, lambda b, n: (0, 0)), pl.BlockSpec(bias_r.shape, lambda b, n: (0, 0)),
                  pl.BlockSpec((BLOCK, 2 * D_W), cur), pl.BlockSpec((BLOCK, D_W), cur), pl.BlockSpec((BLOCK, D_W), cur),
                  pl.BlockSpec((BLOCK, LANES), cur), pl.BlockSpec((N_GATES, BLOCK), lambda b, n: (0, b * N + n)),
                  pl.BlockSpec((None, None, D_HEAD_DIM, 2 * D_W), lambda b, n: (b, n, 0, 0)),
                  pl.BlockSpec((None, None, 8, LANES), lambda b, n: (b, n, 0, 0))],
        out_specs=pl.BlockSpec((BLOCK, D_W), cur),
        scratch_shapes=[pltpu.VMEM((D_HEAD_DIM, 2 * D_W), F32), pltpu.VMEM((8, LANES), F32)],
        compiler_params=_params(2), name="mlstm_main")(bias_c, bias_r, qk, dv, og, gc, gt, sb, mb)


def odd_mixer_parts(x2, positions, g_mix, w_in, norm_q, norm_kv, w_uq, w_ukv, conv_w, gate_bias, B, S,
                    tm=512, tq=512, key_chunk=1024):
    w, wgt = _odd_weights(w_in)
    cq, ckv, kr, dqk, dv, og, gc, gt = _inproj_odd(x2, g_mix.reshape(1, -1), w, wgt, tm)
    cos, sin = _rope_tables(positions, MLA_THETA, C_SLOT, C_NOPE, C_ROPE)
    q, k, vt = _mla_prep(cq, ckv, kr, norm_q, norm_kv, w_uq, w_ukv, cos, sin, min(key_chunk, S))
    o_c = _mla_attn(q, k, vt, B, S, min(tq, S))
    qk = _conv_prep(dqk, conv_w.astype(F32), B, S, min(tm, S))
    bias = gate_bias.astype(F32).reshape(1, N_GATES)
    bias_c = jnp.pad(bias, ((0, 0), (0, LANES - N_GATES)))
    bias_r = bias.reshape(N_GATES, 1)
    sb, mb = _mlstm_bwd_state(bias_c, qk, dv, gc, B, S)
    o_d = _mlstm_main(bias_c, bias_r, qk, dv, og, gc, gt, sb, mb, B, S)
    return o_c, o_d


def kernel(x, positions, norm_mix, norm_ffn, norm_final, ev_w_in, ev_w_out, attn_sink, ret_decay_logit,
           od_w_in, od_w_out, mla_norm_q, mla_norm_kv, mla_w_uq, mla_w_ukv, mlstm_conv, mlstm_gate_bias,
           moe_router, moe_w_gate, moe_w_up, moe_w_down):
    B, S, D = x.shape
    tm = min(512, S)
    x2 = x.reshape(B * S, D)
    for layer in range(norm_mix.shape[0]):
        j = layer // 2
        if layer % 2 == 0:
            o1, o2 = even_mixer_parts(x2, positions, norm_mix[layer], ev_w_in[j], attn_sink[j],
                                      ret_decay_logit[j], B, S, tm)
            w_out = ev_w_out[j]
        else:
            o1, o2 = odd_mixer_parts(x2, positions, norm_mix[layer], od_w_in[j], mla_norm_q[j], mla_norm_kv[j],
                                     mla_w_uq[j], mla_w_ukv[j], mlstm_conv[j], mlstm_gate_bias[j], B, S, tm)
            w_out = od_w_out[j]
        x2 = moe_layer(o1, o2, w_out, x2, norm_ffn[layer], moe_router[layer], moe_w_gate[layer],
                       moe_w_up[layer], moe_w_down[layer], B, S, tm)
    return _final_norm(x2, norm_final, tm).reshape(B, S, D)
```

```python
import functools

import jax
import jax.numpy as jnp
import numpy as np
from jax import lax
from jax.experimental import pallas as pl
from jax.experimental.pallas import tpu as pltpu

F32 = jnp.float32
BF16 = jnp.bfloat16
I32 = jnp.int32

LANES = 128
BLOCK = 128
RMS_EPS = 1e-6
GN_EPS = 1e-5
NEG = -1e30
VMEM_LIMIT = 56 * 1024 * 1024

A_HEADS, A_KV_HEADS, A_HEAD_DIM = 8, 2, 64
ROPE_THETA = 500000.0
ROPE_DIM = A_HEAD_DIM // 4
B_HEADS, B_HEAD_DIM = 8, 64
RET_THETA = 10000.0
C_HEADS, C_NOPE, C_ROPE, C_V = 8, 64, 32, 64
C_Q_LORA, C_KV_LORA = 512, 256
MLA_THETA = 10000.0
D_HEADS, D_HEAD_DIM, D_CONV = 4, 128, 5
N_EXPERTS = 16
EC_CAPACITY_FACTOR = 2

A_Q_W = A_HEADS * A_HEAD_DIM
A_KV_W = A_KV_HEADS * A_HEAD_DIM
B_W = B_HEADS * B_HEAD_DIM
D_W = D_HEADS * D_HEAD_DIM
C_SLOT = 128


def _params(n_axes, vmem=VMEM_LIMIT):
    return pltpu.CompilerParams(dimension_semantics=("arbitrary",) * n_axes,
                                vmem_limit_bytes=vmem)


def _rms(x, g):
    return x * lax.rsqrt(jnp.mean(x * x, axis=-1, keepdims=True) + RMS_EPS) * g


def _dot(a, b):
    return jnp.dot(a, b, preferred_element_type=F32)


def _dot_nt(a, b):
    return lax.dot_general(a, b, (((1,), (1,)), ((), ())), preferred_element_type=F32)


def _dot_tn(a, b):
    return lax.dot_general(a, b, (((0,), (0,)), ((), ())), preferred_element_type=F32)


def _rope_slab(z, cos, sin, half, first_half):
    partner = jnp.where(first_half, pltpu.roll(z, LANES - half, 1), pltpu.roll(z, half, 1))
    return z * cos + partner * sin


def _rope_tables(positions, theta, head_dim, rot_start, rot_dim):
    half = rot_dim // 2
    assert head_dim % half == 0 and rot_start % half == 0
    inv_freq = theta ** (-jnp.arange(half, dtype=F32) * 2.0 / rot_dim)
    d = np.arange(LANES) % head_dim - rot_start
    rot = (d >= 0) & (d < rot_dim)
    sign = np.where(d < half, -1.0, 1.0).astype(np.float32)
    ang = positions.astype(F32).reshape(-1, 1) * inv_freq
    ang = jnp.tile(ang, (1, LANES // half))
    cos = jnp.where(rot[None, :], jnp.cos(ang), 1.0)
    sin = jnp.where(rot[None, :], jnp.sin(ang) * sign[None, :], 0.0)
    return cos, sin


def _inproj_even_kernel(x_ref, g_ref, w_ref, ca_ref, sa_ref, cb_ref, sb_ref,
                        aq_ref, akv_ref, bq_ref, bk_ref, bv_ref, bg_ref):
    h = _rms(x_ref[...], g_ref[...]).astype(BF16)
    lane = lax.broadcasted_iota(I32, (1, LANES), 1)
    first_a = (lane % A_HEAD_DIM) < (ROPE_DIM // 2)
    first_b = (lane % B_HEAD_DIM) < (B_HEAD_DIM // 2)
    ca, sa, cb, sb = ca_ref[...], sa_ref[...], cb_ref[...], sb_ref[...]

    def rope_a(z):
        return _rope_slab(z, ca, sa, ROPE_DIM // 2, first_a)

    def rope_b(z):
        return _rope_slab(z, cb, sb, B_HEAD_DIM // 2, first_b)

    off = 0
    z = _dot(h, w_ref[:, off:off + A_Q_W])
    for s in range(A_Q_W // LANES):
        aq_ref[:, s * LANES:(s + 1) * LANES] = rope_a(z[:, s * LANES:(s + 1) * LANES]).astype(BF16)
    off += A_Q_W
    z = _dot(h, w_ref[:, off:off + 2 * A_KV_W])
    akv_ref[:, :A_KV_W] = rope_a(z[:, :A_KV_W]).astype(BF16)
    akv_ref[:, A_KV_W:] = z[:, A_KV_W:].astype(BF16)
    off += 2 * A_KV_W
    z = _dot(h, w_ref[:, off:off + B_W])
    for s in range(B_W // LANES):
        bq_ref[:, s * LANES:(s + 1) * LANES] = rope_b(z[:, s * LANES:(s + 1) * LANES])
    off += B_W
    z = _dot(h, w_ref[:, off:off + B_W])
    for s in range(B_W // LANES):
        bk_ref[:, s * LANES:(s + 1) * LANES] = rope_b(z[:, s * LANES:(s + 1) * LANES]) * (B_HEAD_DIM ** -0.5)
    off += B_W
    bv_ref[...] = _dot(h, w_ref[:, off:off + B_W]).astype(BF16)
    off += B_W
    bg_ref[...] = _dot(h, w_ref[:, off:off + B_W])


def _stream_spec(xs, D, tm):
    col = xs.shape[1] // D - 1
    return pl.BlockSpec((tm, D), lambda i: (i, col))


def _inproj_even(x2, g, w, tabs, tm):
    T, D = tabs[0].shape[0], g.shape[-1]
    ncol = w.shape[1]
    row = lambda width: pl.BlockSpec((tm, width), lambda i: (i, 0))
    full = lambda a: pl.BlockSpec(a.shape, lambda i: (0, 0))
    out_shape = (jax.ShapeDtypeStruct((T, A_Q_W), BF16), jax.ShapeDtypeStruct((T, 2 * A_KV_W), BF16),
                 jax.ShapeDtypeStruct((T, B_W), F32), jax.ShapeDtypeStruct((T, B_W), F32),
                 jax.ShapeDtypeStruct((T, B_W), BF16), jax.ShapeDtypeStruct((T, B_W), F32))
    return pl.pallas_call(
        _inproj_even_kernel, out_shape=out_shape, grid=(T // tm,),
        in_specs=[_stream_spec(x2, D, tm), full(g), pl.BlockSpec((D, ncol), lambda i: (0, 0))] + [row(LANES)] * 4,
        out_specs=(row(A_Q_W), row(2 * A_KV_W), row(B_W), row(B_W), row(B_W), row(B_W)),
        compiler_params=_params(1), name="inproj_even")(x2, g, w, *tabs)


def _win_attn_kernel(sink_ref, q_ref, kvp_ref, kvc_ref, kvn_ref, o_ref, *, n_blocks):
    n = pl.program_id(1)
    group = A_HEADS // A_KV_HEADS
    rows = group * BLOCK
    qi = lax.broadcasted_iota(I32, (rows, BLOCK), 0) % BLOCK
    kj = lax.broadcasted_iota(I32, (rows, BLOCK), 1)
    ok_prev = (kj >= qi) & (n > 0)
    ok_next = (kj <= qi) & (n < n_blocks - 1)
    q = q_ref[...]
    scale = A_HEAD_DIM ** -0.5
    for g in range(A_KV_HEADS):
        ks = slice(g * A_HEAD_DIM, (g + 1) * A_HEAD_DIM)
        vs = slice(A_KV_W + g * A_HEAD_DIM, A_KV_W + (g + 1) * A_HEAD_DIM)
        qg = jnp.concatenate([q[:, (g * group + i) * A_HEAD_DIM:(g * group + i + 1) * A_HEAD_DIM]
                              for i in range(group)], axis=0)
        sink = jnp.concatenate([jnp.full((BLOCK, 1), sink_ref[g * group + i], F32)
                                for i in range(group)], axis=0)
        sp = jnp.where(ok_prev, _dot_nt(qg, kvp_ref[:, ks]) * scale, NEG)
        sc = _dot_nt(qg, kvc_ref[:, ks]) * scale
        sn = jnp.where(ok_next, _dot_nt(qg, kvn_ref[:, ks]) * scale, NEG)
        m = jnp.maximum(jnp.maximum(jnp.max(sp, -1, keepdims=True), jnp.max(sc, -1, keepdims=True)),
                        jnp.maximum(jnp.max(sn, -1, keepdims=True), sink))
        pp, pc, pn = jnp.exp(sp - m), jnp.exp(sc - m), jnp.exp(sn - m)
        den = (jnp.sum(pp, -1, keepdims=True) + jnp.sum(pc, -1, keepdims=True)
               + jnp.sum(pn, -1, keepdims=True) + jnp.exp(sink - m))
        o = (_dot(pp.astype(BF16), kvp_ref[:, vs]) + _dot(pc.astype(BF16), kvc_ref[:, vs])
             + _dot(pn.astype(BF16), kvn_ref[:, vs])) / den
        for i in range(group):
            hh = g * group + i
            o_ref[:, hh * A_HEAD_DIM:(hh + 1) * A_HEAD_DIM] = o[i * BLOCK:(i + 1) * BLOCK].astype(BF16)


def _win_attn(aq, akv, sink, B, S):
    N = S // BLOCK
    T = B * S
    kv = lambda shift: pl.BlockSpec(
        (BLOCK, 2 * A_KV_W), lambda b, n: (b * N + jnp.clip(n + shift, 0, N - 1), 0))
    return pl.pallas_call(
        functools.partial(_win_attn_kernel, n_blocks=N),
        out_shape=jax.ShapeDtypeStruct((T, A_Q_W), BF16), grid=(B, N),
        in_specs=[pl.BlockSpec(memory_space=pltpu.SMEM),
                  pl.BlockSpec((BLOCK, A_Q_W), lambda b, n: (b * N + n, 0)), kv(-1), kv(0), kv(1)],
        out_specs=pl.BlockSpec((BLOCK, A_Q_W), lambda b, n: (b * N + n, 0)),
        compiler_params=_params(2), name="win_attn")(sink, aq, akv, akv, akv)


def _ret_tables(lg_ref, lgl_ref, d_ref, xif_ref, xib_ref, zf_ref, zb_ref):
    i = lax.broadcasted_iota(I32, (BLOCK, BLOCK), 0)
    j = lax.broadcasted_iota(I32, (BLOCK, BLOCK), 1)
    rel = (i - j).astype(F32)
    for h in range(B_HEADS):
        d_ref[h] = jnp.where(i >= j, jnp.exp(rel * lg_ref[0, h]), jnp.exp(-rel * lg_ref[1, h]))
    t = lax.broadcasted_iota(I32, (BLOCK, B_W), 0).astype(F32)
    lgf, lgb = lgl_ref[0:1, :], lgl_ref[1:2, :]
    xif_ref[...] = jnp.exp((t + 1.0) * lgf)
    zf_ref[...] = jnp.exp((BLOCK - 1.0 - t) * lgf)
    xib_ref[...] = jnp.exp((BLOCK - t) * lgb)
    zb_ref[...] = jnp.exp(t * lgb)


def _ret_bwd_state_kernel(lgl_ref, k_ref, v_ref, r_out_ref, r_ref):
    n = pl.program_id(1)

    @pl.when(n == 0)
    def _():
        r_ref[...] = jnp.zeros_like(r_ref)

    r_out_ref[...] = r_ref[...]
    lgb = lgl_ref[1:2, :]
    t = lax.broadcasted_iota(I32, (BLOCK, B_W), 0).astype(F32)
    kz = (k_ref[...] * jnp.exp(t * lgb)).astype(BF16)
    cd = jnp.exp(BLOCK * lgb)
    v = v_ref[...]
    for h in range(B_HEADS):
        hs = slice(h * B_HEAD_DIM, (h + 1) * B_HEAD_DIM)
        r_ref[:, hs] = cd[:, hs] * r_ref[:, hs] + _dot_tn(kz[:, hs], v[:, hs])


def _ret_bwd_state(lgl, bk, bv, B, S):
    N = S // BLOCK
    blk = lambda: pl.BlockSpec((BLOCK, B_W), lambda b, n: (b * N + N - 1 - n, 0))
    return pl.pallas_call(
        _ret_bwd_state_kernel,
        out_shape=jax.ShapeDtypeStruct((B, N, B_HEAD_DIM, B_W), F32), grid=(B, N),
        in_specs=[pl.BlockSpec(lgl.shape, lambda b, n: (0, 0)), blk(), blk()],
        out_specs=pl.BlockSpec((None, None, B_HEAD_DIM, B_W), lambda b, n: (b, N - 1 - n, 0, 0)),
        scratch_shapes=[pltpu.VMEM((B_HEAD_DIM, B_W), F32)],
        compiler_params=_params(2), name="ret_bwd_state")(lgl, bk, bv)


def _ret_main_kernel(lg_ref, lgl_ref, q_ref, k_ref, v_ref, g_ref, rb_ref, o_ref,
                     rf_ref, d_ref, xif_ref, xib_ref, zf_ref, zb_ref):
    b, n = pl.program_id(0), pl.program_id(1)

    @pl.when((b == 0) & (n == 0))
    def _():
        _ret_tables(lg_ref, lgl_ref, d_ref, xif_ref, xib_ref, zf_ref, zb_ref)

    @pl.when(n == 0)
    def _():
        rf_ref[...] = jnp.zeros_like(rf_ref)

    q, k, v, gate = q_ref[...], k_ref[...], v_ref[...], g_ref[...]
    qb, kb = q.astype(BF16), k.astype(BF16)
    qxf = (q * xif_ref[...]).astype(BF16)
    qxb = (q * xib_ref[...]).astype(BF16)
    kzf = (k * zf_ref[...]).astype(BF16)
    cdf = jnp.exp(BLOCK * lgl_ref[0:1, :])
    for h in range(B_HEADS):
        hs = slice(h * B_HEAD_DIM, (h + 1) * B_HEAD_DIM)
        s = _dot_nt(qb[:, hs], kb[:, hs]) * d_ref[h]
        y = (_dot(s.astype(BF16), v[:, hs])
             + _dot(qxf[:, hs], rf_ref[:, hs].astype(BF16))
             + _dot(qxb[:, hs], rb_ref[:, hs].astype(BF16)))
        yc = y - jnp.mean(y, axis=-1, keepdims=True)
        yn = yc * lax.rsqrt(jnp.mean(yc * yc, axis=-1, keepdims=True) + GN_EPS)
        gh = gate[:, hs]
        o_ref[:, hs] = (gh * jax.nn.sigmoid(gh) * yn).astype(BF16)
        rf_ref[:, hs] = cdf[:, hs] * rf_ref[:, hs] + _dot_tn(kzf[:, hs], v[:, hs])


def _ret_main(lg, lgl, bq, bk, bv, bg, rb, B, S):
    N = S // BLOCK
    T = B * S
    blk = lambda: pl.BlockSpec((BLOCK, B_W), lambda b, n: (b * N + n, 0))
    tab = lambda: pltpu.VMEM((BLOCK, B_W), F32)
    return pl.pallas_call(
        _ret_main_kernel, out_shape=jax.ShapeDtypeStruct((T, B_W), BF16), grid=(B, N),
        in_specs=[pl.BlockSpec(memory_space=pltpu.SMEM), pl.BlockSpec(lgl.shape, lambda b, n: (0, 0)),
                  blk(), blk(), blk(), blk(),
                  pl.BlockSpec((None, None, B_HEAD_DIM, B_W), lambda b, n: (b, n, 0, 0))],
        out_specs=blk(),
        scratch_shapes=[pltpu.VMEM((B_HEAD_DIM, B_W), F32), pltpu.VMEM((B_HEADS, BLOCK, BLOCK), F32),
                        tab(), tab(), tab(), tab()],
        compiler_params=_params(2), name="ret_main")(lg, lgl, bq, bk, bv, bg, rb)


def even_mixer_parts(x2, positions, g_mix, w_in, sink, decay_logit, B, S, tm=512):
    ca, sa = _rope_tables(positions, ROPE_THETA, A_HEAD_DIM, 0, ROPE_DIM)
    cb, sb = _rope_tables(positions, RET_THETA, B_HEAD_DIM, 0, B_HEAD_DIM)
    aq, akv, bq, bk, bv, bg = _inproj_even(x2, g_mix.reshape(1, -1), w_in.astype(BF16), (ca, sa, cb, sb), tm)
    o_a = _win_attn(aq, akv, sink.astype(F32), B, S)
    lg = jax.nn.log_sigmoid(decay_logit.astype(F32))
    lgl = jnp.repeat(lg, B_HEAD_DIM, axis=1)
    rb = _ret_bwd_state(lgl, bk, bv, B, S)
    o_b = _ret_main(lg, lgl, bq, bk, bv, bg, rb, B, S)
    return o_a, o_b


def _outproj_kernel(o1_ref, o2_ref, w_ref, x_ref, g_ref, wr_ref, hx_ref, aff_ref):
    half = o1_ref.shape[1]
    D = x_ref.shape[1]
    x1 = x_ref[...] + (_dot(o1_ref[...], w_ref[:half, :]) + _dot(o2_ref[...], w_ref[half:, :]))
    hn = _rms(x1, g_ref[...])
    hx_ref[:, :D] = hn
    hx_ref[:, D:] = x1
    logits = _dot_nt(wr_ref[...], hn.astype(BF16))
    e = jnp.exp(logits - jnp.max(logits, axis=0, keepdims=True))
    aff_ref[...] = e / jnp.sum(e, axis=0, keepdims=True)


def _outproj_router(o1, o2, w_out, x2, g_ffn, w_router, B, S, tm):
    T, D = o1.shape[0], g_ffn.shape[-1]
    E = w_router.shape[1]
    spare = EC_CAPACITY_FACTOR * S // E
    per = S // tm
    row = lambda width: pl.BlockSpec((tm, width), lambda i: (i, 0))
    full = lambda a: pl.BlockSpec(a.shape, lambda i: (0, 0))
    w = w_out.astype(BF16)
    g = g_ffn.reshape(1, D)
    wr = w_router.T.astype(BF16)
    return pl.pallas_call(
        _outproj_kernel,
        out_shape=(jax.ShapeDtypeStruct((T + spare, 2 * D), F32), jax.ShapeDtypeStruct((B, E, S), F32)),
        grid=(T // tm,),
        in_specs=[row(o1.shape[1]), row(o2.shape[1]), full(w), _stream_spec(x2, D, tm), full(g), full(wr)],
        out_specs=(row(2 * D), pl.BlockSpec((None, E, tm), lambda i: (i // per, 0, i % per))),
        compiler_params=_params(1), name="outproj_router")(o1, o2, w, x2, g, wr)


def _split3(x):
    x1 = x.astype(BF16)
    r = x - x1.astype(F32)
    x2 = r.astype(BF16)
    x3 = (r - x2.astype(F32)).astype(BF16)
    return x1, x2, x3


def _topk_kernel(aff_ref, idx_ref, gate_ref, thr_ref, *, cap):
    E, R, _ = aff_ref.shape
    bits = lax.bitcast_convert_type(aff_ref[...], I32)

    def count(mask):
        return jnp.sum(jnp.sum(mask.astype(F32), axis=2, keepdims=True), axis=1, keepdims=True)

    def bit_body(i, prefix):
        cand = prefix | jnp.left_shift(jnp.int32(1), 30 - i)
        return jnp.where(count(bits >= cand) >= cap, cand, prefix)

    thr = lax.fori_loop(0, 31, bit_body, jnp.zeros((E, 1, 1), I32))
    thr_ref[...] = jnp.broadcast_to(thr, thr_ref.shape)

    li = lax.broadcasted_iota(I32, (LANES, LANES), 0)
    lj = lax.broadcasted_iota(I32, (LANES, LANES), 1)
    tri = (li <= lj).astype(BF16)
    ri = lax.broadcasted_iota(I32, (R, R), 0)
    rj = lax.broadcasted_iota(I32, (R, R), 1)
    below = (rj < ri).astype(BF16)
    slot = lax.broadcasted_iota(I32, (1, cap), 1).astype(F32)
    tok = (lax.broadcasted_iota(I32, (R, LANES), 0) * LANES
           + lax.broadcasted_iota(I32, (R, LANES), 1)).astype(F32)

    def prefix_counts(m):
        within = _dot(m.astype(BF16), tri)
        total = jnp.broadcast_to(within[:, LANES - 1:LANES], (R, LANES))
        before = _dot(below, total.astype(BF16))
        return within, total, before

    def expert(e, carry):
        a = aff_ref[e]
        b = lax.bitcast_convert_type(a, I32)
        t = thr_ref[e]
        gt, eq = b > t, b == t
        n_gt = jnp.sum(jnp.sum(gt.astype(F32), axis=1, keepdims=True), axis=0, keepdims=True)
        eqf = eq.astype(F32)
        within, _, before = prefix_counts(eqf)
        sel = gt | (eq & (before + within - eqf < cap - n_gt))
        self_ = sel.astype(F32)
        within, total, before = prefix_counts(self_)
        rank = before + within
        first, count_r = before[:, 0:1], total[:, 0:1]
        owner = ((first <= slot) & (slot < first + count_r)).astype(BF16)

        def row_of_slot(x):
            return sum(_dot_tn(p, owner) for p in _split3(x))

        hit = (row_of_slot(self_) > 0.5) & (row_of_slot(rank) == slot + 1.0)
        idx_ref[e] = jnp.sum(jnp.where(hit, row_of_slot(tok), 0.0), axis=0, keepdims=True).astype(I32)
        gate_ref[e] = jnp.sum(jnp.where(hit, row_of_slot(a), 0.0), axis=0, keepdims=True)
        return carry

    lax.fori_loop(0, E, expert, 0)


def _topk(aff, cap):
    B, E, S = aff.shape
    R = S // LANES
    aff4 = aff.reshape(B, E, R, LANES)
    out = jax.ShapeDtypeStruct((B, E, 1, cap), I32), jax.ShapeDtypeStruct((B, E, 1, cap), F32)
    spec = pl.BlockSpec((None, E, 1, cap), lambda b: (b, 0, 0, 0))
    return pl.pallas_call(
        functools.partial(_topk_kernel, cap=cap), out_shape=out, grid=(B,),
        in_specs=[pl.BlockSpec((None, E, R, LANES), lambda b: (b, 0, 0, 0))],
        out_specs=(spec, spec), scratch_shapes=[pltpu.VMEM((E, 1, LANES), I32)],
        compiler_params=_params(1), name="topk")(aff4)


ROW_UNROLL = 8
M_BLOCKS = 4
SEM_IN, SEM_OUT = 0, 1


N_SLOTS = 3


def _moe_kernel(idx_prev_ref, idx_ref, idx_next_ref, gate_ref, wg_ref, wu_ref, wd_ref, hx_alias, hx_hbm,
                buf, sems, *, seq, cap, f_chunk, n_tokens):
    del hx_alias
    D = wg_ref.shape[0]
    n_batch = pl.num_programs(1)
    step = pl.program_id(0) * n_batch + pl.program_id(1)
    n_steps = pl.num_programs(0) * n_batch
    cur, nxt, prv = step % N_SLOTS, (step + 1) % N_SLOTS, (step + 2) % N_SLOTS
    res = pl.ds(D, D)

    def for_rows(fn):
        @pl.loop(0, cap // ROW_UNROLL)
        def _(g):
            for u in range(ROW_UNROLL):
                fn(g * ROW_UNROLL + u)

    def gather_row(t, s, j):
        pltpu.make_async_copy(hx_hbm.at[pl.ds(t, 1)], buf.at[s, pl.ds(j, 1)], sems.at[SEM_IN, s]).start()

    def write_row(t, s, j):
        pltpu.make_async_copy(buf.at[s, pl.ds(j, 1), res], hx_hbm.at[pl.ds(t, 1), res], sems.at[SEM_OUT, s]).start()

    def wait_gathers(s):
        pltpu.make_async_copy(hx_hbm.at[pl.ds(0, cap)], buf.at[s], sems.at[SEM_IN, s]).wait()

    def wait_writes(s):
        pltpu.make_async_copy(buf.at[s, :, res], hx_hbm.at[pl.ds(0, cap), res], sems.at[SEM_OUT, s]).wait()

    this_base = pl.program_id(1) * seq
    next_base = ((step + 1) % n_batch) * seq
    prev_base = ((step + n_batch - 1) % n_batch) * seq
    first = step == 0

    @pl.when(first)
    def _():
        for_rows(lambda j: gather_row(this_base + idx_ref[0, j], cur, j))
        buf[prv, :, D:] = jnp.zeros((cap, D), F32)

    @pl.when(step > 0)
    def _():
        wait_writes(nxt)

    wait_gathers(cur)
    xin = buf[cur, :, :D].astype(BF16)
    n_chunks = wg_ref.shape[1] // f_chunk
    n_rows = cap // M_BLOCKS
    ahead = cap // (n_chunks * M_BLOCKS)
    accs = [jnp.zeros((n_rows, D), F32) for _ in range(M_BLOCKS)]
    piece = 0
    for f in range(n_chunks):
        fs = slice(f * f_chunk, (f + 1) * f_chunk)
        for mb in range(M_BLOCKS):
            for j in range(piece * ahead, (piece + 1) * ahead):
                gather_row(next_base + idx_next_ref[0, j], nxt, j)
                write_row(jnp.where(first, n_tokens + j, prev_base + idx_prev_ref[0, j]), prv, j)
            piece += 1
            xs = xin[mb * n_rows:(mb + 1) * n_rows]
            g = _dot(xs, wg_ref[:, fs])
            hid = (g * jax.nn.sigmoid(g) * _dot(xs, wu_ref[:, fs])).astype(BF16)
            accs[mb] = accs[mb] + _dot(hid, wd_ref[fs, :])
    acc = jnp.concatenate(accs, axis=0)
    diag = (lax.broadcasted_iota(I32, (cap, cap), 0) == lax.broadcasted_iota(I32, (cap, cap), 1))
    gate_col = jnp.sum(jnp.where(diag, gate_ref[...], 0.0), axis=1, keepdims=True)
    buf[cur, :, D:] = buf[cur, :, D:] + acc * gate_col

    @pl.when(step == n_steps - 1)
    def _():
        wait_gathers(nxt)
        wait_writes(prv)
        for_rows(lambda j: write_row(this_base + idx_ref[0, j], cur, j))
        wait_writes(cur)


def _moe_ffn(hx, idx, gate, w_gate, w_up, w_down, B, S, f_chunk=512):
    E, D, F = w_gate.shape
    cap = idx.shape[-1]
    T = hx.shape[0] - cap
    f_chunk = min(f_chunk, F)
    assert B >= N_SLOTS and cap % ROW_UNROLL == 0 and cap % (M_BLOCKS * (F // f_chunk)) == 0
    wspec = lambda a: pl.BlockSpec((None,) + a.shape[1:], lambda e, b: (e, 0, 0), pipeline_mode=pl.Buffered(1))
    any_spec = pl.BlockSpec(memory_space=pl.ANY)

    def ids(shift):
        def index_map(e, b):
            s = jnp.clip(e * B + b + shift, 0, E * B - 1)
            return (s % B, s // B, 0, 0)
        return pl.BlockSpec((None, None, 1, cap), index_map, memory_space=pltpu.SMEM)

    return pl.pallas_call(
        functools.partial(_moe_kernel, seq=S, cap=cap, f_chunk=f_chunk, n_tokens=T),
        out_shape=jax.ShapeDtypeStruct(hx.shape, F32), grid=(E, B),
        in_specs=[ids(-1), ids(0), ids(1),
                  pl.BlockSpec((None, None, 1, cap), lambda e, b: (b, e, 0, 0)),
                  wspec(w_gate), wspec(w_up), wspec(w_down), any_spec],
        out_specs=any_spec,
        scratch_shapes=[pltpu.VMEM((N_SLOTS, cap, 2 * D), F32), pltpu.SemaphoreType.DMA((2, N_SLOTS))],
        input_output_aliases={7: 0},
        compiler_params=_params(2), name="moe_ffn")(idx, idx, idx, gate, w_gate, w_up, w_down, hx)


def moe_layer(o1, o2, w_out, x2, g_ffn, w_router, w_gate, w_up, w_down, B, S, tm=512):
    hx, aff = _outproj_router(o1, o2, w_out, x2, g_ffn, w_router, B, S, tm)
    cap = EC_CAPACITY_FACTOR * S // w_router.shape[1]
    idx, gate = _topk(aff, cap)
    return _moe_ffn(hx, idx, gate, w_gate.astype(BF16), w_up.astype(BF16), w_down.astype(BF16), B, S)


def _final_norm_kernel(x_ref, g_ref, o_ref):
    o_ref[...] = _rms(x_ref[...], g_ref[...])


def _final_norm(x2, g, tm, T):
    D = g.shape[-1]
    row = pl.BlockSpec((tm, D), lambda i: (i, 0))
    return pl.pallas_call(
        _final_norm_kernel, out_shape=jax.ShapeDtypeStruct((T, D), F32), grid=(T // tm,),
        in_specs=[_stream_spec(x2, D, tm), pl.BlockSpec((1, D), lambda i: (0, 0))], out_specs=row,
        compiler_params=_params(1), name="final_norm")(x2, g.reshape(1, D))


ODD_SPLITS = (C_Q_LORA, C_KV_LORA, C_ROPE, D_W, D_W, D_W, D_W, 4 * D_HEADS)
N_GATES = 4 * D_HEADS
I_FWD, F_FWD, I_BWD, F_BWD = 0, D_HEADS, 2 * D_HEADS, 3 * D_HEADS


def _odd_weights(w_in):
    cq, ckv, kr, dq, dk, dv, do, dg = jnp.split(w_in, [int(c) for c in np.cumsum(ODD_SPLITS)[:-1]], axis=1)
    D = w_in.shape[0]
    zeros = lambda n: jnp.zeros((D, n), w_in.dtype)
    kr_slot = jnp.concatenate([zeros(C_NOPE), kr, zeros(C_SLOT - C_NOPE - C_ROPE)], axis=1)
    dg_slot = jnp.concatenate([dg, zeros(LANES - N_GATES)], axis=1)
    w = jnp.concatenate([cq, ckv, kr_slot, dq, dk, dv, do, dg_slot], axis=1)
    return w.astype(BF16), dg.T.astype(BF16)


def _inproj_odd_kernel(x_ref, g_ref, w_ref, wgt_ref,
                       cq_ref, ckv_ref, kr_ref, dqk_ref, dv_ref, do_ref, gc_ref, gt_ref):
    h = _rms(x_ref[...], g_ref[...]).astype(BF16)
    off = 0
    for ref, width in ((cq_ref, C_Q_LORA), (ckv_ref, C_KV_LORA), (kr_ref, C_SLOT), (dqk_ref, 2 * D_W),
                       (dv_ref, D_W), (do_ref, D_W), (gc_ref, LANES)):
        ref[...] = _dot(h, w_ref[:, off:off + width]).astype(ref.dtype)
        off += width
    gt_ref[...] = _dot_nt(wgt_ref[...], h)


def _inproj_odd(x2, g, w, wgt, tm, T):
    D = g.shape[-1]
    row = lambda width: pl.BlockSpec((tm, width), lambda i: (i, 0))
    full = lambda a: pl.BlockSpec(a.shape, lambda i: (0, 0))
    widths = (C_Q_LORA, C_KV_LORA, C_SLOT, 2 * D_W, D_W, D_W, LANES)
    dtypes = (F32, F32, F32, F32, BF16, F32, F32)
    out_shape = tuple(jax.ShapeDtypeStruct((T, wd), dt) for wd, dt in zip(widths, dtypes))
    out_shape += (jax.ShapeDtypeStruct((N_GATES, T), F32),)
    return pl.pallas_call(
        _inproj_odd_kernel, out_shape=out_shape, grid=(T // tm,),
        in_specs=[_stream_spec(x2, D, tm), full(g), full(w), full(wgt)],
        out_specs=tuple(row(wd) for wd in widths) + (pl.BlockSpec((N_GATES, tm), lambda i: (0, i)),),
        compiler_params=_params(1), name="inproj_odd")(x2, g, w, wgt)


def _mla_prep_kernel(cq_ref, ckv_ref, kr_ref, nq_ref, nkv_ref, wq_ref, wk_ref, wv_ref, cos_ref, sin_ref,
                     q_ref, k_ref, vt_ref):
    lane = lax.broadcasted_iota(I32, (1, LANES), 1)
    first = lane < C_NOPE + C_ROPE // 2
    cos, sin = cos_ref[...], sin_ref[...]
    rope = lambda z: _rope_slab(z, cos, sin, C_ROPE // 2, first)
    q = _dot(_rms(cq_ref[...], nq_ref[...]).astype(BF16), wq_ref[...])
    hkv = _rms(ckv_ref[...], nkv_ref[...]).astype(BF16)
    kn = _dot(hkv, wk_ref[...])
    kr = rope(kr_ref[...])
    for hh in range(C_HEADS):
        slab = slice(hh * C_SLOT, (hh + 1) * C_SLOT)
        q_ref[:, slab] = rope(q[:, slab]).astype(BF16)
        k_ref[:, slab] = (kn[:, slab] + kr).astype(BF16)
    vt = _dot_nt(wv_ref[...], hkv)
    row = lax.broadcasted_iota(I32, vt.shape, 0)
    vt_ref[...] = jnp.where(row % C_SLOT < C_V, vt, 1.0).astype(BF16)


def _mla_prep(cq, ckv, kr, nq, nkv, w_uq, w_ukv, cos, sin, tm):
    T = cq.shape[0]
    pad_q = C_SLOT - C_NOPE - C_ROPE
    wq = jnp.pad(w_uq.reshape(C_Q_LORA, C_HEADS, C_NOPE + C_ROPE), ((0, 0), (0, 0), (0, pad_q)))
    wq = wq.reshape(C_Q_LORA, C_HEADS * C_SLOT).astype(BF16)
    wkv = w_ukv.reshape(C_KV_LORA, C_HEADS, C_NOPE + C_V)
    wk = jnp.pad(wkv[:, :, :C_NOPE], ((0, 0), (0, 0), (0, C_SLOT - C_NOPE)))
    wk = wk.reshape(C_KV_LORA, C_HEADS * C_SLOT).astype(BF16)
    wv = jnp.pad(wkv[:, :, C_NOPE:], ((0, 0), (0, 0), (0, C_SLOT - C_V)))
    wv = wv.reshape(C_KV_LORA, C_HEADS * C_SLOT).T.astype(BF16)
    nq, nkv = nq.reshape(1, -1), nkv.reshape(1, -1)
    row = lambda width: pl.BlockSpec((tm, width), lambda i: (i, 0))
    full = lambda a: pl.BlockSpec(a.shape, lambda i: (0, 0))
    slots = jax.ShapeDtypeStruct((T, C_HEADS * C_SLOT), BF16)
    vt_shape = jax.ShapeDtypeStruct((T // tm, C_HEADS * C_SLOT, tm), BF16)
    return pl.pallas_call(
        _mla_prep_kernel, out_shape=(slots, slots, vt_shape), grid=(T // tm,),
        in_specs=[row(C_Q_LORA), row(C_KV_LORA), row(C_SLOT), full(nq), full(nkv), full(wq), full(wk), full(wv),
                  row(LANES), row(LANES)],
        out_specs=(row(C_HEADS * C_SLOT), row(C_HEADS * C_SLOT),
                   pl.BlockSpec((None, C_HEADS * C_SLOT, tm), lambda i: (i, 0, 0))),
        compiler_params=_params(1), name="mla_prep")(cq, ckv, kr, nq, nkv, wq, wk, wv, cos, sin)


C_PAIR = 2


def _mla_attn_kernel(q_ref, k_ref, vt_ref, o_ref):
    tq = q_ref.shape[0]
    n_chunks, _, key_chunk = vt_ref.shape
    mult = (C_NOPE + C_ROPE) ** -0.5 * np.log2(np.e)
    for hh in range(C_PAIR):
        slab = slice(hh * C_SLOT, (hh + 1) * C_SLOT)
        q = q_ref[:, slab]

        def scores(c):
            return _dot_nt(k_ref[c * key_chunk:(c + 1) * key_chunk, slab], q)

        m = jnp.full((1, tq), NEG, F32)
        acc = jnp.zeros((C_SLOT, tq), F32)
        st = scores(0)
        for c in range(n_chunks):
            st_next = scores(c + 1) if c + 1 < n_chunks else None
            m_new = jnp.maximum(m, jnp.max(st, axis=0, keepdims=True))
            pt = jnp.exp2(((st - m_new) * mult).astype(BF16))
            acc = jnp.exp2((m - m_new) * mult) * acc + _dot(vt_ref[c, slab, :], pt)
            m, st = m_new, st_next
        acc = acc.T
        o_ref[:, hh * C_V:(hh + 1) * C_V] = (acc[:, :C_V] / acc[:, C_V:C_V + 1]).astype(BF16)


def _mla_attn(q, k, vt, B, S, tq):
    T = B * S
    nq = S // tq
    key_chunk = vt.shape[-1]
    vt = vt.reshape(B, S // key_chunk, C_HEADS * C_SLOT, key_chunk)
    return pl.pallas_call(
        _mla_attn_kernel,
        out_shape=jax.ShapeDtypeStruct((T, C_HEADS * C_V), BF16), grid=(B, C_HEADS // C_PAIR, nq),
        in_specs=[pl.BlockSpec((tq, C_PAIR * C_SLOT), lambda b, p, i: (b * nq + i, p)),
                  pl.BlockSpec((S, C_PAIR * C_SLOT), lambda b, p, i: (b, p)),
                  pl.BlockSpec((None, S // key_chunk, C_PAIR * C_SLOT, key_chunk), lambda b, p, i: (b, 0, p, 0))],
        out_specs=pl.BlockSpec((tq, C_PAIR * C_V), lambda b, p, i: (b * nq + i, p)),
        compiler_params=_params(3), name="mla_attn")(q, k, vt)


HALO = 8


def _conv_kernel(xp_ref, x_ref, xn_ref, w_ref, o_ref, *, n_tiles):
    j = pl.program_id(1)
    tc = x_ref.shape[0]
    prev = jnp.where(j > 0, xp_ref[...], 0.0)
    nxt = jnp.where(j < n_tiles - 1, xn_ref[...], 0.0)
    ext = jnp.concatenate([prev, x_ref[...], nxt], axis=0)
    rows = tc + 2 * HALO
    y = jnp.zeros(x_ref.shape, F32)
    for w in range(D_CONV):
        first = HALO - D_CONV // 2 + w
        y = y + pltpu.roll(ext, (rows - first) % rows, 0)[:tc] * w_ref[w:w + 1, :]
    y = y * jax.nn.sigmoid(y)
    o_ref[:, :D_W] = y[:, :D_W]
    o_ref[:, D_W:] = y[:, D_W:] * (D_HEAD_DIM ** -0.5)


def _conv_prep(dqk, conv_w, B, S, tc):
    T, C = dqk.shape
    n_tiles = S // tc
    per, last = tc // HALO, T // HALO - 1
    cur = lambda b, j: (b * n_tiles + j, 0)
    return pl.pallas_call(
        functools.partial(_conv_kernel, n_tiles=n_tiles),
        out_shape=jax.ShapeDtypeStruct((T, C), F32), grid=(B, n_tiles),
        in_specs=[pl.BlockSpec((HALO, C), lambda b, j: (jnp.maximum((b * n_tiles + j) * per - 1, 0), 0)),
                  pl.BlockSpec((tc, C), cur),
                  pl.BlockSpec((HALO, C), lambda b, j: (jnp.minimum((b * n_tiles + j + 1) * per, last), 0)),
                  pl.BlockSpec(conv_w.shape, lambda b, j: (0, 0))],
        out_specs=pl.BlockSpec((tc, C), cur),
        compiler_params=_params(2), name="conv_prep")(dqk, dqk, dqk, conv_w)


def _log_sigmoid(x):
    return -(jnp.maximum(-x, 0.0) + jnp.log1p(jnp.exp(-jnp.abs(x))))


def _tri(lower):
    a = lax.broadcasted_iota(I32, (BLOCK, BLOCK), 0)
    b = lax.broadcasted_iota(I32, (BLOCK, BLOCK), 1)
    return ((b <= a) if lower else (b >= a)).astype(BF16)


def _mlstm_update(s_ref, m_ref, h, g, a, k, v_ones):
    a_max = jnp.max(a, axis=0, keepdims=True)
    w = jnp.exp(a - a_max)
    upd = _dot_tn((k * w).astype(BF16), v_ones)
    m = m_ref[h:h + 1, 0:1]
    m_new = jnp.maximum(g + m, a_max)
    cols = slice(h * 2 * D_HEAD_DIM, (h + 1) * 2 * D_HEAD_DIM)
    s_ref[:, cols] = jnp.exp(g + m - m_new) * s_ref[:, cols] + jnp.exp(a_max - m_new) * upd
    m_ref[h:h + 1, :] = jnp.broadcast_to(m_new, (1, LANES))


def _mlstm_bwd_state_kernel(bias_c_ref, qk_ref, v_ref, gc_ref, s_out_ref, m_out_ref, s_ref, m_ref):
    @pl.when(pl.program_id(1) == 0)
    def _():
        s_ref[...] = jnp.zeros_like(s_ref)
        m_ref[...] = jnp.zeros_like(m_ref)

    s_out_ref[...] = s_ref[...]
    m_out_ref[...] = m_ref[...]
    gc = gc_ref[...] + bias_c_ref[...]
    suffix = sum(_dot(_tri(False), p) for p in _split3(_log_sigmoid(gc)))
    ones = jnp.ones((BLOCK, D_HEAD_DIM), BF16)
    for h in range(D_HEADS):
        hs = slice(h * D_HEAD_DIM, (h + 1) * D_HEAD_DIM)
        sb = suffix[:, F_BWD + h:F_BWD + h + 1]
        g = sb[0:1, :]
        a = g - sb + gc[:, I_BWD + h:I_BWD + h + 1]
        k = qk_ref[:, D_W + h * D_HEAD_DIM:D_W + (h + 1) * D_HEAD_DIM]
        _mlstm_update(s_ref, m_ref, h, g, a, k, jnp.concatenate([v_ref[:, hs], ones], axis=1))


def _mlstm_bwd_state(bias_c, qk, dv, gc, B, S):
    N = S // BLOCK
    rev = lambda b, n: (b * N + N - 1 - n, 0)
    out_shape = (jax.ShapeDtypeStruct((B, N, D_HEAD_DIM, 2 * D_W), F32), jax.ShapeDtypeStruct((B, N, 8, LANES), F32))
    return pl.pallas_call(
        _mlstm_bwd_state_kernel, out_shape=out_shape, grid=(B, N),
        in_specs=[pl.BlockSpec(bias_c.shape, lambda b, n: (0, 0)), pl.BlockSpec((BLOCK, 2 * D_W), rev),
                  pl.BlockSpec((BLOCK, D_W), rev), pl.BlockSpec((BLOCK, LANES), rev)],
        out_specs=(pl.BlockSpec((None, None, D_HEAD_DIM, 2 * D_W), lambda b, n: (b, N - 1 - n, 0, 0)),
                   pl.BlockSpec((None, None, 8, LANES), lambda b, n: (b, N - 1 - n, 0, 0))),
        scratch_shapes=[pltpu.VMEM((D_HEAD_DIM, 2 * D_W), F32), pltpu.VMEM((8, LANES), F32)],
        compiler_params=_params(2), name="mlstm_bwd_state")(bias_c, qk, dv, gc)


def _mlstm_main_kernel(bias_c_ref, bias_r_ref, qk_ref, v_ref, og_ref, gc_ref, gt_ref, sb_ref, mb_ref,
                       o_ref, s_ref, m_ref):
    @pl.when(pl.program_id(1) == 0)
    def _():
        s_ref[...] = jnp.zeros_like(s_ref)
        m_ref[...] = jnp.zeros_like(m_ref)

    gc = gc_ref[...] + bias_c_ref[...]
    gr = gt_ref[...] + bias_r_ref[...]
    lower, upper = _tri(True), _tri(False)
    lfc, lfr = _split3(_log_sigmoid(gc)), _split3(_log_sigmoid(gr))
    pre_c = sum(_dot(lower, p) for p in lfc)
    suf_c = sum(_dot(upper, p) for p in lfc)
    pre_r = sum(_dot(p, upper) for p in lfr)
    suf_r = sum(_dot(p, lower) for p in lfr)
    ti = lax.broadcasted_iota(I32, (BLOCK, BLOCK), 0)
    tj = lax.broadcasted_iota(I32, (BLOCK, BLOCK), 1)
    ones = jnp.ones((BLOCK, D_HEAD_DIM), BF16)
    og = og_ref[...]
    for h in range(D_HEADS):
        hs = slice(h * D_HEAD_DIM, (h + 1) * D_HEAD_DIM)
        cols = slice(h * 2 * D_HEAD_DIM, (h + 1) * 2 * D_HEAD_DIM)
        k = qk_ref[:, D_W + h * D_HEAD_DIM:D_W + (h + 1) * D_HEAD_DIM]
        qb, v = qk_ref[:, hs].astype(BF16), v_ref[:, hs]
        qkt = _dot_nt(qb, k.astype(BF16))

        def direction(b_col, b_row, i_row, mask, state, m_prev):
            logd = jnp.where(mask, b_col - b_row + i_row, NEG)
            log_inter = b_col + m_prev
            m_t = jnp.maximum(jnp.max(logd, axis=1, keepdims=True), log_inter)
            s = qkt * jnp.exp(logd - m_t)
            inter_w = jnp.exp(log_inter - m_t)
            qs = _dot(qb, state.astype(BF16))
            num = _dot(s.astype(BF16), v) + inter_w * qs[:, :D_HEAD_DIM]
            den = jnp.sum(s, axis=1, keepdims=True) + inter_w * qs[:, D_HEAD_DIM:D_HEAD_DIM + 1]
            return num / jnp.maximum(jnp.abs(den), jnp.exp(-m_t))

        f, bk = F_FWD + h, F_BWD + h
        h_f = direction(pre_c[:, f:f + 1], pre_r[f:f + 1, :], gr[I_FWD + h:I_FWD + h + 1, :], tj <= ti,
                        s_ref[:, cols], m_ref[h:h + 1, 0:1])
        h_b = direction(suf_c[:, bk:bk + 1], suf_r[bk:bk + 1, :], gr[I_BWD + h:I_BWD + h + 1, :], tj > ti,
                        sb_ref[:, cols], mb_ref[h:h + 1, 0:1])
        o_ref[:, hs] = (jax.nn.sigmoid(og[:, hs]) * (h_f + h_b)).astype(BF16)
        b_col = pre_c[:, f:f + 1]
        g = b_col[BLOCK - 1:BLOCK, :]
        a = g - b_col + gc[:, I_FWD + h:I_FWD + h + 1]
        _mlstm_update(s_ref, m_ref, h, g, a, k, jnp.concatenate([v, ones], axis=1))


def _mlstm_main(bias_c, bias_r, qk, dv, og, gc, gt, sb, mb, B, S):
    N = S // BLOCK
    T = B * S
    cur = lambda b, n: (b * N + n, 0)
    return pl.pallas_call(
        _mlstm_main_kernel, out_shape=jax.ShapeDtypeStruct((T, D_W), BF16), grid=(B, N),
        in_specs=[pl.BlockSpec(bias_c.shape, lambda b, n: (0, 0)), pl.BlockSpec(bias_r.shape, lambda b, n: (0, 0)),
                  pl.BlockSpec((BLOCK, 2 * D_W), cur), pl.BlockSpec((BLOCK, D_W), cur), pl.BlockSpec((BLOCK, D_W), cur),
                  pl.BlockSpec((BLOCK, LANES), cur), pl.BlockSpec((N_GATES, BLOCK), lambda b, n: (0, b * N + n)),
                  pl.BlockSpec((None, None, D_HEAD_DIM, 2 * D_W), lambda b, n: (b, n, 0, 0)),
                  pl.BlockSpec((None, None, 8, LANES), lambda b, n: (b, n, 0, 0))],
        out_specs=pl.BlockSpec((BLOCK, D_W), cur),
        scratch_shapes=[pltpu.VMEM((D_HEAD_DIM, 2 * D_W), F32), pltpu.VMEM((8, LANES), F32)],
        compiler_params=_params(2), name="mlstm_main")(bias_c, bias_r, qk, dv, og, gc, gt, sb, mb)


def odd_mixer_parts(x2, positions, g_mix, w_in, norm_q, norm_kv, w_uq, w_ukv, conv_w, gate_bias, B, S,
                    tm=512, tq=512, key_chunk=1024):
    w, wgt = _odd_weights(w_in)
    cq, ckv, kr, dqk, dv, og, gc, gt = _inproj_odd(x2, g_mix.reshape(1, -1), w, wgt, tm, B * S)
    cos, sin = _rope_tables(positions, MLA_THETA, C_SLOT, C_NOPE, C_ROPE)
    q, k, vt = _mla_prep(cq, ckv, kr, norm_q, norm_kv, w_uq, w_ukv, cos, sin, min(key_chunk, S))
    o_c = _mla_attn(q, k, vt, B, S, min(tq, S))
    qk = _conv_prep(dqk, conv_w.astype(F32), B, S, min(tm, S))
    bias = gate_bias.astype(F32).reshape(1, N_GATES)
    bias_c = jnp.pad(bias, ((0, 0), (0, LANES - N_GATES)))
    bias_r = bias.reshape(N_GATES, 1)
    sb, mb = _mlstm_bwd_state(bias_c, qk, dv, gc, B, S)
    o_d = _mlstm_main(bias_c, bias_r, qk, dv, og, gc, gt, sb, mb, B, S)
    return o_c, o_d


def kernel(x, positions, norm_mix, norm_ffn, norm_final, ev_w_in, ev_w_out, attn_sink, ret_decay_logit,
           od_w_in, od_w_out, mla_norm_q, mla_norm_kv, mla_w_uq, mla_w_ukv, mlstm_conv, mlstm_gate_bias,
           moe_router, moe_w_gate, moe_w_up, moe_w_down):
    B, S, D = x.shape
    tm = min(512, S)
    x2 = x.reshape(B * S, D)
    for layer in range(norm_mix.shape[0]):
        j = layer // 2
        if layer % 2 == 0:
            o1, o2 = even_mixer_parts(x2, positions, norm_mix[layer], ev_w_in[j], attn_sink[j],
                                      ret_decay_logit[j], B, S, tm)
            w_out = ev_w_out[j]
        else:
            o1, o2 = odd_mixer_parts(x2, positions, norm_mix[layer], od_w_in[j], mla_norm_q[j], mla_norm_kv[j],
                                     mla_w_uq[j], mla_w_ukv[j], mlstm_conv[j], mlstm_gate_bias[j], B, S, tm)
            w_out = od_w_out[j]
        x2 = moe_layer(o1, o2, w_out, x2, norm_ffn[layer], moe_router[layer], moe_w_gate[layer],
                       moe_w_up[layer], moe_w_down[layer], B, S, tm)
    return _final_norm(x2, norm_final, tm, B * S).reshape(B, S, D)
```

```python
import functools

import jax
import jax.numpy as jnp
import numpy as np
from jax import lax
from jax.experimental import pallas as pl
from jax.experimental.pallas import tpu as pltpu

F32 = jnp.float32
BF16 = jnp.bfloat16
I32 = jnp.int32

LANES = 128
BLOCK = 128
RMS_EPS = 1e-6
GN_EPS = 1e-5
NEG = -1e30
VMEM_LIMIT = 56 * 1024 * 1024

A_HEADS, A_KV_HEADS, A_HEAD_DIM = 8, 2, 64
ROPE_THETA = 500000.0
ROPE_DIM = A_HEAD_DIM // 4
B_HEADS, B_HEAD_DIM = 8, 64
RET_THETA = 10000.0
C_HEADS, C_NOPE, C_ROPE, C_V = 8, 64, 32, 64
C_Q_LORA, C_KV_LORA = 512, 256
MLA_THETA = 10000.0
D_HEADS, D_HEAD_DIM, D_CONV = 4, 128, 5
N_EXPERTS = 16
EC_CAPACITY_FACTOR = 2

A_Q_W = A_HEADS * A_HEAD_DIM
A_KV_W = A_KV_HEADS * A_HEAD_DIM
B_W = B_HEADS * B_HEAD_DIM
D_W = D_HEADS * D_HEAD_DIM
C_SLOT = 128


def _params(n_axes, vmem=VMEM_LIMIT):
    return pltpu.CompilerParams(dimension_semantics=("arbitrary",) * n_axes,
                                vmem_limit_bytes=vmem)


def _rms(x, g):
    return x * lax.rsqrt(jnp.mean(x * x, axis=-1, keepdims=True) + RMS_EPS) * g


def _dot(a, b):
    return jnp.dot(a, b, preferred_element_type=F32)


def _dot_nt(a, b):
    return lax.dot_general(a, b, (((1,), (1,)), ((), ())), preferred_element_type=F32)


def _dot_tn(a, b):
    return lax.dot_general(a, b, (((0,), (0,)), ((), ())), preferred_element_type=F32)


def _rope_slab(z, cos, sin, half, first_half):
    partner = jnp.where(first_half, pltpu.roll(z, LANES - half, 1), pltpu.roll(z, half, 1))
    return z * cos + partner * sin


def _rope_tables(positions, theta, head_dim, rot_start, rot_dim):
    half = rot_dim // 2
    assert head_dim % half == 0 and rot_start % half == 0
    inv_freq = theta ** (-jnp.arange(half, dtype=F32) * 2.0 / rot_dim)
    d = np.arange(LANES) % head_dim - rot_start
    rot = (d >= 0) & (d < rot_dim)
    sign = np.where(d < half, -1.0, 1.0).astype(np.float32)
    ang = positions.astype(F32).reshape(-1, 1) * inv_freq
    ang = jnp.tile(ang, (1, LANES // half))
    cos = jnp.where(rot[None, :], jnp.cos(ang), 1.0)
    sin = jnp.where(rot[None, :], jnp.sin(ang) * sign[None, :], 0.0)
    return cos, sin


def _inproj_even_kernel(x_ref, g_ref, w_ref, ca_ref, sa_ref, cb_ref, sb_ref,
                        aq_ref, akv_ref, bq_ref, bk_ref, bv_ref, bg_ref):
    h = _rms(x_ref[...], g_ref[...]).astype(BF16)
    lane = lax.broadcasted_iota(I32, (1, LANES), 1)
    first_a = (lane % A_HEAD_DIM) < (ROPE_DIM // 2)
    first_b = (lane % B_HEAD_DIM) < (B_HEAD_DIM // 2)
    ca, sa, cb, sb = ca_ref[...], sa_ref[...], cb_ref[...], sb_ref[...]

    def rope_a(z):
        return _rope_slab(z, ca, sa, ROPE_DIM // 2, first_a)

    def rope_b(z):
        return _rope_slab(z, cb, sb, B_HEAD_DIM // 2, first_b)

    off = 0
    z = _dot(h, w_ref[:, off:off + A_Q_W])
    for s in range(A_Q_W // LANES):
        aq_ref[:, s * LANES:(s + 1) * LANES] = rope_a(z[:, s * LANES:(s + 1) * LANES]).astype(BF16)
    off += A_Q_W
    z = _dot(h, w_ref[:, off:off + 2 * A_KV_W])
    akv_ref[:, :A_KV_W] = rope_a(z[:, :A_KV_W]).astype(BF16)
    akv_ref[:, A_KV_W:] = z[:, A_KV_W:].astype(BF16)
    off += 2 * A_KV_W
    z = _dot(h, w_ref[:, off:off + B_W])
    for s in range(B_W // LANES):
        bq_ref[:, s * LANES:(s + 1) * LANES] = rope_b(z[:, s * LANES:(s + 1) * LANES])
    off += B_W
    z = _dot(h, w_ref[:, off:off + B_W])
    for s in range(B_W // LANES):
        bk_ref[:, s * LANES:(s + 1) * LANES] = rope_b(z[:, s * LANES:(s + 1) * LANES]) * (B_HEAD_DIM ** -0.5)
    off += B_W
    bv_ref[...] = _dot(h, w_ref[:, off:off + B_W]).astype(BF16)
    off += B_W
    bg_ref[...] = _dot(h, w_ref[:, off:off + B_W])


def _stream_spec(xs, D, tm):
    col = xs.shape[1] // D - 1
    return pl.BlockSpec((tm, D), lambda i: (i, col))


def _inproj_even(x2, g, w, tabs, tm):
    T, D = tabs[0].shape[0], g.shape[-1]
    ncol = w.shape[1]
    row = lambda width: pl.BlockSpec((tm, width), lambda i: (i, 0))
    full = lambda a: pl.BlockSpec(a.shape, lambda i: (0, 0))
    out_shape = (jax.ShapeDtypeStruct((T, A_Q_W), BF16), jax.ShapeDtypeStruct((T, 2 * A_KV_W), BF16),
                 jax.ShapeDtypeStruct((T, B_W), F32), jax.ShapeDtypeStruct((T, B_W), F32),
                 jax.ShapeDtypeStruct((T, B_W), BF16), jax.ShapeDtypeStruct((T, B_W), F32))
    return pl.pallas_call(
        _inproj_even_kernel, out_shape=out_shape, grid=(T // tm,),
        in_specs=[_stream_spec(x2, D, tm), full(g), pl.BlockSpec((D, ncol), lambda i: (0, 0))] + [row(LANES)] * 4,
        out_specs=(row(A_Q_W), row(2 * A_KV_W), row(B_W), row(B_W), row(B_W), row(B_W)),
        compiler_params=_params(1), name="inproj_even")(x2, g, w, *tabs)


def _win_attn_kernel(sink_ref, q_ref, kvp_ref, kvc_ref, kvn_ref, o_ref, *, n_blocks):
    n = pl.program_id(1)
    group = A_HEADS // A_KV_HEADS
    cols = group * BLOCK
    kj = lax.broadcasted_iota(I32, (BLOCK, cols), 0)
    qi = lax.broadcasted_iota(I32, (BLOCK, cols), 1) % BLOCK
    ok_prev = (kj >= qi) & (n > 0)
    ok_next = (kj <= qi) & (n < n_blocks - 1)
    q = q_ref[...]
    scale = A_HEAD_DIM ** -0.5
    for g in range(A_KV_HEADS):
        ks = slice(g * A_HEAD_DIM, (g + 1) * A_HEAD_DIM)
        vs = slice(A_KV_W + g * A_HEAD_DIM, A_KV_W + (g + 1) * A_HEAD_DIM)
        qg = jnp.concatenate([q[:, (g * group + i) * A_HEAD_DIM:(g * group + i + 1) * A_HEAD_DIM]
                              for i in range(group)], axis=0)
        sink = jnp.concatenate([jnp.full((1, BLOCK), sink_ref[g * group + i], F32)
                                for i in range(group)], axis=1)
        sp = jnp.where(ok_prev, _dot_nt(kvp_ref[:, ks], qg) * scale, NEG)
        sc = _dot_nt(kvc_ref[:, ks], qg) * scale
        sn = jnp.where(ok_next, _dot_nt(kvn_ref[:, ks], qg) * scale, NEG)
        m = jnp.maximum(jnp.maximum(jnp.max(sp, 0, keepdims=True), jnp.max(sc, 0, keepdims=True)),
                        jnp.maximum(jnp.max(sn, 0, keepdims=True), sink))
        pp, pc, pn = jnp.exp(sp - m), jnp.exp(sc - m), jnp.exp(sn - m)
        den = (jnp.sum(pp, 0, keepdims=True) + jnp.sum(pc, 0, keepdims=True)
               + jnp.sum(pn, 0, keepdims=True) + jnp.exp(sink - m))
        ot = (_dot_tn(kvp_ref[:, vs], pp.astype(BF16)) + _dot_tn(kvc_ref[:, vs], pc.astype(BF16))
              + _dot_tn(kvn_ref[:, vs], pn.astype(BF16))) / den
        o = ot.T
        for i in range(group):
            hh = g * group + i
            o_ref[:, hh * A_HEAD_DIM:(hh + 1) * A_HEAD_DIM] = o[i * BLOCK:(i + 1) * BLOCK].astype(BF16)


def _win_attn(aq, akv, sink, B, S):
    N = S // BLOCK
    T = B * S
    kv = lambda shift: pl.BlockSpec(
        (BLOCK, 2 * A_KV_W), lambda b, n: (b * N + jnp.clip(n + shift, 0, N - 1), 0))
    return pl.pallas_call(
        functools.partial(_win_attn_kernel, n_blocks=N),
        out_shape=jax.ShapeDtypeStruct((T, A_Q_W), BF16), grid=(B, N),
        in_specs=[pl.BlockSpec(memory_space=pltpu.SMEM),
                  pl.BlockSpec((BLOCK, A_Q_W), lambda b, n: (b * N + n, 0)), kv(-1), kv(0), kv(1)],
        out_specs=pl.BlockSpec((BLOCK, A_Q_W), lambda b, n: (b * N + n, 0)),
        compiler_params=_params(2), name="win_attn")(sink, aq, akv, akv, akv)


def _ret_tables(lg_ref, lgl_ref, d_ref, xif_ref, xib_ref, zf_ref, zb_ref):
    j = lax.broadcasted_iota(I32, (BLOCK, BLOCK), 0)
    i = lax.broadcasted_iota(I32, (BLOCK, BLOCK), 1)
    rel = (i - j).astype(F32)
    for h in range(B_HEADS):
        d_ref[h] = jnp.where(i >= j, jnp.exp(rel * lg_ref[0, h]), jnp.exp(-rel * lg_ref[1, h]))
    t = lax.broadcasted_iota(I32, (BLOCK, B_W), 0).astype(F32)
    lgf, lgb = lgl_ref[0:1, :], lgl_ref[1:2, :]
    xif_ref[...] = jnp.exp((t + 1.0) * lgf)
    zf_ref[...] = jnp.exp((BLOCK - 1.0 - t) * lgf)
    xib_ref[...] = jnp.exp((BLOCK - t) * lgb)
    zb_ref[...] = jnp.exp(t * lgb)


def _ret_bwd_state_kernel(lgl_ref, k_ref, v_ref, r_out_ref, r_ref):
    n = pl.program_id(1)

    @pl.when(n == 0)
    def _():
        r_ref[...] = jnp.zeros_like(r_ref)

    r_out_ref[...] = r_ref[...]
    lgb = lgl_ref[1:2, :]
    t = lax.broadcasted_iota(I32, (BLOCK, B_W), 0).astype(F32)
    kz = (k_ref[...] * jnp.exp(t * lgb)).astype(BF16)
    cd = jnp.exp(BLOCK * lgb)
    _ret_state_update(r_ref, cd, v_ref[...], kz)


def _ret_state_update(r_ref, chunk_decay, v, kz):
    upd = _dot_tn(v, kz)
    for h in range(B_HEADS):
        hs = slice(h * B_HEAD_DIM, (h + 1) * B_HEAD_DIM)
        r_ref[hs, :] = chunk_decay[:, hs] * r_ref[hs, :] + upd[hs, hs]


def _ret_bwd_state(lgl, bk, bv, B, S):
    N = S // BLOCK
    blk = lambda: pl.BlockSpec((BLOCK, B_W), lambda b, n: (b * N + N - 1 - n, 0))
    return pl.pallas_call(
        _ret_bwd_state_kernel,
        out_shape=jax.ShapeDtypeStruct((B, N, B_W, B_HEAD_DIM), F32), grid=(B, N),
        in_specs=[pl.BlockSpec(lgl.shape, lambda b, n: (0, 0)), blk(), blk()],
        out_specs=pl.BlockSpec((None, None, B_W, B_HEAD_DIM), lambda b, n: (b, N - 1 - n, 0, 0)),
        scratch_shapes=[pltpu.VMEM((B_W, B_HEAD_DIM), F32)],
        compiler_params=_params(2), name="ret_bwd_state")(lgl, bk, bv)


def _ret_main_kernel(lg_ref, lgl_ref, q_ref, k_ref, v_ref, g_ref, rb_ref, o_ref,
                     rf_ref, d_ref, xif_ref, xib_ref, zf_ref, zb_ref):
    b, n = pl.program_id(0), pl.program_id(1)

    @pl.when((b == 0) & (n == 0))
    def _():
        _ret_tables(lg_ref, lgl_ref, d_ref, xif_ref, xib_ref, zf_ref, zb_ref)

    @pl.when(n == 0)
    def _():
        rf_ref[...] = jnp.zeros_like(rf_ref)

    n_seq = q_ref.shape[0]
    cdf = jnp.exp(BLOCK * lgl_ref[0:1, :])
    q, k, v, gate = ([r[i] for i in range(n_seq)] for r in (q_ref, k_ref, v_ref, g_ref))
    qb = [x.astype(BF16) for x in q]
    kb = [x.astype(BF16) for x in k]
    qxf = [(x * xif_ref[...]).astype(BF16) for x in q]
    qxb = [(x * xib_ref[...]).astype(BF16) for x in q]
    kzf = [(x * zf_ref[...]).astype(BF16) for x in k]
    ynt = [[None] * B_HEADS for _ in range(n_seq)]
    for h in range(B_HEADS):
        hs = slice(h * B_HEAD_DIM, (h + 1) * B_HEAD_DIM)
        for i in range(n_seq):
            st = _dot_nt(kb[i][:, hs], qb[i][:, hs]) * d_ref[h]
            states = jnp.concatenate([rf_ref[i, hs, :], rb_ref[i, hs, :]], axis=1).astype(BF16)
            queries = jnp.concatenate([qxf[i][:, hs], qxb[i][:, hs]], axis=1)
            yt = _dot_tn(v[i][:, hs], st.astype(BF16)) + _dot_nt(states, queries)
            yc = yt - jnp.mean(yt, axis=0, keepdims=True)
            ynt[i][h] = yc * lax.rsqrt(jnp.mean(yc * yc, axis=0, keepdims=True) + GN_EPS)
    for i in range(n_seq):
        yn = jnp.concatenate(ynt[i], axis=0).T
        o_ref[i] = (gate[i] * jax.nn.sigmoid(gate[i]) * yn).astype(BF16)
        _ret_state_update(rf_ref.at[i], cdf, v[i], kzf[i])


SEQS_PER_STEP = 2


def _ret_main(lg, lgl, bq, bk, bv, bg, rb, B, S):
    N = S // BLOCK
    nb = SEQS_PER_STEP if B % SEQS_PER_STEP == 0 else 1
    per_seq = lambda a: a.reshape(B, S, a.shape[-1])
    blk = lambda: pl.BlockSpec((nb, BLOCK, B_W), lambda b, n: (b, n, 0))
    tab = lambda: pltpu.VMEM((BLOCK, B_W), F32)
    out = pl.pallas_call(
        _ret_main_kernel, out_shape=jax.ShapeDtypeStruct((B, S, B_W), BF16), grid=(B // nb, N),
        in_specs=[pl.BlockSpec(memory_space=pltpu.SMEM), pl.BlockSpec(lgl.shape, lambda b, n: (0, 0)),
                  blk(), blk(), blk(), blk(),
                  pl.BlockSpec((nb, None, B_W, B_HEAD_DIM), lambda b, n: (b, n, 0, 0))],
        out_specs=blk(),
        scratch_shapes=[pltpu.VMEM((nb, B_W, B_HEAD_DIM), F32), pltpu.VMEM((B_HEADS, BLOCK, BLOCK), F32),
                        tab(), tab(), tab(), tab()],
        compiler_params=_params(2), name="ret_main")(lg, lgl, per_seq(bq), per_seq(bk), per_seq(bv), per_seq(bg), rb)
    return out.reshape(B * S, B_W)


def even_mixer_parts(x2, positions, g_mix, w_in, sink, decay_logit, B, S, tm=512):
    ca, sa = _rope_tables(positions, ROPE_THETA, A_HEAD_DIM, 0, ROPE_DIM)
    cb, sb = _rope_tables(positions, RET_THETA, B_HEAD_DIM, 0, B_HEAD_DIM)
    aq, akv, bq, bk, bv, bg = _inproj_even(x2, g_mix.reshape(1, -1), w_in.astype(BF16), (ca, sa, cb, sb), tm)
    o_a = _win_attn(aq, akv, sink.astype(F32), B, S)
    lg = jax.nn.log_sigmoid(decay_logit.astype(F32))
    lgl = jnp.repeat(lg, B_HEAD_DIM, axis=1)
    rb = _ret_bwd_state(lgl, bk, bv, B, S)
    o_b = _ret_main(lg, lgl, bq, bk, bv, bg, rb, B, S)
    return o_a, o_b


def _outproj_kernel(o1_ref, o2_ref, w_ref, x_ref, g_ref, wr_ref, hx_ref, aff_ref):
    half = o1_ref.shape[1]
    D = x_ref.shape[1]
    x1 = x_ref[...] + (_dot(o1_ref[...], w_ref[:half, :]) + _dot(o2_ref[...], w_ref[half:, :]))
    hn = _rms(x1, g_ref[...])
    hx_ref[:, :D] = hn
    hx_ref[:, D:] = x1
    logits = _dot_nt(wr_ref[...], hn.astype(BF16))
    e = jnp.exp(logits - jnp.max(logits, axis=0, keepdims=True))
    aff_ref[...] = e / jnp.sum(e, axis=0, keepdims=True)


def _outproj_router(o1, o2, w_out, x2, g_ffn, w_router, B, S, tm):
    T, D = o1.shape[0], g_ffn.shape[-1]
    E = w_router.shape[1]
    spare = EC_CAPACITY_FACTOR * S // E
    per = S // tm
    row = lambda width: pl.BlockSpec((tm, width), lambda i: (i, 0))
    full = lambda a: pl.BlockSpec(a.shape, lambda i: (0, 0))
    w = w_out.astype(BF16)
    g = g_ffn.reshape(1, D)
    wr = w_router.T.astype(BF16)
    return pl.pallas_call(
        _outproj_kernel,
        out_shape=(jax.ShapeDtypeStruct((T + spare, 2 * D), F32), jax.ShapeDtypeStruct((B, E, S), F32)),
        grid=(T // tm,),
        in_specs=[row(o1.shape[1]), row(o2.shape[1]), full(w), _stream_spec(x2, D, tm), full(g), full(wr)],
        out_specs=(row(2 * D), pl.BlockSpec((None, E, tm), lambda i: (i // per, 0, i % per))),
        compiler_params=_params(1), name="outproj_router")(o1, o2, w, x2, g, wr)


def _split3(x):
    x1 = x.astype(BF16)
    r = x - x1.astype(F32)
    x2 = r.astype(BF16)
    x3 = (r - x2.astype(F32)).astype(BF16)
    return x1, x2, x3


def _topk_kernel(aff_ref, idx_ref, gate_ref, thr_ref, *, cap):
    E, R, _ = aff_ref.shape
    bits = lax.bitcast_convert_type(aff_ref[...], I32)

    def count(mask):
        return jnp.sum(jnp.sum(mask.astype(F32), axis=2, keepdims=True), axis=1, keepdims=True)

    def bit_body(i, prefix):
        cand = prefix | jnp.left_shift(jnp.int32(1), 30 - i)
        return jnp.where(count(bits >= cand) >= cap, cand, prefix)

    thr = lax.fori_loop(0, 31, bit_body, jnp.zeros((E, 1, 1), I32))
    thr_ref[...] = jnp.broadcast_to(thr, thr_ref.shape)

    li = lax.broadcasted_iota(I32, (LANES, LANES), 0)
    lj = lax.broadcasted_iota(I32, (LANES, LANES), 1)
    tri = (li <= lj).astype(BF16)
    ri = lax.broadcasted_iota(I32, (R, R), 0)
    rj = lax.broadcasted_iota(I32, (R, R), 1)
    below = (rj < ri).astype(BF16)
    slot = lax.broadcasted_iota(I32, (1, cap), 1).astype(F32)
    tok = (lax.broadcasted_iota(I32, (R, LANES), 0) * LANES
           + lax.broadcasted_iota(I32, (R, LANES), 1)).astype(F32)

    def prefix_counts(m):
        within = _dot(m.astype(BF16), tri)
        total = jnp.broadcast_to(within[:, LANES - 1:LANES], (R, LANES))
        before = _dot(below, total.astype(BF16))
        return within, total, before

    def expert(e, carry):
        a = aff_ref[e]
        b = lax.bitcast_convert_type(a, I32)
        t = thr_ref[e]
        gt, eq = b > t, b == t
        n_gt = jnp.sum(jnp.sum(gt.astype(F32), axis=1, keepdims=True), axis=0, keepdims=True)
        eqf = eq.astype(F32)
        within, _, before = prefix_counts(eqf)
        sel = gt | (eq & (before + within - eqf < cap - n_gt))
        self_ = sel.astype(F32)
        within, total, before = prefix_counts(self_)
        rank = before + within
        first, count_r = before[:, 0:1], total[:, 0:1]
        owner = ((first <= slot) & (slot < first + count_r)).astype(BF16)

        def row_of_slot(x):
            return sum(_dot_tn(p, owner) for p in _split3(x))

        hit = (row_of_slot(self_) > 0.5) & (row_of_slot(rank) == slot + 1.0)
        idx_ref[e] = jnp.sum(jnp.where(hit, row_of_slot(tok), 0.0), axis=0, keepdims=True).astype(I32)
        gate_ref[e] = jnp.sum(jnp.where(hit, row_of_slot(a), 0.0), axis=0, keepdims=True)
        return carry

    lax.fori_loop(0, E, expert, 0)


def _topk(aff, cap):
    B, E, S = aff.shape
    R = S // LANES
    aff4 = aff.reshape(B, E, R, LANES)
    out = jax.ShapeDtypeStruct((B, E, 1, cap), I32), jax.ShapeDtypeStruct((B, E, 1, cap), F32)
    spec = pl.BlockSpec((None, E, 1, cap), lambda b: (b, 0, 0, 0))
    return pl.pallas_call(
        functools.partial(_topk_kernel, cap=cap), out_shape=out, grid=(B,),
        in_specs=[pl.BlockSpec((None, E, R, LANES), lambda b: (b, 0, 0, 0))],
        out_specs=(spec, spec), scratch_shapes=[pltpu.VMEM((E, 1, LANES), I32)],
        compiler_params=_params(1), name="topk")(aff4)


ROW_UNROLL = 8
M_BLOCKS = 4
SEM_IN, SEM_OUT = 0, 1


N_SLOTS = 3


def _moe_kernel(idx_prev_ref, idx_ref, idx_next_ref, gate_ref, wg_ref, wu_ref, wd_ref, hx_alias, hx_hbm,
                buf, sems, *, seq, cap, f_chunk, n_tokens):
    del hx_alias
    D = wg_ref.shape[0]
    n_batch = pl.num_programs(1)
    step = pl.program_id(0) * n_batch + pl.program_id(1)
    n_steps = pl.num_programs(0) * n_batch
    cur, nxt, prv = step % N_SLOTS, (step + 1) % N_SLOTS, (step + 2) % N_SLOTS
    res = pl.ds(D, D)

    def for_rows(fn):
        @pl.loop(0, cap // ROW_UNROLL)
        def _(g):
            for u in range(ROW_UNROLL):
                fn(g * ROW_UNROLL + u)

    def gather_row(t, s, j):
        pltpu.make_async_copy(hx_hbm.at[pl.ds(t, 1)], buf.at[s, pl.ds(j, 1)], sems.at[SEM_IN, s]).start()

    def write_row(t, s, j):
        pltpu.make_async_copy(buf.at[s, pl.ds(j, 1), res], hx_hbm.at[pl.ds(t, 1), res], sems.at[SEM_OUT, s]).start()

    def wait_gathers(s):
        pltpu.make_async_copy(hx_hbm.at[pl.ds(0, cap)], buf.at[s], sems.at[SEM_IN, s]).wait()

    def wait_writes(s):
        pltpu.make_async_copy(buf.at[s, :, res], hx_hbm.at[pl.ds(0, cap), res], sems.at[SEM_OUT, s]).wait()

    this_base = pl.program_id(1) * seq
    next_base = ((step + 1) % n_batch) * seq
    prev_base = ((step + n_batch - 1) % n_batch) * seq
    first = step == 0

    @pl.when(first)
    def _():
        for_rows(lambda j: gather_row(this_base + idx_ref[0, j], cur, j))
        buf[prv, :, D:] = jnp.zeros((cap, D), F32)

    @pl.when(step > 0)
    def _():
        wait_writes(nxt)

    wait_gathers(cur)
    xin = buf[cur, :, :D].astype(BF16)
    n_chunks = wg_ref.shape[1] // f_chunk
    n_rows = cap // M_BLOCKS
    ahead = cap // (n_chunks * M_BLOCKS)
    accs = [jnp.zeros((n_rows, D), F32) for _ in range(M_BLOCKS)]
    piece = 0
    for f in range(n_chunks):
        fs = slice(f * f_chunk, (f + 1) * f_chunk)
        for mb in range(M_BLOCKS):
            for j in range(piece * ahead, (piece + 1) * ahead):
                gather_row(next_base + idx_next_ref[0, j], nxt, j)
                write_row(jnp.where(first, n_tokens + j, prev_base + idx_prev_ref[0, j]), prv, j)
            piece += 1
            xs = xin[mb * n_rows:(mb + 1) * n_rows]
            g = _dot(xs, wg_ref[:, fs])
            hid = (g * jax.nn.sigmoid(g) * _dot(xs, wu_ref[:, fs])).astype(BF16)
            accs[mb] = accs[mb] + _dot(hid, wd_ref[fs, :])
    acc = jnp.concatenate(accs, axis=0)
    diag = (lax.broadcasted_iota(I32, (cap, cap), 0) == lax.broadcasted_iota(I32, (cap, cap), 1))
    gate_col = jnp.sum(jnp.where(diag, gate_ref[...], 0.0), axis=1, keepdims=True)
    buf[cur, :, D:] = buf[cur, :, D:] + acc * gate_col

    @pl.when(step == n_steps - 1)
    def _():
        wait_gathers(nxt)
        wait_writes(prv)
        for_rows(lambda j: write_row(this_base + idx_ref[0, j], cur, j))
        wait_writes(cur)


def _moe_ffn(hx, idx, gate, w_gate, w_up, w_down, B, S, f_chunk=512):
    E, D, F = w_gate.shape
    cap = idx.shape[-1]
    T = hx.shape[0] - cap
    f_chunk = min(f_chunk, F)
    assert B >= N_SLOTS and cap % ROW_UNROLL == 0 and cap % (M_BLOCKS * (F // f_chunk)) == 0
    wspec = lambda a: pl.BlockSpec((None,) + a.shape[1:], lambda e, b: (e, 0, 0), pipeline_mode=pl.Buffered(1))
    any_spec = pl.BlockSpec(memory_space=pl.ANY)

    def ids(shift):
        def index_map(e, b):
            s = jnp.clip(e * B + b + shift, 0, E * B - 1)
            return (s % B, s // B, 0, 0)
        return pl.BlockSpec((None, None, 1, cap), index_map, memory_space=pltpu.SMEM)

    return pl.pallas_call(
        functools.partial(_moe_kernel, seq=S, cap=cap, f_chunk=f_chunk, n_tokens=T),
        out_shape=jax.ShapeDtypeStruct(hx.shape, F32), grid=(E, B),
        in_specs=[ids(-1), ids(0), ids(1),
                  pl.BlockSpec((None, None, 1, cap), lambda e, b: (b, e, 0, 0)),
                  wspec(w_gate), wspec(w_up), wspec(w_down), any_spec],
        out_specs=any_spec,
        scratch_shapes=[pltpu.VMEM((N_SLOTS, cap, 2 * D), F32), pltpu.SemaphoreType.DMA((2, N_SLOTS))],
        input_output_aliases={7: 0},
        compiler_params=_params(2), name="moe_ffn")(idx, idx, idx, gate, w_gate, w_up, w_down, hx)


def moe_layer(o1, o2, w_out, x2, g_ffn, w_router, w_gate, w_up, w_down, B, S, tm=512):
    hx, aff = _outproj_router(o1, o2, w_out, x2, g_ffn, w_router, B, S, tm)
    cap = EC_CAPACITY_FACTOR * S // w_router.shape[1]
    idx, gate = _topk(aff, cap)
    return _moe_ffn(hx, idx, gate, w_gate.astype(BF16), w_up.astype(BF16), w_down.astype(BF16), B, S)


def _final_norm_kernel(x_ref, g_ref, o_ref):
    o_ref[...] = _rms(x_ref[...], g_ref[...])


def _final_norm(x2, g, tm, T):
    D = g.shape[-1]
    row = pl.BlockSpec((tm, D), lambda i: (i, 0))
    return pl.pallas_call(
        _final_norm_kernel, out_shape=jax.ShapeDtypeStruct((T, D), F32), grid=(T // tm,),
        in_specs=[_stream_spec(x2, D, tm), pl.BlockSpec((1, D), lambda i: (0, 0))], out_specs=row,
        compiler_params=_params(1), name="final_norm")(x2, g.reshape(1, D))


ODD_SPLITS = (C_Q_LORA, C_KV_LORA, C_ROPE, D_W, D_W, D_W, D_W, 4 * D_HEADS)
N_GATES = 4 * D_HEADS
I_FWD, F_FWD, I_BWD, F_BWD = 0, D_HEADS, 2 * D_HEADS, 3 * D_HEADS


def _odd_weights(w_in):
    cq, ckv, kr, dq, dk, dv, do, dg = jnp.split(w_in, [int(c) for c in np.cumsum(ODD_SPLITS)[:-1]], axis=1)
    D = w_in.shape[0]
    zeros = lambda n: jnp.zeros((D, n), w_in.dtype)
    kr_slot = jnp.concatenate([zeros(C_NOPE), kr, zeros(C_SLOT - C_NOPE - C_ROPE)], axis=1)
    dg_slot = jnp.concatenate([dg, zeros(LANES - N_GATES)], axis=1)
    w = jnp.concatenate([cq, ckv, kr_slot, dq, dk, dv, do, dg_slot], axis=1)
    return w.astype(BF16), dg.T.astype(BF16)


def _inproj_odd_kernel(x_ref, g_ref, w_ref, wgt_ref,
                       cq_ref, ckv_ref, kr_ref, dqk_ref, dv_ref, do_ref, gc_ref, gt_ref):
    h = _rms(x_ref[...], g_ref[...]).astype(BF16)
    off = 0
    for ref, width in ((cq_ref, C_Q_LORA), (ckv_ref, C_KV_LORA), (kr_ref, C_SLOT), (dqk_ref, 2 * D_W),
                       (dv_ref, D_W), (do_ref, D_W), (gc_ref, LANES)):
        ref[...] = _dot(h, w_ref[:, off:off + width]).astype(ref.dtype)
        off += width
    gt_ref[...] = _dot_nt(wgt_ref[...], h)


def _inproj_odd(x2, g, w, wgt, tm, T):
    D = g.shape[-1]
    row = lambda width: pl.BlockSpec((tm, width), lambda i: (i, 0))
    full = lambda a: pl.BlockSpec(a.shape, lambda i: (0, 0))
    widths = (C_Q_LORA, C_KV_LORA, C_SLOT, 2 * D_W, D_W, D_W, LANES)
    dtypes = (F32, F32, F32, F32, BF16, F32, F32)
    out_shape = tuple(jax.ShapeDtypeStruct((T, wd), dt) for wd, dt in zip(widths, dtypes))
    out_shape += (jax.ShapeDtypeStruct((N_GATES, T), F32),)
    return pl.pallas_call(
        _inproj_odd_kernel, out_shape=out_shape, grid=(T // tm,),
        in_specs=[_stream_spec(x2, D, tm), full(g), full(w), full(wgt)],
        out_specs=tuple(row(wd) for wd in widths) + (pl.BlockSpec((N_GATES, tm), lambda i: (0, i)),),
        compiler_params=_params(1), name="inproj_odd")(x2, g, w, wgt)


def _mla_prep_kernel(cq_ref, ckv_ref, kr_ref, nq_ref, nkv_ref, wq_ref, wk_ref, wv_ref, cos_ref, sin_ref,
                     q_ref, k_ref, vt_ref):
    lane = lax.broadcasted_iota(I32, (1, LANES), 1)
    first = lane < C_NOPE + C_ROPE // 2
    cos, sin = cos_ref[...], sin_ref[...]
    rope = lambda z: _rope_slab(z, cos, sin, C_ROPE // 2, first)
    q = _dot(_rms(cq_ref[...], nq_ref[...]).astype(BF16), wq_ref[...])
    hkv = _rms(ckv_ref[...], nkv_ref[...]).astype(BF16)
    kn = _dot(hkv, wk_ref[...])
    kr = rope(kr_ref[...])
    for hh in range(C_HEADS):
        slab = slice(hh * C_SLOT, (hh + 1) * C_SLOT)
        q_ref[:, slab] = rope(q[:, slab]).astype(BF16)
        k_ref[:, slab] = (kn[:, slab] + kr).astype(BF16)
    vt = _dot_nt(wv_ref[...], hkv)
    row = lax.broadcasted_iota(I32, vt.shape, 0)
    vt_ref[...] = jnp.where(row % C_SLOT < C_V, vt, 1.0).astype(BF16)


def _mla_prep(cq, ckv, kr, nq, nkv, w_uq, w_ukv, cos, sin, tm):
    T = cq.shape[0]
    pad_q = C_SLOT - C_NOPE - C_ROPE
    wq = jnp.pad(w_uq.reshape(C_Q_LORA, C_HEADS, C_NOPE + C_ROPE), ((0, 0), (0, 0), (0, pad_q)))
    wq = wq.reshape(C_Q_LORA, C_HEADS * C_SLOT).astype(BF16)
    wkv = w_ukv.reshape(C_KV_LORA, C_HEADS, C_NOPE + C_V)
    wk = jnp.pad(wkv[:, :, :C_NOPE], ((0, 0), (0, 0), (0, C_SLOT - C_NOPE)))
    wk = wk.reshape(C_KV_LORA, C_HEADS * C_SLOT).astype(BF16)
    wv = jnp.pad(wkv[:, :, C_NOPE:], ((0, 0), (0, 0), (0, C_SLOT - C_V)))
    wv = wv.reshape(C_KV_LORA, C_HEADS * C_SLOT).T.astype(BF16)
    nq, nkv = nq.reshape(1, -1), nkv.reshape(1, -1)
    row = lambda width: pl.BlockSpec((tm, width), lambda i: (i, 0))
    full = lambda a: pl.BlockSpec(a.shape, lambda i: (0, 0))
    slots = jax.ShapeDtypeStruct((T, C_HEADS * C_SLOT), BF16)
    vt_shape = jax.ShapeDtypeStruct((T // tm, C_HEADS * C_SLOT, tm), BF16)
    return pl.pallas_call(
        _mla_prep_kernel, out_shape=(slots, slots, vt_shape), grid=(T // tm,),
        in_specs=[row(C_Q_LORA), row(C_KV_LORA), row(C_SLOT), full(nq), full(nkv), full(wq), full(wk), full(wv),
                  row(LANES), row(LANES)],
        out_specs=(row(C_HEADS * C_SLOT), row(C_HEADS * C_SLOT),
                   pl.BlockSpec((None, C_HEADS * C_SLOT, tm), lambda i: (i, 0, 0))),
        compiler_params=_params(1), name="mla_prep")(cq, ckv, kr, nq, nkv, wq, wk, wv, cos, sin)


C_PAIR = 2


def _mla_attn_kernel(q_ref, k_ref, vt_ref, o_ref):
    tq = q_ref.shape[0]
    n_chunks, _, key_chunk = vt_ref.shape
    mult = (C_NOPE + C_ROPE) ** -0.5 * np.log2(np.e)
    slabs = [slice(hh * C_SLOT, (hh + 1) * C_SLOT) for hh in range(C_PAIR)]
    qs = [q_ref[:, slab] for slab in slabs]

    def scores(hh, c):
        st = _dot_nt(k_ref[c * key_chunk:(c + 1) * key_chunk, slabs[hh]], qs[hh])
        return st, jnp.max(st, axis=0, keepdims=True)

    m = [jnp.full((1, tq), NEG, F32) for _ in range(C_PAIR)]
    acc = [jnp.zeros((C_SLOT, tq), F32) for _ in range(C_PAIR)]
    st = [scores(hh, 0) for hh in range(C_PAIR)]
    for c in range(n_chunks):
        for hh in range(C_PAIR):
            st_next = scores(hh, c + 1) if c + 1 < n_chunks else None
            m_new = jnp.maximum(m[hh], st[hh][1])
            pt = jnp.exp2(((st[hh][0] - m_new) * mult).astype(BF16))
            acc[hh] = jnp.exp2((m[hh] - m_new) * mult) * acc[hh] + _dot(vt_ref[c, slabs[hh], :], pt)
            m[hh], st[hh] = m_new, st_next
    for hh in range(C_PAIR):
        a = acc[hh].T
        o_ref[:, hh * C_V:(hh + 1) * C_V] = (a[:, :C_V] / a[:, C_V:C_V + 1]).astype(BF16)


def _mla_attn(q, k, vt, B, S, tq):
    T = B * S
    nq = S // tq
    key_chunk = vt.shape[-1]
    vt = vt.reshape(B, S // key_chunk, C_HEADS * C_SLOT, key_chunk)
    return pl.pallas_call(
        _mla_attn_kernel,
        out_shape=jax.ShapeDtypeStruct((T, C_HEADS * C_V), BF16), grid=(B, C_HEADS // C_PAIR, nq),
        in_specs=[pl.BlockSpec((tq, C_PAIR * C_SLOT), lambda b, p, i: (b * nq + i, p)),
                  pl.BlockSpec((S, C_PAIR * C_SLOT), lambda b, p, i: (b, p)),
                  pl.BlockSpec((None, S // key_chunk, C_PAIR * C_SLOT, key_chunk), lambda b, p, i: (b, 0, p, 0))],
        out_specs=pl.BlockSpec((tq, C_PAIR * C_V), lambda b, p, i: (b * nq + i, p)),
        compiler_params=_params(3), name="mla_attn")(q, k, vt)


HALO = 8


def _conv_kernel(xp_ref, x_ref, xn_ref, w_ref, o_ref, *, n_tiles):
    j = pl.program_id(1)
    tc = x_ref.shape[0]
    prev = jnp.where(j > 0, xp_ref[...], 0.0)
    nxt = jnp.where(j < n_tiles - 1, xn_ref[...], 0.0)
    ext = jnp.concatenate([prev, x_ref[...], nxt], axis=0)
    rows = tc + 2 * HALO
    y = jnp.zeros(x_ref.shape, F32)
    for w in range(D_CONV):
        first = HALO - D_CONV // 2 + w
        y = y + pltpu.roll(ext, (rows - first) % rows, 0)[:tc] * w_ref[w:w + 1, :]
    y = y * jax.nn.sigmoid(y)
    o_ref[:, :D_W] = y[:, :D_W]
    o_ref[:, D_W:] = y[:, D_W:] * (D_HEAD_DIM ** -0.5)


def _conv_prep(dqk, conv_w, B, S, tc):
    T, C = dqk.shape
    n_tiles = S // tc
    per, last = tc // HALO, T // HALO - 1
    cur = lambda b, j: (b * n_tiles + j, 0)
    return pl.pallas_call(
        functools.partial(_conv_kernel, n_tiles=n_tiles),
        out_shape=jax.ShapeDtypeStruct((T, C), F32), grid=(B, n_tiles),
        in_specs=[pl.BlockSpec((HALO, C), lambda b, j: (jnp.maximum((b * n_tiles + j) * per - 1, 0), 0)),
                  pl.BlockSpec((tc, C), cur),
                  pl.BlockSpec((HALO, C), lambda b, j: (jnp.minimum((b * n_tiles + j + 1) * per, last), 0)),
                  pl.BlockSpec(conv_w.shape, lambda b, j: (0, 0))],
        out_specs=pl.BlockSpec((tc, C), cur),
        compiler_params=_params(2), name="conv_prep")(dqk, dqk, dqk, conv_w)


def _log_sigmoid(x):
    return -(jnp.maximum(-x, 0.0) + jnp.log1p(jnp.exp(-jnp.abs(x))))


def _tri(lower):
    a = lax.broadcasted_iota(I32, (BLOCK, BLOCK), 0)
    b = lax.broadcasted_iota(I32, (BLOCK, BLOCK), 1)
    return ((b <= a) if lower else (b >= a)).astype(BF16)


def _mlstm_update(s_ref, m_ref, h, g, a, k, v_ones):
    a_max = jnp.max(a, axis=0, keepdims=True)
    w = jnp.exp(a - a_max)
    upd = _dot_tn(v_ones, (k * w).astype(BF16))
    m = m_ref[h:h + 1, 0:1]
    m_new = jnp.maximum(g + m, a_max)
    rows = slice(h * 2 * D_HEAD_DIM, (h + 1) * 2 * D_HEAD_DIM)
    s_ref[rows, :] = jnp.exp(g + m - m_new) * s_ref[rows, :] + jnp.exp(a_max - m_new) * upd
    m_ref[h:h + 1, :] = jnp.broadcast_to(m_new, (1, LANES))


def _mlstm_bwd_state_kernel(bias_c_ref, qk_ref, v_ref, gc_ref, s_out_ref, m_out_ref, s_ref, m_ref):
    @pl.when(pl.program_id(1) == 0)
    def _():
        s_ref[...] = jnp.zeros_like(s_ref)
        m_ref[...] = jnp.zeros_like(m_ref)

    s_out_ref[...] = s_ref[...]
    m_out_ref[...] = m_ref[...]
    gc = gc_ref[...] + bias_c_ref[...]
    suffix = sum(_dot(_tri(False), p) for p in _split3(_log_sigmoid(gc)))
    ones = jnp.ones((BLOCK, D_HEAD_DIM), BF16)
    for h in range(D_HEADS):
        hs = slice(h * D_HEAD_DIM, (h + 1) * D_HEAD_DIM)
        sb = suffix[:, F_BWD + h:F_BWD + h + 1]
        g = sb[0:1, :]
        a = g - sb + gc[:, I_BWD + h:I_BWD + h + 1]
        k = qk_ref[:, D_W + h * D_HEAD_DIM:D_W + (h + 1) * D_HEAD_DIM]
        _mlstm_update(s_ref, m_ref, h, g, a, k, jnp.concatenate([v_ref[:, hs], ones], axis=1))


def _mlstm_bwd_state(bias_c, qk, dv, gc, B, S):
    N = S // BLOCK
    rev = lambda b, n: (b * N + N - 1 - n, 0)
    out_shape = (jax.ShapeDtypeStruct((B, N, 2 * D_W, D_HEAD_DIM), F32), jax.ShapeDtypeStruct((B, N, 8, LANES), F32))
    return pl.pallas_call(
        _mlstm_bwd_state_kernel, out_shape=out_shape, grid=(B, N),
        in_specs=[pl.BlockSpec(bias_c.shape, lambda b, n: (0, 0)), pl.BlockSpec((BLOCK, 2 * D_W), rev),
                  pl.BlockSpec((BLOCK, D_W), rev), pl.BlockSpec((BLOCK, LANES), rev)],
        out_specs=(pl.BlockSpec((None, None, 2 * D_W, D_HEAD_DIM), lambda b, n: (b, N - 1 - n, 0, 0)),
                   pl.BlockSpec((None, None, 8, LANES), lambda b, n: (b, N - 1 - n, 0, 0))),
        scratch_shapes=[pltpu.VMEM((2 * D_W, D_HEAD_DIM), F32), pltpu.VMEM((8, LANES), F32)],
        compiler_params=_params(2), name="mlstm_bwd_state")(bias_c, qk, dv, gc)


def _mlstm_main_kernel(bias_c_ref, bias_r_ref, qk_ref, v_ref, og_ref, gc_ref, gt_ref, sb_ref, mb_ref,
                       o_ref, s_ref, m_ref):
    @pl.when(pl.program_id(1) == 0)
    def _():
        s_ref[...] = jnp.zeros_like(s_ref)
        m_ref[...] = jnp.zeros_like(m_ref)

    gc = gc_ref[...] + bias_c_ref[...]
    gr = gt_ref[...] + bias_r_ref[...]
    lower, upper = _tri(True), _tri(False)
    lfc, lfr = _split3(_log_sigmoid(gc)), _split3(_log_sigmoid(gr))
    pre_c = sum(_dot(lower, p) for p in lfc)
    suf_c = sum(_dot(upper, p) for p in lfc)
    pre_r = sum(_dot(p, upper) for p in lfr)
    suf_r = sum(_dot(p, lower) for p in lfr)
    src = lax.broadcasted_iota(I32, (BLOCK, BLOCK), 0)
    qry = lax.broadcasted_iota(I32, (BLOCK, BLOCK), 1)
    ones = jnp.ones((BLOCK, D_HEAD_DIM), BF16)
    og = og_ref[...]
    for h in range(D_HEADS):
        hs = slice(h * D_HEAD_DIM, (h + 1) * D_HEAD_DIM)
        rows = slice(h * 2 * D_HEAD_DIM, (h + 1) * 2 * D_HEAD_DIM)
        k = qk_ref[:, D_W + h * D_HEAD_DIM:D_W + (h + 1) * D_HEAD_DIM]
        qb, v = qk_ref[:, hs].astype(BF16), v_ref[:, hs]
        kq = _dot_nt(k.astype(BF16), qb)

        def direction(b_row, c_col, mask, state_t, m_prev):
            logd = jnp.where(mask, b_row + c_col, NEG)
            log_inter = b_row + m_prev
            m_t = jnp.maximum(jnp.max(logd, axis=0, keepdims=True), log_inter)
            s = kq * jnp.exp(logd - m_t)
            inter_w = jnp.exp(log_inter - m_t)
            qs = _dot_nt(state_t.astype(BF16), qb)
            num = _dot_tn(v, s.astype(BF16)) + inter_w * qs[:D_HEAD_DIM, :]
            den = jnp.sum(s, axis=0, keepdims=True) + inter_w * qs[D_HEAD_DIM:D_HEAD_DIM + 1, :]
            return num / jnp.maximum(jnp.abs(den), jnp.exp(-m_t))

        f, bk = F_FWD + h, F_BWD + h
        h_f = direction(pre_r[f:f + 1, :], gc[:, I_FWD + h:I_FWD + h + 1] - pre_c[:, f:f + 1], src <= qry,
                        s_ref[rows, :], m_ref[h:h + 1, 0:1])
        h_b = direction(suf_r[bk:bk + 1, :], gc[:, I_BWD + h:I_BWD + h + 1] - suf_c[:, bk:bk + 1], src > qry,
                        sb_ref[rows, :], mb_ref[h:h + 1, 0:1])
        o_ref[:, hs] = (jax.nn.sigmoid(og[:, hs]) * (h_f + h_b).T).astype(BF16)
        b_col = pre_c[:, f:f + 1]
        g = b_col[BLOCK - 1:BLOCK, :]
        a = g - b_col + gc[:, I_FWD + h:I_FWD + h + 1]
        _mlstm_update(s_ref, m_ref, h, g, a, k, jnp.concatenate([v, ones], axis=1))


def _mlstm_main(bias_c, bias_r, qk, dv, og, gc, gt, sb, mb, B, S):
    N = S // BLOCK
    T = B * S
    cur = lambda b, n: (b * N + n, 0)
    return pl.pallas_call(
        _mlstm_main_kernel, out_shape=jax.ShapeDtypeStruct((T, D_W), BF16), grid=(B, N),
        in_specs=[pl.BlockSpec(bias_c.shape, lambda b, n: (0, 0)), pl.BlockSpec(bias_r.shape, lambda b, n: (0, 0)),
                  pl.BlockSpec((BLOCK, 2 * D_W), cur), pl.BlockSpec((BLOCK, D_W), cur), pl.BlockSpec((BLOCK, D_W), cur),
                  pl.BlockSpec((BLOCK, LANES), cur), pl.BlockSpec((N_GATES, BLOCK), lambda b, n: (0, b * N + n)),
                  pl.BlockSpec((None, None, 2 * D_W, D_HEAD_DIM), lambda b, n: (b, n, 0, 0)),
                  pl.BlockSpec((None, None, 8, LANES), lambda b, n: (b, n, 0, 0))],
        out_specs=pl.BlockSpec((BLOCK, D_W), cur),
        scratch_shapes=[pltpu.VMEM((2 * D_W, D_HEAD_DIM), F32), pltpu.VMEM((8, LANES), F32)],
        compiler_params=_params(2), name="mlstm_main")(bias_c, bias_r, qk, dv, og, gc, gt, sb, mb)


def odd_mixer_parts(x2, positions, g_mix, w_in, norm_q, norm_kv, w_uq, w_ukv, conv_w, gate_bias, B, S,
                    tm=512, tq=256, key_chunk=1024):
    w, wgt = _odd_weights(w_in)
    cq, ckv, kr, dqk, dv, og, gc, gt = _inproj_odd(x2, g_mix.reshape(1, -1), w, wgt, tm, B * S)
    cos, sin = _rope_tables(positions, MLA_THETA, C_SLOT, C_NOPE, C_ROPE)
    q, k, vt = _mla_prep(cq, ckv, kr, norm_q, norm_kv, w_uq, w_ukv, cos, sin, min(key_chunk, S))
    o_c = _mla_attn(q, k, vt, B, S, min(tq, S))
    qk = _conv_prep(dqk, conv_w.astype(F32), B, S, min(tm, S))
    bias = gate_bias.astype(F32).reshape(1, N_GATES)
    bias_c = jnp.pad(bias, ((0, 0), (0, LANES - N_GATES)))
    bias_r = bias.reshape(N_GATES, 1)
    sb, mb = _mlstm_bwd_state(bias_c, qk, dv, gc, B, S)
    o_d = _mlstm_main(bias_c, bias_r, qk, dv, og, gc, gt, sb, mb, B, S)
    return o_c, o_d


def kernel(x, positions, norm_mix, norm_ffn, norm_final, ev_w_in, ev_w_out, attn_sink, ret_decay_logit,
           od_w_in, od_w_out, mla_norm_q, mla_norm_kv, mla_w_uq, mla_w_ukv, mlstm_conv, mlstm_gate_bias,
           moe_router, moe_w_gate, moe_w_up, moe_w_down):
    B, S, D = x.shape
    tm = min(512, S)
    x2 = x.reshape(B * S, D)
    for layer in range(norm_mix.shape[0]):
        j = layer // 2
        if layer % 2 == 0:
            o1, o2 = even_mixer_parts(x2, positions, norm_mix[layer], ev_w_in[j], attn_sink[j],
                                      ret_decay_logit[j], B, S, tm)
            w_out = ev_w_out[j]
        else:
            o1, o2 = odd_mixer_parts(x2, positions, norm_mix[layer], od_w_in[j], mla_norm_q[j], mla_norm_kv[j],
                                     mla_w_uq[j], mla_w_ukv[j], mlstm_conv[j], mlstm_gate_bias[j], B, S, tm)
            w_out = od_w_out[j]
        x2 = moe_layer(o1, o2, w_out, x2, norm_ffn[layer], moe_router[layer], moe_w_gate[layer],
                       moe_w_up[layer], moe_w_down[layer], B, S, tm)
    return _final_norm(x2, norm_final, tm, B * S).reshape(B, S, D)
```

```python
import functools

import jax
import jax.numpy as jnp
import numpy as np
from jax import lax
from jax.experimental import pallas as pl
from jax.experimental.pallas import tpu as pltpu

F32 = jnp.float32
BF16 = jnp.bfloat16
I32 = jnp.int32

LANES = 128
BLOCK = 128
RMS_EPS = 1e-6
GN_EPS = 1e-5
NEG = -1e30
VMEM_LIMIT = 56 * 1024 * 1024

A_HEADS, A_KV_HEADS, A_HEAD_DIM = 8, 2, 64
ROPE_THETA = 500000.0
ROPE_DIM = A_HEAD_DIM // 4
B_HEADS, B_HEAD_DIM = 8, 64
RET_THETA = 10000.0
C_HEADS, C_NOPE, C_ROPE, C_V = 8, 64, 32, 64
C_Q_LORA, C_KV_LORA = 512, 256
MLA_THETA = 10000.0
D_HEADS, D_HEAD_DIM, D_CONV = 4, 128, 5
N_EXPERTS = 16
EC_CAPACITY_FACTOR = 2

A_Q_W = A_HEADS * A_HEAD_DIM
A_KV_W = A_KV_HEADS * A_HEAD_DIM
B_W = B_HEADS * B_HEAD_DIM
D_W = D_HEADS * D_HEAD_DIM
C_SLOT = 128
C_PAIR = 2
BF16_ROWS = 16
C_VROWS = C_V + BF16_ROWS
C_LOG2_SCALE = float((C_NOPE + C_ROPE) ** -0.5 * np.log2(np.e))


def _params(n_axes, vmem=VMEM_LIMIT):
    return pltpu.CompilerParams(dimension_semantics=("arbitrary",) * n_axes,
                                vmem_limit_bytes=vmem)


def _rms(x, g):
    return x * lax.rsqrt(jnp.mean(x * x, axis=-1, keepdims=True) + RMS_EPS) * g


def _dot(a, b):
    return jnp.dot(a, b, preferred_element_type=F32)


def _dot_nt(a, b):
    return lax.dot_general(a, b, (((1,), (1,)), ((), ())), preferred_element_type=F32)


def _dot_tn(a, b):
    return lax.dot_general(a, b, (((0,), (0,)), ((), ())), preferred_element_type=F32)


def _rope_slab(z, cos, sin, half, first_half):
    partner = jnp.where(first_half, pltpu.roll(z, LANES - half, 1), pltpu.roll(z, half, 1))
    return z * cos + partner * sin


def _rope_tables(positions, theta, head_dim, rot_start, rot_dim):
    half = rot_dim // 2
    assert head_dim % half == 0 and rot_start % half == 0
    inv_freq = theta ** (-jnp.arange(half, dtype=F32) * 2.0 / rot_dim)
    d = np.arange(LANES) % head_dim - rot_start
    rot = (d >= 0) & (d < rot_dim)
    sign = np.where(d < half, -1.0, 1.0).astype(np.float32)
    ang = positions.astype(F32).reshape(-1, 1) * inv_freq
    ang = jnp.tile(ang, (1, LANES // half))
    cos = jnp.where(rot[None, :], jnp.cos(ang), 1.0)
    sin = jnp.where(rot[None, :], jnp.sin(ang) * sign[None, :], 0.0)
    return cos, sin


def _inproj_even_kernel(x_ref, g_ref, w_ref, ca_ref, sa_ref, cb_ref, sb_ref,
                        aq_ref, akv_ref, bq_ref, bk_ref, bv_ref, bg_ref):
    h = _rms(x_ref[...], g_ref[...]).astype(BF16)
    lane = lax.broadcasted_iota(I32, (1, LANES), 1)
    first_a = (lane % A_HEAD_DIM) < (ROPE_DIM // 2)
    first_b = (lane % B_HEAD_DIM) < (B_HEAD_DIM // 2)
    ca, sa, cb, sb = ca_ref[...], sa_ref[...], cb_ref[...], sb_ref[...]

    def rope_a(z):
        return _rope_slab(z, ca, sa, ROPE_DIM // 2, first_a)

    def rope_b(z):
        return _rope_slab(z, cb, sb, B_HEAD_DIM // 2, first_b)

    off = 0
    z = _dot(h, w_ref[:, off:off + A_Q_W])
    for s in range(A_Q_W // LANES):
        aq_ref[:, s * LANES:(s + 1) * LANES] = rope_a(z[:, s * LANES:(s + 1) * LANES]).astype(BF16)
    off += A_Q_W
    z = _dot(h, w_ref[:, off:off + 2 * A_KV_W])
    akv_ref[:, :A_KV_W] = rope_a(z[:, :A_KV_W]).astype(BF16)
    akv_ref[:, A_KV_W:] = z[:, A_KV_W:].astype(BF16)
    off += 2 * A_KV_W
    z = _dot(h, w_ref[:, off:off + B_W])
    for s in range(B_W // LANES):
        bq_ref[:, s * LANES:(s + 1) * LANES] = rope_b(z[:, s * LANES:(s + 1) * LANES])
    off += B_W
    z = _dot(h, w_ref[:, off:off + B_W])
    for s in range(B_W // LANES):
        bk_ref[:, s * LANES:(s + 1) * LANES] = rope_b(z[:, s * LANES:(s + 1) * LANES]) * (B_HEAD_DIM ** -0.5)
    off += B_W
    bv_ref[...] = _dot(h, w_ref[:, off:off + B_W]).astype(BF16)
    off += B_W
    bg_ref[...] = _dot(h, w_ref[:, off:off + B_W])


def _stream_spec(xs, D, tm):
    col = xs.shape[1] // D - 1
    return pl.BlockSpec((tm, D), lambda i: (i, col))


def _inproj_even(x2, g, w, tabs, tm):
    T, D = tabs[0].shape[0], g.shape[-1]
    ncol = w.shape[1]
    row = lambda width: pl.BlockSpec((tm, width), lambda i: (i, 0))
    full = lambda a: pl.BlockSpec(a.shape, lambda i: (0, 0))
    out_shape = (jax.ShapeDtypeStruct((T, A_Q_W), BF16), jax.ShapeDtypeStruct((T, 2 * A_KV_W), BF16),
                 jax.ShapeDtypeStruct((T, B_W), F32), jax.ShapeDtypeStruct((T, B_W), F32),
                 jax.ShapeDtypeStruct((T, B_W), BF16), jax.ShapeDtypeStruct((T, B_W), F32))
    return pl.pallas_call(
        _inproj_even_kernel, out_shape=out_shape, grid=(T // tm,),
        in_specs=[_stream_spec(x2, D, tm), full(g), pl.BlockSpec((D, ncol), lambda i: (0, 0))] + [row(LANES)] * 4,
        out_specs=(row(A_Q_W), row(2 * A_KV_W), row(B_W), row(B_W), row(B_W), row(B_W)),
        compiler_params=_params(1), name="inproj_even")(x2, g, w, *tabs)


def _win_attn_kernel(sink_ref, q_ref, kvp_ref, kvc_ref, kvn_ref, o_ref, *, n_blocks):
    n = pl.program_id(1)
    group = A_HEADS // A_KV_HEADS
    cols = group * BLOCK
    kj = lax.broadcasted_iota(I32, (BLOCK, cols), 0)
    qi = lax.broadcasted_iota(I32, (BLOCK, cols), 1) % BLOCK
    ok_prev = (kj >= qi) & (n > 0)
    ok_next = (kj <= qi) & (n < n_blocks - 1)
    q = q_ref[...]
    scale = A_HEAD_DIM ** -0.5
    for g in range(A_KV_HEADS):
        ks = slice(g * A_HEAD_DIM, (g + 1) * A_HEAD_DIM)
        vs = slice(A_KV_W + g * A_HEAD_DIM, A_KV_W + (g + 1) * A_HEAD_DIM)
        qg = jnp.concatenate([q[:, (g * group + i) * A_HEAD_DIM:(g * group + i + 1) * A_HEAD_DIM]
                              for i in range(group)], axis=0)
        sink = jnp.concatenate([jnp.full((1, BLOCK), sink_ref[g * group + i], F32)
                                for i in range(group)], axis=1)
        sp = jnp.where(ok_prev, _dot_nt(kvp_ref[:, ks], qg) * scale, NEG)
        sc = _dot_nt(kvc_ref[:, ks], qg) * scale
        sn = jnp.where(ok_next, _dot_nt(kvn_ref[:, ks], qg) * scale, NEG)
        m = jnp.maximum(jnp.maximum(jnp.max(sp, 0, keepdims=True), jnp.max(sc, 0, keepdims=True)),
                        jnp.maximum(jnp.max(sn, 0, keepdims=True), sink))
        pp, pc, pn = jnp.exp(sp - m), jnp.exp(sc - m), jnp.exp(sn - m)
        den = (jnp.sum(pp, 0, keepdims=True) + jnp.sum(pc, 0, keepdims=True)
               + jnp.sum(pn, 0, keepdims=True) + jnp.exp(sink - m))
        ot = (_dot_tn(kvp_ref[:, vs], pp.astype(BF16)) + _dot_tn(kvc_ref[:, vs], pc.astype(BF16))
              + _dot_tn(kvn_ref[:, vs], pn.astype(BF16))) / den
        o = ot.T
        for i in range(group):
            hh = g * group + i
            o_ref[:, hh * A_HEAD_DIM:(hh + 1) * A_HEAD_DIM] = o[i * BLOCK:(i + 1) * BLOCK].astype(BF16)


def _win_attn(aq, akv, sink, B, S):
    N = S // BLOCK
    T = B * S
    kv = lambda shift: pl.BlockSpec(
        (BLOCK, 2 * A_KV_W), lambda b, n: (b * N + jnp.clip(n + shift, 0, N - 1), 0))
    return pl.pallas_call(
        functools.partial(_win_attn_kernel, n_blocks=N),
        out_shape=jax.ShapeDtypeStruct((T, A_Q_W), BF16), grid=(B, N),
        in_specs=[pl.BlockSpec(memory_space=pltpu.SMEM),
                  pl.BlockSpec((BLOCK, A_Q_W), lambda b, n: (b * N + n, 0)), kv(-1), kv(0), kv(1)],
        out_specs=pl.BlockSpec((BLOCK, A_Q_W), lambda b, n: (b * N + n, 0)),
        compiler_params=_params(2), name="win_attn")(sink, aq, akv, akv, akv)


def _ret_tables(lg_ref, lgl_ref, d_ref, xif_ref, xib_ref, zf_ref, zb_ref):
    j = lax.broadcasted_iota(I32, (BLOCK, BLOCK), 0)
    i = lax.broadcasted_iota(I32, (BLOCK, BLOCK), 1)
    rel = (i - j).astype(F32)
    for h in range(B_HEADS):
        d_ref[h] = jnp.where(i >= j, jnp.exp(rel * lg_ref[0, h]), jnp.exp(-rel * lg_ref[1, h]))
    t = lax.broadcasted_iota(I32, (BLOCK, B_W), 0).astype(F32)
    lgf, lgb = lgl_ref[0:1, :], lgl_ref[1:2, :]
    xif_ref[...] = jnp.exp((t + 1.0) * lgf)
    zf_ref[...] = jnp.exp((BLOCK - 1.0 - t) * lgf)
    xib_ref[...] = jnp.exp((BLOCK - t) * lgb)
    zb_ref[...] = jnp.exp(t * lgb)


def _ret_bwd_state_kernel(lgl_ref, k_ref, v_ref, r_out_ref, r_ref):
    n = pl.program_id(1)

    @pl.when(n == 0)
    def _():
        r_ref[...] = jnp.zeros_like(r_ref)

    r_out_ref[...] = r_ref[...]
    lgb = lgl_ref[1:2, :]
    t = lax.broadcasted_iota(I32, (BLOCK, B_W), 0).astype(F32)
    kz = (k_ref[...] * jnp.exp(t * lgb)).astype(BF16)
    cd = jnp.exp(BLOCK * lgb)
    _ret_state_update(r_ref, cd, v_ref[...], kz)


def _ret_state_update(r_ref, chunk_decay, v, kz):
    upd = _dot_tn(v, kz)
    for h in range(B_HEADS):
        hs = slice(h * B_HEAD_DIM, (h + 1) * B_HEAD_DIM)
        r_ref[hs, :] = chunk_decay[:, hs] * r_ref[hs, :] + upd[hs, hs]


def _ret_bwd_state(lgl, bk, bv, B, S):
    N = S // BLOCK
    blk = lambda: pl.BlockSpec((BLOCK, B_W), lambda b, n: (b * N + N - 1 - n, 0))
    return pl.pallas_call(
        _ret_bwd_state_kernel,
        out_shape=jax.ShapeDtypeStruct((B, N, B_W, B_HEAD_DIM), F32), grid=(B, N),
        in_specs=[pl.BlockSpec(lgl.shape, lambda b, n: (0, 0)), blk(), blk()],
        out_specs=pl.BlockSpec((None, None, B_W, B_HEAD_DIM), lambda b, n: (b, N - 1 - n, 0, 0)),
        scratch_shapes=[pltpu.VMEM((B_W, B_HEAD_DIM), F32)],
        compiler_params=_params(2), name="ret_bwd_state")(lgl, bk, bv)


def _ret_main_kernel(lg_ref, lgl_ref, q_ref, k_ref, v_ref, g_ref, rb_ref, o_ref,
                     rf_ref, d_ref, xif_ref, xib_ref, zf_ref, zb_ref):
    b, n = pl.program_id(0), pl.program_id(1)

    @pl.when((b == 0) & (n == 0))
    def _():
        _ret_tables(lg_ref, lgl_ref, d_ref, xif_ref, xib_ref, zf_ref, zb_ref)

    @pl.when(n == 0)
    def _():
        rf_ref[...] = jnp.zeros_like(rf_ref)

    n_seq = q_ref.shape[0]
    cdf = jnp.exp(BLOCK * lgl_ref[0:1, :])
    q, k, v, gate = ([r[i] for i in range(n_seq)] for r in (q_ref, k_ref, v_ref, g_ref))
    qb = [x.astype(BF16) for x in q]
    kb = [x.astype(BF16) for x in k]
    qxf = [(x * xif_ref[...]).astype(BF16) for x in q]
    qxb = [(x * xib_ref[...]).astype(BF16) for x in q]
    kzf = [(x * zf_ref[...]).astype(BF16) for x in k]
    ynt = [[None] * B_HEADS for _ in range(n_seq)]
    for h in range(B_HEADS):
        hs = slice(h * B_HEAD_DIM, (h + 1) * B_HEAD_DIM)
        for i in range(n_seq):
            st = _dot_nt(kb[i][:, hs], qb[i][:, hs]) * d_ref[h]
            states = jnp.concatenate([rf_ref[i, hs, :], rb_ref[i, hs, :]], axis=1).astype(BF16)
            queries = jnp.concatenate([qxf[i][:, hs], qxb[i][:, hs]], axis=1)
            yt = _dot_tn(v[i][:, hs], st.astype(BF16)) + _dot_nt(states, queries)
            yc = yt - jnp.mean(yt, axis=0, keepdims=True)
            ynt[i][h] = yc * lax.rsqrt(jnp.mean(yc * yc, axis=0, keepdims=True) + GN_EPS)
    for i in range(n_seq):
        yn = jnp.concatenate(ynt[i], axis=0).T
        o_ref[i] = (gate[i] * jax.nn.sigmoid(gate[i]) * yn).astype(BF16)
        _ret_state_update(rf_ref.at[i], cdf, v[i], kzf[i])


SEQS_PER_STEP = 2


def _ret_main(lg, lgl, bq, bk, bv, bg, rb, B, S):
    N = S // BLOCK
    nb = SEQS_PER_STEP if B % SEQS_PER_STEP == 0 else 1
    per_seq = lambda a: a.reshape(B, S, a.shape[-1])
    blk = lambda: pl.BlockSpec((nb, BLOCK, B_W), lambda b, n: (b, n, 0))
    tab = lambda: pltpu.VMEM((BLOCK, B_W), F32)
    out = pl.pallas_call(
        _ret_main_kernel, out_shape=jax.ShapeDtypeStruct((B, S, B_W), BF16), grid=(B // nb, N),
        in_specs=[pl.BlockSpec(memory_space=pltpu.SMEM), pl.BlockSpec(lgl.shape, lambda b, n: (0, 0)),
                  blk(), blk(), blk(), blk(),
                  pl.BlockSpec((nb, None, B_W, B_HEAD_DIM), lambda b, n: (b, n, 0, 0))],
        out_specs=blk(),
        scratch_shapes=[pltpu.VMEM((nb, B_W, B_HEAD_DIM), F32), pltpu.VMEM((B_HEADS, BLOCK, BLOCK), F32),
                        tab(), tab(), tab(), tab()],
        compiler_params=_params(2), name="ret_main")(lg, lgl, per_seq(bq), per_seq(bk), per_seq(bv), per_seq(bg), rb)
    return out.reshape(B * S, B_W)


def even_mixer_parts(x2, positions, g_mix, w_in, sink, decay_logit, B, S, tm=512):
    ca, sa = _rope_tables(positions, ROPE_THETA, A_HEAD_DIM, 0, ROPE_DIM)
    cb, sb = _rope_tables(positions, RET_THETA, B_HEAD_DIM, 0, B_HEAD_DIM)
    aq, akv, bq, bk, bv, bg = _inproj_even(x2, g_mix.reshape(1, -1), w_in.astype(BF16), (ca, sa, cb, sb), tm)
    o_a = _win_attn(aq, akv, sink.astype(F32), B, S)
    lg = jax.nn.log_sigmoid(decay_logit.astype(F32))
    lgl = jnp.repeat(lg, B_HEAD_DIM, axis=1)
    rb = _ret_bwd_state(lgl, bk, bv, B, S)
    o_b = _ret_main(lg, lgl, bq, bk, bv, bg, rb, B, S)
    return o_a, o_b


def _outproj_kernel(o1_ref, o2_ref, w_ref, x_ref, g_ref, wr_ref, hx_ref, aff_ref):
    half = o1_ref.shape[1]
    D = x_ref.shape[1]
    x1 = x_ref[...] + (_dot(o1_ref[...], w_ref[:half, :]) + _dot(o2_ref[...], w_ref[half:, :]))
    hn = _rms(x1, g_ref[...])
    hx_ref[:, :D] = hn
    hx_ref[:, D:] = x1
    logits = _dot_nt(wr_ref[...], hn.astype(BF16))
    e = jnp.exp(logits - jnp.max(logits, axis=0, keepdims=True))
    aff_ref[...] = e / jnp.sum(e, axis=0, keepdims=True)


def _outproj_router(o1, o2, w_out, x2, g_ffn, w_router, B, S, tm):
    T, D = o1.shape[0], g_ffn.shape[-1]
    E = w_router.shape[1]
    spare = EC_CAPACITY_FACTOR * S // E
    per = S // tm
    row = lambda width: pl.BlockSpec((tm, width), lambda i: (i, 0))
    full = lambda a: pl.BlockSpec(a.shape, lambda i: (0, 0))
    w = w_out.astype(BF16)
    g = g_ffn.reshape(1, D)
    wr = w_router.T.astype(BF16)
    return pl.pallas_call(
        _outproj_kernel,
        out_shape=(jax.ShapeDtypeStruct((T + spare, 2 * D), F32), jax.ShapeDtypeStruct((B, E, S), F32)),
        grid=(T // tm,),
        in_specs=[row(o1.shape[1]), row(o2.shape[1]), full(w), _stream_spec(x2, D, tm), full(g), full(wr)],
        out_specs=(row(2 * D), pl.BlockSpec((None, E, tm), lambda i: (i // per, 0, i % per))),
        compiler_params=_params(1), name="outproj_router")(o1, o2, w, x2, g, wr)


def _split3(x):
    x1 = x.astype(BF16)
    r = x - x1.astype(F32)
    x2 = r.astype(BF16)
    x3 = (r - x2.astype(F32)).astype(BF16)
    return x1, x2, x3


def _topk_kernel(aff_ref, idx_ref, gate_ref, thr_ref, *, cap):
    E, R, _ = aff_ref.shape
    bits = lax.bitcast_convert_type(aff_ref[...], I32)

    def count(mask):
        return jnp.sum(jnp.sum(mask.astype(F32), axis=2, keepdims=True), axis=1, keepdims=True)

    def bit_body(i, prefix):
        cand = prefix | jnp.left_shift(jnp.int32(1), 30 - i)
        return jnp.where(count(bits >= cand) >= cap, cand, prefix)

    thr = lax.fori_loop(0, 31, bit_body, jnp.zeros((E, 1, 1), I32))
    thr_ref[...] = jnp.broadcast_to(thr, thr_ref.shape)

    li = lax.broadcasted_iota(I32, (LANES, LANES), 0)
    lj = lax.broadcasted_iota(I32, (LANES, LANES), 1)
    tri = (li <= lj).astype(BF16)
    ri = lax.broadcasted_iota(I32, (R, R), 0)
    rj = lax.broadcasted_iota(I32, (R, R), 1)
    below = (rj < ri).astype(BF16)
    slot = lax.broadcasted_iota(I32, (1, cap), 1).astype(F32)
    tok = (lax.broadcasted_iota(I32, (R, LANES), 0) * LANES
           + lax.broadcasted_iota(I32, (R, LANES), 1)).astype(F32)

    def prefix_counts(m):
        within = _dot(m.astype(BF16), tri)
        total = jnp.broadcast_to(within[:, LANES - 1:LANES], (R, LANES))
        before = _dot(below, total.astype(BF16))
        return within, total, before

    def expert(e, carry):
        a = aff_ref[e]
        b = lax.bitcast_convert_type(a, I32)
        t = thr_ref[e]
        gt, eq = b > t, b == t
        n_gt = jnp.sum(jnp.sum(gt.astype(F32), axis=1, keepdims=True), axis=0, keepdims=True)
        eqf = eq.astype(F32)
        within, _, before = prefix_counts(eqf)
        sel = gt | (eq & (before + within - eqf < cap - n_gt))
        self_ = sel.astype(F32)
        within, total, before = prefix_counts(self_)
        rank = before + within
        first, count_r = before[:, 0:1], total[:, 0:1]
        owner = ((first <= slot) & (slot < first + count_r)).astype(BF16)

        def row_of_slot(x):
            return sum(_dot_tn(p, owner) for p in _split3(x))

        hit = (row_of_slot(self_) > 0.5) & (row_of_slot(rank) == slot + 1.0)
        idx_ref[e] = jnp.sum(jnp.where(hit, row_of_slot(tok), 0.0), axis=0, keepdims=True).astype(I32)
        gate_ref[e] = jnp.sum(jnp.where(hit, row_of_slot(a), 0.0), axis=0, keepdims=True)
        return carry

    lax.fori_loop(0, E, expert, 0)


def _topk(aff, cap):
    B, E, S = aff.shape
    R = S // LANES
    aff4 = aff.reshape(B, E, R, LANES)
    out = jax.ShapeDtypeStruct((B, E, 1, cap), I32), jax.ShapeDtypeStruct((B, E, 1, cap), F32)
    spec = pl.BlockSpec((None, E, 1, cap), lambda b: (b, 0, 0, 0))
    return pl.pallas_call(
        functools.partial(_topk_kernel, cap=cap), out_shape=out, grid=(B,),
        in_specs=[pl.BlockSpec((None, E, R, LANES), lambda b: (b, 0, 0, 0))],
        out_specs=(spec, spec), scratch_shapes=[pltpu.VMEM((E, 1, LANES), I32)],
        compiler_params=_params(1), name="topk")(aff4)


ROW_UNROLL = 8
M_BLOCKS = 2
SEM_IN, SEM_OUT = 0, 1


N_SLOTS = 3


def _moe_kernel(idx_prev_ref, idx_ref, idx_next_ref, gate_ref, wg_ref, wu_ref, wd_ref, hx_alias, hx_hbm,
                buf, sems, *, seq, cap, f_chunk, n_tokens):
    del hx_alias
    D = wg_ref.shape[0]
    n_batch = pl.num_programs(1)
    step = pl.program_id(0) * n_batch + pl.program_id(1)
    n_steps = pl.num_programs(0) * n_batch
    cur, nxt, prv = step % N_SLOTS, (step + 1) % N_SLOTS, (step + 2) % N_SLOTS
    res = pl.ds(D, D)

    def for_rows(fn):
        @pl.loop(0, cap // ROW_UNROLL)
        def _(g):
            for u in range(ROW_UNROLL):
                fn(g * ROW_UNROLL + u)

    def gather_row(t, s, j):
        pltpu.make_async_copy(hx_hbm.at[pl.ds(t, 1)], buf.at[s, pl.ds(j, 1)], sems.at[SEM_IN, s]).start()

    def write_row(t, s, j):
        pltpu.make_async_copy(buf.at[s, pl.ds(j, 1), res], hx_hbm.at[pl.ds(t, 1), res], sems.at[SEM_OUT, s]).start()

    def wait_gathers(s):
        pltpu.make_async_copy(hx_hbm.at[pl.ds(0, cap)], buf.at[s], sems.at[SEM_IN, s]).wait()

    def wait_writes(s):
        pltpu.make_async_copy(buf.at[s, :, res], hx_hbm.at[pl.ds(0, cap), res], sems.at[SEM_OUT, s]).wait()

    this_base = pl.program_id(1) * seq
    next_base = ((step + 1) % n_batch) * seq
    prev_base = ((step + n_batch - 1) % n_batch) * seq
    first = step == 0

    @pl.when(first)
    def _():
        for_rows(lambda j: gather_row(this_base + idx_ref[0, j], cur, j))
        buf[prv, :, D:] = jnp.zeros((cap, D), F32)

    @pl.when(step > 0)
    def _():
        wait_writes(nxt)

    wait_gathers(cur)
    xin = buf[cur, :, :D].astype(BF16)
    n_chunks = wg_ref.shape[1] // f_chunk
    n_rows = cap // M_BLOCKS
    ahead = cap // (n_chunks * M_BLOCKS)
    accs = [jnp.zeros((n_rows, D), F32) for _ in range(M_BLOCKS)]
    piece = 0
    for f in range(n_chunks):
        fs = slice(f * f_chunk, (f + 1) * f_chunk)
        for mb in range(M_BLOCKS):
            for j in range(piece * ahead, (piece + 1) * ahead):
                gather_row(next_base + idx_next_ref[0, j], nxt, j)
                write_row(jnp.where(first, n_tokens + j, prev_base + idx_prev_ref[0, j]), prv, j)
            piece += 1
            xs = xin[mb * n_rows:(mb + 1) * n_rows]
            g = _dot(xs, wg_ref[:, fs])
            hid = (g * jax.nn.sigmoid(g) * _dot(xs, wu_ref[:, fs])).astype(BF16)
            accs[mb] = accs[mb] + _dot(hid, wd_ref[fs, :])
    acc = jnp.concatenate(accs, axis=0)
    diag = (lax.broadcasted_iota(I32, (cap, cap), 0) == lax.broadcasted_iota(I32, (cap, cap), 1))
    gate_col = jnp.sum(jnp.where(diag, gate_ref[...], 0.0), axis=1, keepdims=True)
    buf[cur, :, D:] = buf[cur, :, D:] + acc * gate_col

    @pl.when(step == n_steps - 1)
    def _():
        wait_gathers(nxt)
        wait_writes(prv)
        for_rows(lambda j: write_row(this_base + idx_ref[0, j], cur, j))
        wait_writes(cur)


def _moe_ffn(hx, idx, gate, w_gate, w_up, w_down, B, S, f_chunk=512):
    E, D, F = w_gate.shape
    cap = idx.shape[-1]
    T = hx.shape[0] - cap
    f_chunk = min(f_chunk, F)
    assert B >= N_SLOTS and cap % ROW_UNROLL == 0 and cap % (M_BLOCKS * (F // f_chunk)) == 0
    wspec = lambda a: pl.BlockSpec((None,) + a.shape[1:], lambda e, b: (e, 0, 0), pipeline_mode=pl.Buffered(1))
    any_spec = pl.BlockSpec(memory_space=pl.ANY)

    def ids(shift):
        def index_map(e, b):
            s = jnp.clip(e * B + b + shift, 0, E * B - 1)
            return (s % B, s // B, 0, 0)
        return pl.BlockSpec((None, None, 1, cap), index_map, memory_space=pltpu.SMEM)

    return pl.pallas_call(
        functools.partial(_moe_kernel, seq=S, cap=cap, f_chunk=f_chunk, n_tokens=T),
        out_shape=jax.ShapeDtypeStruct(hx.shape, F32), grid=(E, B),
        in_specs=[ids(-1), ids(0), ids(1),
                  pl.BlockSpec((None, None, 1, cap), lambda e, b: (b, e, 0, 0)),
                  wspec(w_gate), wspec(w_up), wspec(w_down), any_spec],
        out_specs=any_spec,
        scratch_shapes=[pltpu.VMEM((N_SLOTS, cap, 2 * D), F32), pltpu.SemaphoreType.DMA((2, N_SLOTS))],
        input_output_aliases={7: 0},
        compiler_params=_params(2), name="moe_ffn")(idx, idx, idx, gate, w_gate, w_up, w_down, hx)


def moe_layer(o1, o2, w_out, x2, g_ffn, w_router, w_gate, w_up, w_down, B, S, tm=512):
    hx, aff = _outproj_router(o1, o2, w_out, x2, g_ffn, w_router, B, S, tm)
    cap = EC_CAPACITY_FACTOR * S // w_router.shape[1]
    idx, gate = _topk(aff, cap)
    return _moe_ffn(hx, idx, gate, w_gate.astype(BF16), w_up.astype(BF16), w_down.astype(BF16), B, S)


def _final_norm_kernel(x_ref, g_ref, o_ref):
    o_ref[...] = _rms(x_ref[...], g_ref[...])


def _final_norm(x2, g, tm, T):
    D = g.shape[-1]
    row = pl.BlockSpec((tm, D), lambda i: (i, 0))
    return pl.pallas_call(
        _final_norm_kernel, out_shape=jax.ShapeDtypeStruct((T, D), F32), grid=(T // tm,),
        in_specs=[_stream_spec(x2, D, tm), pl.BlockSpec((1, D), lambda i: (0, 0))], out_specs=row,
        compiler_params=_params(1), name="final_norm")(x2, g.reshape(1, D))


ODD_SPLITS = (C_Q_LORA, C_KV_LORA, C_ROPE, D_W, D_W, D_W, D_W, 4 * D_HEADS)
N_GATES = 4 * D_HEADS
I_FWD, F_FWD, I_BWD, F_BWD = 0, D_HEADS, 2 * D_HEADS, 3 * D_HEADS


def _odd_weights(w_in):
    cq, ckv, kr, dq, dk, dv, do, dg = jnp.split(w_in, [int(c) for c in np.cumsum(ODD_SPLITS)[:-1]], axis=1)
    D = w_in.shape[0]
    zeros = lambda n: jnp.zeros((D, n), w_in.dtype)
    kr_slot = jnp.concatenate([zeros(C_NOPE), kr, zeros(C_SLOT - C_NOPE - C_ROPE)], axis=1)
    dg_slot = jnp.concatenate([dg, zeros(LANES - N_GATES)], axis=1)
    w = jnp.concatenate([cq, ckv, kr_slot, dq, dk, dv, do, dg_slot], axis=1)
    return w.astype(BF16), dg.T.astype(BF16)


def _inproj_odd_kernel(x_ref, g_ref, w_ref, wgt_ref,
                       cq_ref, ckv_ref, kr_ref, dqk_ref, dv_ref, do_ref, gc_ref, gt_ref):
    h = _rms(x_ref[...], g_ref[...]).astype(BF16)
    off = 0
    for ref, width in ((cq_ref, C_Q_LORA), (ckv_ref, C_KV_LORA), (kr_ref, C_SLOT), (dqk_ref, 2 * D_W),
                       (dv_ref, D_W), (do_ref, D_W), (gc_ref, LANES)):
        ref[...] = _dot(h, w_ref[:, off:off + width]).astype(ref.dtype)
        off += width
    gt_ref[...] = _dot_nt(wgt_ref[...], h)


def _inproj_odd(x2, g, w, wgt, tm, T):
    D = g.shape[-1]
    row = lambda width: pl.BlockSpec((tm, width), lambda i: (i, 0))
    full = lambda a: pl.BlockSpec(a.shape, lambda i: (0, 0))
    widths = (C_Q_LORA, C_KV_LORA, C_SLOT, 2 * D_W, D_W, D_W, LANES)
    dtypes = (F32, F32, F32, F32, BF16, F32, F32)
    out_shape = tuple(jax.ShapeDtypeStruct((T, wd), dt) for wd, dt in zip(widths, dtypes))
    out_shape += (jax.ShapeDtypeStruct((N_GATES, T), F32),)
    return pl.pallas_call(
        _inproj_odd_kernel, out_shape=out_shape, grid=(T // tm,),
        in_specs=[_stream_spec(x2, D, tm), full(g), full(w), full(wgt)],
        out_specs=tuple(row(wd) for wd in widths) + (pl.BlockSpec((N_GATES, tm), lambda i: (0, i)),),
        compiler_params=_params(1), name="inproj_odd")(x2, g, w, wgt)


def _mla_prep_kernel(cq_ref, ckv_ref, kr_ref, nq_ref, nkv_ref, wq_ref, wk_ref, wv_ref, cos_ref, sin_ref,
                     q_ref, k_ref, vt_ref):
    lane = lax.broadcasted_iota(I32, (1, LANES), 1)
    first = lane < C_NOPE + C_ROPE // 2
    cos, sin = cos_ref[...], sin_ref[...]
    rope = lambda z: _rope_slab(z, cos, sin, C_ROPE // 2, first)
    q = _dot(_rms(cq_ref[...], nq_ref[...]).astype(BF16), wq_ref[...])
    hkv = _rms(ckv_ref[...], nkv_ref[...]).astype(BF16)
    kn = _dot(hkv, wk_ref[...])
    kr = rope(kr_ref[...])
    for hh in range(C_HEADS):
        slab = slice(hh * C_SLOT, (hh + 1) * C_SLOT)
        q_ref[:, slab] = (rope(q[:, slab]) * C_LOG2_SCALE).astype(BF16)
        k_ref[:, slab] = (kn[:, slab] + kr).astype(BF16)
    vt = _dot_nt(wv_ref[...], hkv)
    row = lax.broadcasted_iota(I32, vt.shape, 0)
    vt_ref[...] = jnp.where(row % C_VROWS < C_V, vt, 1.0).astype(BF16)


def _mla_prep(cq, ckv, kr, nq, nkv, w_uq, w_ukv, cos, sin, tm):
    T = cq.shape[0]
    pad_q = C_SLOT - C_NOPE - C_ROPE
    wq = jnp.pad(w_uq.reshape(C_Q_LORA, C_HEADS, C_NOPE + C_ROPE), ((0, 0), (0, 0), (0, pad_q)))
    wq = wq.reshape(C_Q_LORA, C_HEADS * C_SLOT).astype(BF16)
    wkv = w_ukv.reshape(C_KV_LORA, C_HEADS, C_NOPE + C_V)
    wk = jnp.pad(wkv[:, :, :C_NOPE], ((0, 0), (0, 0), (0, C_SLOT - C_NOPE)))
    wk = wk.reshape(C_KV_LORA, C_HEADS * C_SLOT).astype(BF16)
    wv = jnp.pad(wkv[:, :, C_NOPE:], ((0, 0), (0, 0), (0, C_VROWS - C_V)))
    wv = wv.reshape(C_KV_LORA, C_HEADS * C_VROWS).T.astype(BF16)
    nq, nkv = nq.reshape(1, -1), nkv.reshape(1, -1)
    row = lambda width: pl.BlockSpec((tm, width), lambda i: (i, 0))
    full = lambda a: pl.BlockSpec(a.shape, lambda i: (0, 0))
    slots = jax.ShapeDtypeStruct((T, C_HEADS * C_SLOT), BF16)
    vt_shape = jax.ShapeDtypeStruct((T // tm, C_HEADS * C_VROWS, tm), BF16)
    return pl.pallas_call(
        _mla_prep_kernel, out_shape=(slots, slots, vt_shape), grid=(T // tm,),
        in_specs=[row(C_Q_LORA), row(C_KV_LORA), row(C_SLOT), full(nq), full(nkv), full(wq), full(wk), full(wv),
                  row(LANES), row(LANES)],
        out_specs=(row(C_HEADS * C_SLOT), row(C_HEADS * C_SLOT),
                   pl.BlockSpec((None, C_HEADS * C_VROWS, tm), lambda i: (i, 0, 0))),
        compiler_params=_params(1), name="mla_prep")(cq, ckv, kr, nq, nkv, wq, wk, wv, cos, sin)


def _mla_attn_kernel(q_ref, k_ref, vt_ref, o_ref):
    tq = q_ref.shape[0]
    n_chunks, _, key_chunk = vt_ref.shape
    slabs = [slice(hh * C_SLOT, (hh + 1) * C_SLOT) for hh in range(C_PAIR)]
    vrows = [slice(hh * C_VROWS, (hh + 1) * C_VROWS) for hh in range(C_PAIR)]
    qs = [q_ref[:, slab] for slab in slabs]

    def scores(hh, c):
        st = _dot_nt(k_ref[c * key_chunk:(c + 1) * key_chunk, slabs[hh]], qs[hh])
        return st, jnp.max(st, axis=0, keepdims=True)

    m = [jnp.full((1, tq), NEG, F32) for _ in range(C_PAIR)]
    acc = [jnp.zeros((C_VROWS, tq), F32) for _ in range(C_PAIR)]
    st = [scores(hh, 0) for hh in range(C_PAIR)]
    for c in range(n_chunks):
        for hh in range(C_PAIR):
            st_next = scores(hh, c + 1) if c + 1 < n_chunks else None
            m_new = jnp.maximum(m[hh], st[hh][1])
            pt = jnp.exp2((st[hh][0] - m_new).astype(BF16))
            acc[hh] = jnp.exp2(m[hh] - m_new) * acc[hh] + _dot(vt_ref[c, vrows[hh], :], pt)
            m[hh], st[hh] = m_new, st_next
    for hh in range(C_PAIR):
        ot = acc[hh][:C_V] / acc[hh][C_V:C_V + 1]
        o = jnp.concatenate([ot, jnp.zeros((LANES - C_V, tq), F32)], axis=0).T
        o_ref[:, hh * C_V:(hh + 1) * C_V] = o[:, :C_V].astype(BF16)


def _mla_attn(q, k, vt, B, S, tq):
    T = B * S
    nq = S // tq
    key_chunk = vt.shape[-1]
    vt = vt.reshape(B, S // key_chunk, C_HEADS * C_VROWS, key_chunk)
    return pl.pallas_call(
        _mla_attn_kernel,
        out_shape=jax.ShapeDtypeStruct((T, C_HEADS * C_V), BF16), grid=(B, C_HEADS // C_PAIR, nq),
        in_specs=[pl.BlockSpec((tq, C_PAIR * C_SLOT), lambda b, p, i: (b * nq + i, p)),
                  pl.BlockSpec((S, C_PAIR * C_SLOT), lambda b, p, i: (b, p)),
                  pl.BlockSpec((None, S // key_chunk, C_PAIR * C_VROWS, key_chunk), lambda b, p, i: (b, 0, p, 0))],
        out_specs=pl.BlockSpec((tq, C_PAIR * C_V), lambda b, p, i: (b * nq + i, p)),
        compiler_params=_params(3), name="mla_attn")(q, k, vt)


HALO = 8


def _conv_kernel(xp_ref, x_ref, xn_ref, w_ref, o_ref, *, n_tiles):
    j = pl.program_id(1)
    tc = x_ref.shape[0]
    prev = jnp.where(j > 0, xp_ref[...], 0.0)
    nxt = jnp.where(j < n_tiles - 1, xn_ref[...], 0.0)
    ext = jnp.concatenate([prev, x_ref[...], nxt], axis=0)
    rows = tc + 2 * HALO
    y = jnp.zeros(x_ref.shape, F32)
    for w in range(D_CONV):
        first = HALO - D_CONV // 2 + w
        y = y + pltpu.roll(ext, (rows - first) % rows, 0)[:tc] * w_ref[w:w + 1, :]
    y = y * jax.nn.sigmoid(y)
    o_ref[:, :D_W] = y[:, :D_W]
    o_ref[:, D_W:] = y[:, D_W:] * (D_HEAD_DIM ** -0.5)


def _conv_prep(dqk, conv_w, B, S, tc):
    T, C = dqk.shape
    n_tiles = S // tc
    per, last = tc // HALO, T // HALO - 1
    cur = lambda b, j: (b * n_tiles + j, 0)
    return pl.pallas_call(
        functools.partial(_conv_kernel, n_tiles=n_tiles),
        out_shape=jax.ShapeDtypeStruct((T, C), F32), grid=(B, n_tiles),
        in_specs=[pl.BlockSpec((HALO, C), lambda b, j: (jnp.maximum((b * n_tiles + j) * per - 1, 0), 0)),
                  pl.BlockSpec((tc, C), cur),
                  pl.BlockSpec((HALO, C), lambda b, j: (jnp.minimum((b * n_tiles + j + 1) * per, last), 0)),
                  pl.BlockSpec(conv_w.shape, lambda b, j: (0, 0))],
        out_specs=pl.BlockSpec((tc, C), cur),
        compiler_params=_params(2), name="conv_prep")(dqk, dqk, dqk, conv_w)


def _log_sigmoid(x):
    return -(jnp.maximum(-x, 0.0) + jnp.log1p(jnp.exp(-jnp.abs(x))))


def _tri(lower):
    a = lax.broadcasted_iota(I32, (BLOCK, BLOCK), 0)
    b = lax.broadcasted_iota(I32, (BLOCK, BLOCK), 1)
    return ((b <= a) if lower else (b >= a)).astype(BF16)


def _mlstm_update(s_ref, m_ref, h, g, a, k, v_ones):
    a_max = jnp.max(a, axis=0, keepdims=True)
    w = jnp.exp(a - a_max)
    upd = _dot_tn(v_ones, (k * w).astype(BF16))
    m = m_ref[h:h + 1, 0:1]
    m_new = jnp.maximum(g + m, a_max)
    rows = slice(h * 2 * D_HEAD_DIM, (h + 1) * 2 * D_HEAD_DIM)
    s_ref[rows, :] = jnp.exp(g + m - m_new) * s_ref[rows, :] + jnp.exp(a_max - m_new) * upd
    m_ref[h:h + 1, :] = jnp.broadcast_to(m_new, (1, LANES))


def _mlstm_bwd_state_kernel(bias_c_ref, qk_ref, v_ref, gc_ref, s_out_ref, m_out_ref, s_ref, m_ref):
    @pl.when(pl.program_id(1) == 0)
    def _():
        s_ref[...] = jnp.zeros_like(s_ref)
        m_ref[...] = jnp.zeros_like(m_ref)

    s_out_ref[...] = s_ref[...]
    m_out_ref[...] = m_ref[...]
    gc = gc_ref[...] + bias_c_ref[...]
    suffix = sum(_dot(_tri(False), p) for p in _split3(_log_sigmoid(gc)))
    ones = jnp.ones((BLOCK, D_HEAD_DIM), BF16)
    for h in range(D_HEADS):
        hs = slice(h * D_HEAD_DIM, (h + 1) * D_HEAD_DIM)
        sb = suffix[:, F_BWD + h:F_BWD + h + 1]
        g = sb[0:1, :]
        a = g - sb + gc[:, I_BWD + h:I_BWD + h + 1]
        k = qk_ref[:, D_W + h * D_HEAD_DIM:D_W + (h + 1) * D_HEAD_DIM]
        _mlstm_update(s_ref, m_ref, h, g, a, k, jnp.concatenate([v_ref[:, hs], ones], axis=1))


def _mlstm_bwd_state(bias_c, qk, dv, gc, B, S):
    N = S // BLOCK
    rev = lambda b, n: (b * N + N - 1 - n, 0)
    out_shape = (jax.ShapeDtypeStruct((B, N, 2 * D_W, D_HEAD_DIM), F32), jax.ShapeDtypeStruct((B, N, 8, LANES), F32))
    return pl.pallas_call(
        _mlstm_bwd_state_kernel, out_shape=out_shape, grid=(B, N),
        in_specs=[pl.BlockSpec(bias_c.shape, lambda b, n: (0, 0)), pl.BlockSpec((BLOCK, 2 * D_W), rev),
                  pl.BlockSpec((BLOCK, D_W), rev), pl.BlockSpec((BLOCK, LANES), rev)],
        out_specs=(pl.BlockSpec((None, None, 2 * D_W, D_HEAD_DIM), lambda b, n: (b, N - 1 - n, 0, 0)),
                   pl.BlockSpec((None, None, 8, LANES), lambda b, n: (b, N - 1 - n, 0, 0))),
        scratch_shapes=[pltpu.VMEM((2 * D_W, D_HEAD_DIM), F32), pltpu.VMEM((8, LANES), F32)],
        compiler_params=_params(2), name="mlstm_bwd_state")(bias_c, qk, dv, gc)


def _mlstm_main_kernel(bias_c_ref, bias_r_ref, qk_ref, v_ref, og_ref, gc_ref, gt_ref, sb_ref, mb_ref,
                       o_ref, s_ref, m_ref):
    @pl.when(pl.program_id(1) == 0)
    def _():
        s_ref[...] = jnp.zeros_like(s_ref)
        m_ref[...] = jnp.zeros_like(m_ref)

    gc = gc_ref[...] + bias_c_ref[...]
    gr = gt_ref[...] + bias_r_ref[...]
    lower, upper = _tri(True), _tri(False)
    lfc, lfr = _split3(_log_sigmoid(gc)), _split3(_log_sigmoid(gr))
    pre_c = sum(_dot(lower, p) for p in lfc)
    suf_c = sum(_dot(upper, p) for p in lfc)
    pre_r = sum(_dot(p, upper) for p in lfr)
    suf_r = sum(_dot(p, lower) for p in lfr)
    src = lax.broadcasted_iota(I32, (BLOCK, BLOCK), 0)
    qry = lax.broadcasted_iota(I32, (BLOCK, BLOCK), 1)
    ones = jnp.ones((BLOCK, D_HEAD_DIM), BF16)
    og = og_ref[...]
    for h in range(D_HEADS):
        hs = slice(h * D_HEAD_DIM, (h + 1) * D_HEAD_DIM)
        rows = slice(h * 2 * D_HEAD_DIM, (h + 1) * 2 * D_HEAD_DIM)
        k = qk_ref[:, D_W + h * D_HEAD_DIM:D_W + (h + 1) * D_HEAD_DIM]
        qb, v = qk_ref[:, hs].astype(BF16), v_ref[:, hs]
        kq = _dot_nt(k.astype(BF16), qb)

        def direction(b_row, c_col, mask, state_t, m_prev):
            logd = jnp.where(mask, b_row + c_col, NEG)
            log_inter = b_row + m_prev
            m_t = jnp.maximum(jnp.max(logd, axis=0, keepdims=True), log_inter)
            s = kq * jnp.exp(logd - m_t)
            inter_w = jnp.exp(log_inter - m_t)
            qs = _dot_nt(state_t.astype(BF16), qb)
            num = _dot_tn(v, s.astype(BF16)) + inter_w * qs[:D_HEAD_DIM, :]
            den = jnp.sum(s, axis=0, keepdims=True) + inter_w * qs[D_HEAD_DIM:D_HEAD_DIM + 1, :]
            return num / jnp.maximum(jnp.abs(den), jnp.exp(-m_t))

        f, bk = F_FWD + h, F_BWD + h
        h_f = direction(pre_r[f:f + 1, :], gc[:, I_FWD + h:I_FWD + h + 1] - pre_c[:, f:f + 1], src <= qry,
                        s_ref[rows, :], m_ref[h:h + 1, 0:1])
        h_b = direction(suf_r[bk:bk + 1, :], gc[:, I_BWD + h:I_BWD + h + 1] - suf_c[:, bk:bk + 1], src > qry,
                        sb_ref[rows, :], mb_ref[h:h + 1, 0:1])
        o_ref[:, hs] = (jax.nn.sigmoid(og[:, hs]) * (h_f + h_b).T).astype(BF16)
        b_col = pre_c[:, f:f + 1]
        g = b_col[BLOCK - 1:BLOCK, :]
        a = g - b_col + gc[:, I_FWD + h:I_FWD + h + 1]
        _mlstm_update(s_ref, m_ref, h, g, a, k, jnp.concatenate([v, ones], axis=1))


def _mlstm_main(bias_c, bias_r, qk, dv, og, gc, gt, sb, mb, B, S):
    N = S // BLOCK
    T = B * S
    cur = lambda b, n: (b * N + n, 0)
    return pl.pallas_call(
        _mlstm_main_kernel, out_shape=jax.ShapeDtypeStruct((T, D_W), BF16), grid=(B, N),
        in_specs=[pl.BlockSpec(bias_c.shape, lambda b, n: (0, 0)), pl.BlockSpec(bias_r.shape, lambda b, n: (0, 0)),
                  pl.BlockSpec((BLOCK, 2 * D_W), cur), pl.BlockSpec((BLOCK, D_W), cur), pl.BlockSpec((BLOCK, D_W), cur),
                  pl.BlockSpec((BLOCK, LANES), cur), pl.BlockSpec((N_GATES, BLOCK), lambda b, n: (0, b * N + n)),
                  pl.BlockSpec((None, None, 2 * D_W, D_HEAD_DIM), lambda b, n: (b, n, 0, 0)),
                  pl.BlockSpec((None, None, 8, LANES), lambda b, n: (b, n, 0, 0))],
        out_specs=pl.BlockSpec((BLOCK, D_W), cur),
        scratch_shapes=[pltpu.VMEM((2 * D_W, D_HEAD_DIM), F32), pltpu.VMEM((8, LANES), F32)],
        compiler_params=_params(2), name="mlstm_main")(bias_c, bias_r, qk, dv, og, gc, gt, sb, mb)


def odd_mixer_parts(x2, positions, g_mix, w_in, norm_q, norm_kv, w_uq, w_ukv, conv_w, gate_bias, B, S,
                    tm=512, tq=256, key_chunk=1024):
    w, wgt = _odd_weights(w_in)
    cq, ckv, kr, dqk, dv, og, gc, gt = _inproj_odd(x2, g_mix.reshape(1, -1), w, wgt, tm, B * S)
    cos, sin = _rope_tables(positions, MLA_THETA, C_SLOT, C_NOPE, C_ROPE)
    q, k, vt = _mla_prep(cq, ckv, kr, norm_q, norm_kv, w_uq, w_ukv, cos, sin, min(key_chunk, S))
    o_c = _mla_attn(q, k, vt, B, S, min(tq, S))
    qk = _conv_prep(dqk, conv_w.astype(F32), B, S, min(tm, S))
    bias = gate_bias.astype(F32).reshape(1, N_GATES)
    bias_c = jnp.pad(bias, ((0, 0), (0, LANES - N_GATES)))
    bias_r = bias.reshape(N_GATES, 1)
    sb, mb = _mlstm_bwd_state(bias_c, qk, dv, gc, B, S)
    o_d = _mlstm_main(bias_c, bias_r, qk, dv, og, gc, gt, sb, mb, B, S)
    return o_c, o_d


def kernel(x, positions, norm_mix, norm_ffn, norm_final, ev_w_in, ev_w_out, attn_sink, ret_decay_logit,
           od_w_in, od_w_out, mla_norm_q, mla_norm_kv, mla_w_uq, mla_w_ukv, mlstm_conv, mlstm_gate_bias,
           moe_router, moe_w_gate, moe_w_up, moe_w_down):
    B, S, D = x.shape
    tm = min(512, S)
    x2 = x.reshape(B * S, D)
    for layer in range(norm_mix.shape[0]):
        j = layer // 2
        if layer % 2 == 0:
            o1, o2 = even_mixer_parts(x2, positions, norm_mix[layer], ev_w_in[j], attn_sink[j],
                                      ret_decay_logit[j], B, S, tm)
            w_out = ev_w_out[j]
        else:
            o1, o2 = odd_mixer_parts(x2, positions, norm_mix[layer], od_w_in[j], mla_norm_q[j], mla_norm_kv[j],
                                     mla_w_uq[j], mla_w_ukv[j], mlstm_conv[j], mlstm_gate_bias[j], B, S, tm)
            w_out = od_w_out[j]
        x2 = moe_layer(o1, o2, w_out, x2, norm_ffn[layer], moe_router[layer], moe_w_gate[layer],
                       moe_w_up[layer], moe_w_down[layer], B, S, tm)
    return _final_norm(x2, norm_final, tm, B * S).reshape(B, S, D)
```

```python
import functools

import jax
import jax.numpy as jnp
import numpy as np
from jax import lax
from jax.experimental import pallas as pl
from jax.experimental.pallas import tpu as pltpu

F32 = jnp.float32
BF16 = jnp.bfloat16
I32 = jnp.int32

LANES = 128
BLOCK = 128
RMS_EPS = 1e-6
GN_EPS = 1e-5
NEG = -1e30
VMEM_LIMIT = 56 * 1024 * 1024

A_HEADS, A_KV_HEADS, A_HEAD_DIM = 8, 2, 64
ROPE_THETA = 500000.0
ROPE_DIM = A_HEAD_DIM // 4
B_HEADS, B_HEAD_DIM = 8, 64
RET_THETA = 10000.0
C_HEADS, C_NOPE, C_ROPE, C_V = 8, 64, 32, 64
C_Q_LORA, C_KV_LORA = 512, 256
MLA_THETA = 10000.0
D_HEADS, D_HEAD_DIM, D_CONV = 4, 128, 5
N_EXPERTS = 16
EC_CAPACITY_FACTOR = 2

A_Q_W = A_HEADS * A_HEAD_DIM
A_KV_W = A_KV_HEADS * A_HEAD_DIM
B_W = B_HEADS * B_HEAD_DIM
D_W = D_HEADS * D_HEAD_DIM
C_SLOT = 128
C_PAIR = 4
BF16_ROWS = 16
C_VROWS = C_V + BF16_ROWS
C_LOG2_SCALE = float((C_NOPE + C_ROPE) ** -0.5 * np.log2(np.e))


def _params(n_axes, vmem=VMEM_LIMIT):
    return pltpu.CompilerParams(dimension_semantics=("arbitrary",) * n_axes,
                                vmem_limit_bytes=vmem)


def _rms(x, g):
    return x * lax.rsqrt(jnp.mean(x * x, axis=-1, keepdims=True) + RMS_EPS) * g


def _dot(a, b):
    return jnp.dot(a, b, preferred_element_type=F32)


def _dot_nt(a, b):
    return lax.dot_general(a, b, (((1,), (1,)), ((), ())), preferred_element_type=F32)


def _dot_tn(a, b):
    return lax.dot_general(a, b, (((0,), (0,)), ((), ())), preferred_element_type=F32)


def _rope_slab(z, cos, sin, half, first_half):
    partner = jnp.where(first_half, pltpu.roll(z, LANES - half, 1), pltpu.roll(z, half, 1))
    return z * cos + partner * sin


def _rope_tables(positions, theta, head_dim, rot_start, rot_dim):
    half = rot_dim // 2
    assert head_dim % half == 0 and rot_start % half == 0
    inv_freq = theta ** (-jnp.arange(half, dtype=F32) * 2.0 / rot_dim)
    d = np.arange(LANES) % head_dim - rot_start
    rot = (d >= 0) & (d < rot_dim)
    sign = np.where(d < half, -1.0, 1.0).astype(np.float32)
    ang = positions.astype(F32).reshape(-1, 1) * inv_freq
    ang = jnp.tile(ang, (1, LANES // half))
    cos = jnp.where(rot[None, :], jnp.cos(ang), 1.0)
    sin = jnp.where(rot[None, :], jnp.sin(ang) * sign[None, :], 0.0)
    return cos, sin


def _inproj_even_kernel(x_ref, g_ref, w_ref, ca_ref, sa_ref, cb_ref, sb_ref,
                        aq_ref, akv_ref, bq_ref, bk_ref, bv_ref, bg_ref):
    h = _rms(x_ref[...], g_ref[...]).astype(BF16)
    lane = lax.broadcasted_iota(I32, (1, LANES), 1)
    first_a = (lane % A_HEAD_DIM) < (ROPE_DIM // 2)
    first_b = (lane % B_HEAD_DIM) < (B_HEAD_DIM // 2)
    ca, sa, cb, sb = ca_ref[...], sa_ref[...], cb_ref[...], sb_ref[...]

    def rope_a(z):
        return _rope_slab(z, ca, sa, ROPE_DIM // 2, first_a)

    def rope_b(z):
        return _rope_slab(z, cb, sb, B_HEAD_DIM // 2, first_b)

    off = 0
    z = _dot(h, w_ref[:, off:off + A_Q_W])
    for s in range(A_Q_W // LANES):
        aq_ref[:, s * LANES:(s + 1) * LANES] = rope_a(z[:, s * LANES:(s + 1) * LANES]).astype(BF16)
    off += A_Q_W
    z = _dot(h, w_ref[:, off:off + 2 * A_KV_W])
    akv_ref[:, :A_KV_W] = rope_a(z[:, :A_KV_W]).astype(BF16)
    akv_ref[:, A_KV_W:] = z[:, A_KV_W:].astype(BF16)
    off += 2 * A_KV_W
    z = _dot(h, w_ref[:, off:off + B_W])
    for s in range(B_W // LANES):
        bq_ref[:, s * LANES:(s + 1) * LANES] = rope_b(z[:, s * LANES:(s + 1) * LANES])
    off += B_W
    z = _dot(h, w_ref[:, off:off + B_W])
    for s in range(B_W // LANES):
        bk_ref[:, s * LANES:(s + 1) * LANES] = rope_b(z[:, s * LANES:(s + 1) * LANES]) * (B_HEAD_DIM ** -0.5)
    off += B_W
    bv_ref[...] = _dot(h, w_ref[:, off:off + B_W]).astype(BF16)
    off += B_W
    bg_ref[...] = _dot(h, w_ref[:, off:off + B_W])


def _stream_spec(xs, D, tm):
    col = xs.shape[1] // D - 1
    return pl.BlockSpec((tm, D), lambda i: (i, col))


def _inproj_even(x2, g, w, tabs, tm):
    T, D = tabs[0].shape[0], g.shape[-1]
    ncol = w.shape[1]
    row = lambda width: pl.BlockSpec((tm, width), lambda i: (i, 0))
    full = lambda a: pl.BlockSpec(a.shape, lambda i: (0, 0))
    out_shape = (jax.ShapeDtypeStruct((T, A_Q_W), BF16), jax.ShapeDtypeStruct((T, 2 * A_KV_W), BF16),
                 jax.ShapeDtypeStruct((T, B_W), F32), jax.ShapeDtypeStruct((T, B_W), F32),
                 jax.ShapeDtypeStruct((T, B_W), BF16), jax.ShapeDtypeStruct((T, B_W), F32))
    return pl.pallas_call(
        _inproj_even_kernel, out_shape=out_shape, grid=(T // tm,),
        in_specs=[_stream_spec(x2, D, tm), full(g), pl.BlockSpec((D, ncol), lambda i: (0, 0))] + [row(LANES)] * 4,
        out_specs=(row(A_Q_W), row(2 * A_KV_W), row(B_W), row(B_W), row(B_W), row(B_W)),
        compiler_params=_params(1), name="inproj_even")(x2, g, w, *tabs)


def _win_attn_kernel(sink_ref, q_ref, kvp_ref, kvc_ref, kvn_ref, o_ref, *, n_blocks):
    n = pl.program_id(1)
    group = A_HEADS // A_KV_HEADS
    cols = group * BLOCK
    kj = lax.broadcasted_iota(I32, (BLOCK, cols), 0)
    qi = lax.broadcasted_iota(I32, (BLOCK, cols), 1) % BLOCK
    ok_prev = (kj >= qi) & (n > 0)
    ok_next = (kj <= qi) & (n < n_blocks - 1)
    q = q_ref[...]
    scale = A_HEAD_DIM ** -0.5
    for g in range(A_KV_HEADS):
        ks = slice(g * A_HEAD_DIM, (g + 1) * A_HEAD_DIM)
        vs = slice(A_KV_W + g * A_HEAD_DIM, A_KV_W + (g + 1) * A_HEAD_DIM)
        qg = jnp.concatenate([q[:, (g * group + i) * A_HEAD_DIM:(g * group + i + 1) * A_HEAD_DIM]
                              for i in range(group)], axis=0)
        sink = jnp.concatenate([jnp.full((1, BLOCK), sink_ref[g * group + i], F32)
                                for i in range(group)], axis=1)
        sp = jnp.where(ok_prev, _dot_nt(kvp_ref[:, ks], qg) * scale, NEG)
        sc = _dot_nt(kvc_ref[:, ks], qg) * scale
        sn = jnp.where(ok_next, _dot_nt(kvn_ref[:, ks], qg) * scale, NEG)
        m = jnp.maximum(jnp.maximum(jnp.max(sp, 0, keepdims=True), jnp.max(sc, 0, keepdims=True)),
                        jnp.maximum(jnp.max(sn, 0, keepdims=True), sink))
        pp, pc, pn = jnp.exp(sp - m), jnp.exp(sc - m), jnp.exp(sn - m)
        den = (jnp.sum(pp, 0, keepdims=True) + jnp.sum(pc, 0, keepdims=True)
               + jnp.sum(pn, 0, keepdims=True) + jnp.exp(sink - m))
        ot = (_dot_tn(kvp_ref[:, vs], pp.astype(BF16)) + _dot_tn(kvc_ref[:, vs], pc.astype(BF16))
              + _dot_tn(kvn_ref[:, vs], pn.astype(BF16))) / den
        o = ot.T
        for i in range(group):
            hh = g * group + i
            o_ref[:, hh * A_HEAD_DIM:(hh + 1) * A_HEAD_DIM] = o[i * BLOCK:(i + 1) * BLOCK].astype(BF16)


def _win_attn(aq, akv, sink, B, S):
    N = S // BLOCK
    T = B * S
    kv = lambda shift: pl.BlockSpec(
        (BLOCK, 2 * A_KV_W), lambda b, n: (b * N + jnp.clip(n + shift, 0, N - 1), 0))
    return pl.pallas_call(
        functools.partial(_win_attn_kernel, n_blocks=N),
        out_shape=jax.ShapeDtypeStruct((T, A_Q_W), BF16), grid=(B, N),
        in_specs=[pl.BlockSpec(memory_space=pltpu.SMEM),
                  pl.BlockSpec((BLOCK, A_Q_W), lambda b, n: (b * N + n, 0)), kv(-1), kv(0), kv(1)],
        out_specs=pl.BlockSpec((BLOCK, A_Q_W), lambda b, n: (b * N + n, 0)),
        compiler_params=_params(2), name="win_attn")(sink, aq, akv, akv, akv)


def _ret_tables(lg_ref, lgl_ref, d_ref, xif_ref, xib_ref, zf_ref, zb_ref):
    j = lax.broadcasted_iota(I32, (BLOCK, BLOCK), 0)
    i = lax.broadcasted_iota(I32, (BLOCK, BLOCK), 1)
    rel = (i - j).astype(F32)
    for h in range(B_HEADS):
        d_ref[h] = jnp.where(i >= j, jnp.exp(rel * lg_ref[0, h]), jnp.exp(-rel * lg_ref[1, h]))
    t = lax.broadcasted_iota(I32, (BLOCK, B_W), 0).astype(F32)
    lgf, lgb = lgl_ref[0:1, :], lgl_ref[1:2, :]
    xif_ref[...] = jnp.exp((t + 1.0) * lgf)
    zf_ref[...] = jnp.exp((BLOCK - 1.0 - t) * lgf)
    xib_ref[...] = jnp.exp((BLOCK - t) * lgb)
    zb_ref[...] = jnp.exp(t * lgb)


def _ret_bwd_state_kernel(lgl_ref, k_ref, v_ref, r_out_ref, r_ref):
    n = pl.program_id(1)

    @pl.when(n == 0)
    def _():
        r_ref[...] = jnp.zeros_like(r_ref)

    r_out_ref[...] = r_ref[...]
    lgb = lgl_ref[1:2, :]
    t = lax.broadcasted_iota(I32, (BLOCK, B_W), 0).astype(F32)
    kz = (k_ref[...] * jnp.exp(t * lgb)).astype(BF16)
    cd = jnp.exp(BLOCK * lgb)
    _ret_state_update(r_ref, cd, v_ref[...], kz)


def _ret_state_update(r_ref, chunk_decay, v, kz):
    upd = _dot_tn(v, kz)
    for h in range(B_HEADS):
        hs = slice(h * B_HEAD_DIM, (h + 1) * B_HEAD_DIM)
        r_ref[hs, :] = chunk_decay[:, hs] * r_ref[hs, :] + upd[hs, hs]


def _ret_bwd_state(lgl, bk, bv, B, S):
    N = S // BLOCK
    blk = lambda: pl.BlockSpec((BLOCK, B_W), lambda b, n: (b * N + N - 1 - n, 0))
    return pl.pallas_call(
        _ret_bwd_state_kernel,
        out_shape=jax.ShapeDtypeStruct((B, N, B_W, B_HEAD_DIM), F32), grid=(B, N),
        in_specs=[pl.BlockSpec(lgl.shape, lambda b, n: (0, 0)), blk(), blk()],
        out_specs=pl.BlockSpec((None, None, B_W, B_HEAD_DIM), lambda b, n: (b, N - 1 - n, 0, 0)),
        scratch_shapes=[pltpu.VMEM((B_W, B_HEAD_DIM), F32)],
        compiler_params=_params(2), name="ret_bwd_state")(lgl, bk, bv)


def _ret_main_kernel(lg_ref, lgl_ref, q_ref, k_ref, v_ref, g_ref, rb_ref, o_ref,
                     rf_ref, d_ref, xif_ref, xib_ref, zf_ref, zb_ref):
    b, n = pl.program_id(0), pl.program_id(1)

    @pl.when((b == 0) & (n == 0))
    def _():
        _ret_tables(lg_ref, lgl_ref, d_ref, xif_ref, xib_ref, zf_ref, zb_ref)

    @pl.when(n == 0)
    def _():
        rf_ref[...] = jnp.zeros_like(rf_ref)

    n_seq = q_ref.shape[0]
    cdf = jnp.exp(BLOCK * lgl_ref[0:1, :])
    q, k, v, gate = ([r[i] for i in range(n_seq)] for r in (q_ref, k_ref, v_ref, g_ref))
    qb = [x.astype(BF16) for x in q]
    kb = [x.astype(BF16) for x in k]
    qxf = [(x * xif_ref[...]).astype(BF16) for x in q]
    qxb = [(x * xib_ref[...]).astype(BF16) for x in q]
    kzf = [(x * zf_ref[...]).astype(BF16) for x in k]
    ynt = [[None] * B_HEADS for _ in range(n_seq)]
    for h in range(B_HEADS):
        hs = slice(h * B_HEAD_DIM, (h + 1) * B_HEAD_DIM)
        for i in range(n_seq):
            st = _dot_nt(kb[i][:, hs], qb[i][:, hs]) * d_ref[h]
            states = jnp.concatenate([rf_ref[i, hs, :], rb_ref[i, hs, :]], axis=1).astype(BF16)
            queries = jnp.concatenate([qxf[i][:, hs], qxb[i][:, hs]], axis=1)
            yt = _dot_tn(v[i][:, hs], st.astype(BF16)) + _dot_nt(states, queries)
            yc = yt - jnp.mean(yt, axis=0, keepdims=True)
            ynt[i][h] = yc * lax.rsqrt(jnp.mean(yc * yc, axis=0, keepdims=True) + GN_EPS)
    for i in range(n_seq):
        yn = jnp.concatenate(ynt[i], axis=0).T
        o_ref[i] = (gate[i] * jax.nn.sigmoid(gate[i]) * yn).astype(BF16)
        _ret_state_update(rf_ref.at[i], cdf, v[i], kzf[i])


SEQS_PER_STEP = 2


def _ret_main(lg, lgl, bq, bk, bv, bg, rb, B, S):
    N = S // BLOCK
    nb = SEQS_PER_STEP if B % SEQS_PER_STEP == 0 else 1
    per_seq = lambda a: a.reshape(B, S, a.shape[-1])
    blk = lambda: pl.BlockSpec((nb, BLOCK, B_W), lambda b, n: (b, n, 0))
    tab = lambda: pltpu.VMEM((BLOCK, B_W), F32)
    out = pl.pallas_call(
        _ret_main_kernel, out_shape=jax.ShapeDtypeStruct((B, S, B_W), BF16), grid=(B // nb, N),
        in_specs=[pl.BlockSpec(memory_space=pltpu.SMEM), pl.BlockSpec(lgl.shape, lambda b, n: (0, 0)),
                  blk(), blk(), blk(), blk(),
                  pl.BlockSpec((nb, None, B_W, B_HEAD_DIM), lambda b, n: (b, n, 0, 0))],
        out_specs=blk(),
        scratch_shapes=[pltpu.VMEM((nb, B_W, B_HEAD_DIM), F32), pltpu.VMEM((B_HEADS, BLOCK, BLOCK), F32),
                        tab(), tab(), tab(), tab()],
        compiler_params=_params(2), name="ret_main")(lg, lgl, per_seq(bq), per_seq(bk), per_seq(bv), per_seq(bg), rb)
    return out.reshape(B * S, B_W)


def even_mixer_parts(x2, positions, g_mix, w_in, sink, decay_logit, B, S, tm=512):
    ca, sa = _rope_tables(positions, ROPE_THETA, A_HEAD_DIM, 0, ROPE_DIM)
    cb, sb = _rope_tables(positions, RET_THETA, B_HEAD_DIM, 0, B_HEAD_DIM)
    aq, akv, bq, bk, bv, bg = _inproj_even(x2, g_mix.reshape(1, -1), w_in.astype(BF16), (ca, sa, cb, sb), tm)
    o_a = _win_attn(aq, akv, sink.astype(F32), B, S)
    lg = jax.nn.log_sigmoid(decay_logit.astype(F32))
    lgl = jnp.repeat(lg, B_HEAD_DIM, axis=1)
    rb = _ret_bwd_state(lgl, bk, bv, B, S)
    o_b = _ret_main(lg, lgl, bq, bk, bv, bg, rb, B, S)
    return o_a, o_b


def _outproj_kernel(o1_ref, o2_ref, w_ref, x_ref, g_ref, wr_ref, hx_ref, aff_ref):
    half = o1_ref.shape[1]
    D = x_ref.shape[1]
    x1 = x_ref[...] + (_dot(o1_ref[...], w_ref[:half, :]) + _dot(o2_ref[...], w_ref[half:, :]))
    hn = _rms(x1, g_ref[...])
    hx_ref[:, :D] = hn
    hx_ref[:, D:] = x1
    logits = _dot_nt(wr_ref[...], hn.astype(BF16))
    e = jnp.exp(logits - jnp.max(logits, axis=0, keepdims=True))
    aff_ref[...] = e / jnp.sum(e, axis=0, keepdims=True)


def _outproj_router(o1, o2, w_out, x2, g_ffn, w_router, B, S, tm):
    T, D = o1.shape[0], g_ffn.shape[-1]
    E = w_router.shape[1]
    spare = EC_CAPACITY_FACTOR * S // E
    per = S // tm
    row = lambda width: pl.BlockSpec((tm, width), lambda i: (i, 0))
    full = lambda a: pl.BlockSpec(a.shape, lambda i: (0, 0))
    w = w_out.astype(BF16)
    g = g_ffn.reshape(1, D)
    wr = w_router.T.astype(BF16)
    return pl.pallas_call(
        _outproj_kernel,
        out_shape=(jax.ShapeDtypeStruct((T + spare, 2 * D), F32), jax.ShapeDtypeStruct((B, E, S), F32)),
        grid=(T // tm,),
        in_specs=[row(o1.shape[1]), row(o2.shape[1]), full(w), _stream_spec(x2, D, tm), full(g), full(wr)],
        out_specs=(row(2 * D), pl.BlockSpec((None, E, tm), lambda i: (i // per, 0, i % per))),
        compiler_params=_params(1), name="outproj_router")(o1, o2, w, x2, g, wr)


def _split3(x):
    x1 = x.astype(BF16)
    r = x - x1.astype(F32)
    x2 = r.astype(BF16)
    x3 = (r - x2.astype(F32)).astype(BF16)
    return x1, x2, x3


def _topk_kernel(aff_ref, idx_ref, gate_ref, thr_ref, *, cap):
    E, R, _ = aff_ref.shape
    bits = lax.bitcast_convert_type(aff_ref[...], I32)

    def count(mask):
        return jnp.sum(jnp.sum(mask.astype(F32), axis=2, keepdims=True), axis=1, keepdims=True)

    def bit_body(i, prefix):
        cand = prefix | jnp.left_shift(jnp.int32(1), 30 - i)
        return jnp.where(count(bits >= cand) >= cap, cand, prefix)

    thr = lax.fori_loop(0, 31, bit_body, jnp.zeros((E, 1, 1), I32))
    thr_ref[...] = jnp.broadcast_to(thr, thr_ref.shape)

    li = lax.broadcasted_iota(I32, (LANES, LANES), 0)
    lj = lax.broadcasted_iota(I32, (LANES, LANES), 1)
    tri = (li <= lj).astype(BF16)
    ri = lax.broadcasted_iota(I32, (R, R), 0)
    rj = lax.broadcasted_iota(I32, (R, R), 1)
    below = (rj < ri).astype(BF16)
    slot = lax.broadcasted_iota(I32, (1, cap), 1).astype(F32)
    tok = (lax.broadcasted_iota(I32, (R, LANES), 0) * LANES
           + lax.broadcasted_iota(I32, (R, LANES), 1)).astype(F32)

    def prefix_counts(m):
        within = _dot(m.astype(BF16), tri)
        total = jnp.broadcast_to(within[:, LANES - 1:LANES], (R, LANES))
        before = _dot(below, total.astype(BF16))
        return within, total, before

    def expert(e, carry):
        a = aff_ref[e]
        b = lax.bitcast_convert_type(a, I32)
        t = thr_ref[e]
        gt, eq = b > t, b == t
        n_gt = jnp.sum(jnp.sum(gt.astype(F32), axis=1, keepdims=True), axis=0, keepdims=True)
        eqf = eq.astype(F32)
        within, _, before = prefix_counts(eqf)
        sel = gt | (eq & (before + within - eqf < cap - n_gt))
        self_ = sel.astype(F32)
        within, total, before = prefix_counts(self_)
        rank = before + within
        first, count_r = before[:, 0:1], total[:, 0:1]
        owner = ((first <= slot) & (slot < first + count_r)).astype(BF16)

        pieces = [self_.astype(BF16)] + list(_split3(rank)[:2]) + list(_split3(tok)[:2]) + list(_split3(a))
        rows = _dot_tn(jnp.concatenate(pieces, axis=1), owner)
        part = lambda n: rows[n * LANES:(n + 1) * LANES]
        hit = (part(0) > 0.5) & (part(1) + part(2) == slot + 1.0)
        idx_ref[e] = jnp.sum(jnp.where(hit, part(3) + part(4), 0.0), axis=0, keepdims=True).astype(I32)
        gate_ref[e] = jnp.sum(jnp.where(hit, part(5) + part(6) + part(7), 0.0), axis=0, keepdims=True)
        return carry

    lax.fori_loop(0, E, expert, 0)


def _topk(aff, cap):
    B, E, S = aff.shape
    assert S < 2 ** 16
    R = S // LANES
    aff4 = aff.reshape(B, E, R, LANES)
    out = jax.ShapeDtypeStruct((B, E, 1, cap), I32), jax.ShapeDtypeStruct((B, E, 1, cap), F32)
    spec = pl.BlockSpec((None, E, 1, cap), lambda b: (b, 0, 0, 0))
    return pl.pallas_call(
        functools.partial(_topk_kernel, cap=cap), out_shape=out, grid=(B,),
        in_specs=[pl.BlockSpec((None, E, R, LANES), lambda b: (b, 0, 0, 0))],
        out_specs=(spec, spec), scratch_shapes=[pltpu.VMEM((E, 1, LANES), I32)],
        compiler_params=_params(1), name="topk")(aff4)


ROW_UNROLL = 8
M_BLOCKS = 2
SEM_IN, SEM_OUT = 0, 1


N_SLOTS = 3


def _moe_kernel(idx_prev_ref, idx_ref, idx_next_ref, gate_ref, wg_ref, wu_ref, wd_ref, hx_alias, hx_hbm,
                buf, sems, *, seq, cap, f_chunk, n_tokens):
    del hx_alias
    D = wg_ref.shape[0]
    n_batch = pl.num_programs(1)
    step = pl.program_id(0) * n_batch + pl.program_id(1)
    n_steps = pl.num_programs(0) * n_batch
    cur, nxt, prv = step % N_SLOTS, (step + 1) % N_SLOTS, (step + 2) % N_SLOTS
    res = pl.ds(D, D)

    def for_rows(fn):
        @pl.loop(0, cap // ROW_UNROLL)
        def _(g):
            for u in range(ROW_UNROLL):
                fn(g * ROW_UNROLL + u)

    def gather_row(t, s, j):
        pltpu.make_async_copy(hx_hbm.at[pl.ds(t, 1)], buf.at[s, pl.ds(j, 1)], sems.at[SEM_IN, s]).start()

    def write_row(t, s, j):
        pltpu.make_async_copy(buf.at[s, pl.ds(j, 1), res], hx_hbm.at[pl.ds(t, 1), res], sems.at[SEM_OUT, s]).start()

    def wait_gathers(s):
        pltpu.make_async_copy(hx_hbm.at[pl.ds(0, cap)], buf.at[s], sems.at[SEM_IN, s]).wait()

    def wait_writes(s):
        pltpu.make_async_copy(buf.at[s, :, res], hx_hbm.at[pl.ds(0, cap), res], sems.at[SEM_OUT, s]).wait()

    this_base = pl.program_id(1) * seq
    next_base = ((step + 1) % n_batch) * seq
    prev_base = ((step + n_batch - 1) % n_batch) * seq
    first = step == 0

    @pl.when(first)
    def _():
        for_rows(lambda j: gather_row(this_base + idx_ref[0, j], cur, j))
        buf[prv, :, D:] = jnp.zeros((cap, D), F32)

    @pl.when(step > 0)
    def _():
        wait_writes(nxt)

    wait_gathers(cur)
    xin = buf[cur, :, :D].astype(BF16)
    n_chunks = wg_ref.shape[1] // f_chunk
    n_rows = cap // M_BLOCKS
    ahead = cap // (n_chunks * M_BLOCKS)
    accs = [jnp.zeros((n_rows, D), F32) for _ in range(M_BLOCKS)]
    piece = 0
    for f in range(n_chunks):
        fs = slice(f * f_chunk, (f + 1) * f_chunk)
        for mb in range(M_BLOCKS):
            for j in range(piece * ahead, (piece + 1) * ahead):
                gather_row(next_base + idx_next_ref[0, j], nxt, j)
                write_row(jnp.where(first, n_tokens + j, prev_base + idx_prev_ref[0, j]), prv, j)
            piece += 1
            xs = xin[mb * n_rows:(mb + 1) * n_rows]
            g = _dot(xs, wg_ref[:, fs])
            hid = (g * jax.nn.sigmoid(g) * _dot(xs, wu_ref[:, fs])).astype(BF16)
            accs[mb] = accs[mb] + _dot(hid, wd_ref[fs, :])
    acc = jnp.concatenate(accs, axis=0)
    diag = (lax.broadcasted_iota(I32, (cap, cap), 0) == lax.broadcasted_iota(I32, (cap, cap), 1))
    gate_col = jnp.sum(jnp.where(diag, gate_ref[...], 0.0), axis=1, keepdims=True)
    buf[cur, :, D:] = buf[cur, :, D:] + acc * gate_col

    @pl.when(step == n_steps - 1)
    def _():
        wait_gathers(nxt)
        wait_writes(prv)
        for_rows(lambda j: write_row(this_base + idx_ref[0, j], cur, j))
        wait_writes(cur)


def _moe_ffn(hx, idx, gate, w_gate, w_up, w_down, B, S, f_chunk=512):
    E, D, F = w_gate.shape
    cap = idx.shape[-1]
    T = hx.shape[0] - cap
    f_chunk = min(f_chunk, F)
    assert B >= N_SLOTS and cap % ROW_UNROLL == 0 and cap % (M_BLOCKS * (F // f_chunk)) == 0
    wspec = lambda a: pl.BlockSpec((None,) + a.shape[1:], lambda e, b: (e, 0, 0), pipeline_mode=pl.Buffered(1))
    any_spec = pl.BlockSpec(memory_space=pl.ANY)

    def ids(shift):
        def index_map(e, b):
            s = jnp.clip(e * B + b + shift, 0, E * B - 1)
            return (s % B, s // B, 0, 0)
        return pl.BlockSpec((None, None, 1, cap), index_map, memory_space=pltpu.SMEM)

    return pl.pallas_call(
        functools.partial(_moe_kernel, seq=S, cap=cap, f_chunk=f_chunk, n_tokens=T),
        out_shape=jax.ShapeDtypeStruct(hx.shape, F32), grid=(E, B),
        in_specs=[ids(-1), ids(0), ids(1),
                  pl.BlockSpec((None, None, 1, cap), lambda e, b: (b, e, 0, 0)),
                  wspec(w_gate), wspec(w_up), wspec(w_down), any_spec],
        out_specs=any_spec,
        scratch_shapes=[pltpu.VMEM((N_SLOTS, cap, 2 * D), F32), pltpu.SemaphoreType.DMA((2, N_SLOTS))],
        input_output_aliases={7: 0},
        compiler_params=_params(2), name="moe_ffn")(idx, idx, idx, gate, w_gate, w_up, w_down, hx)


def _cast_kernel(*refs):
    n = len(refs) // 2
    for src, dst in zip(refs[:n], refs[n:]):
        dst[...] = src[...].astype(dst.dtype)


WEIGHT_SPLIT = 2


def _expert_weights_bf16(layer, w_gate, w_up, w_down):
    _, E, D, F = w_gate.shape
    fh = F // WEIGHT_SPLIT
    cols = lambda: pl.BlockSpec((None, None, D, fh), lambda e, h: (layer, e, 0, h))
    rows = lambda: pl.BlockSpec((None, None, fh, D), lambda e, h: (layer, e, h, 0))
    return pl.pallas_call(
        _cast_kernel,
        out_shape=(jax.ShapeDtypeStruct((E, D, F), BF16), jax.ShapeDtypeStruct((E, D, F), BF16),
                   jax.ShapeDtypeStruct((E, F, D), BF16)),
        grid=(E, WEIGHT_SPLIT), in_specs=[cols(), cols(), rows()],
        out_specs=(pl.BlockSpec((None, D, fh), lambda e, h: (e, 0, h)),
                   pl.BlockSpec((None, D, fh), lambda e, h: (e, 0, h)),
                   pl.BlockSpec((None, fh, D), lambda e, h: (e, h, 0))),
        compiler_params=_params(2), name="cast_weights")(w_gate, w_up, w_down)


def moe_layer(o1, o2, w_out, x2, g_ffn, w_router, expert_weights, B, S, tm=512):
    hx, aff = _outproj_router(o1, o2, w_out, x2, g_ffn, w_router, B, S, tm)
    cap = EC_CAPACITY_FACTOR * S // w_router.shape[1]
    idx, gate = _topk(aff, cap)
    return _moe_ffn(hx, idx, gate, *expert_weights, B, S)


def _final_norm_kernel(x_ref, g_ref, o_ref):
    o_ref[...] = _rms(x_ref[...], g_ref[...])


def _final_norm(x2, g, tm, T):
    D = g.shape[-1]
    row = pl.BlockSpec((tm, D), lambda i: (i, 0))
    return pl.pallas_call(
        _final_norm_kernel, out_shape=jax.ShapeDtypeStruct((T, D), F32), grid=(T // tm,),
        in_specs=[_stream_spec(x2, D, tm), pl.BlockSpec((1, D), lambda i: (0, 0))], out_specs=row,
        compiler_params=_params(1), name="final_norm")(x2, g.reshape(1, D))


ODD_SPLITS = (C_Q_LORA, C_KV_LORA, C_ROPE, D_W, D_W, D_W, D_W, 4 * D_HEADS)
N_GATES = 4 * D_HEADS
I_FWD, F_FWD, I_BWD, F_BWD = 0, D_HEADS, 2 * D_HEADS, 3 * D_HEADS


def _odd_weights(w_in):
    cq, ckv, kr, dq, dk, dv, do, dg = jnp.split(w_in, [int(c) for c in np.cumsum(ODD_SPLITS)[:-1]], axis=1)
    D = w_in.shape[0]
    zeros = lambda n: jnp.zeros((D, n), w_in.dtype)
    kr_slot = jnp.concatenate([zeros(C_NOPE), kr, zeros(C_SLOT - C_NOPE - C_ROPE)], axis=1)
    dg_slot = jnp.concatenate([dg, zeros(LANES - N_GATES)], axis=1)
    w = jnp.concatenate([cq, ckv, kr_slot, dq, dk, dv, do, dg_slot], axis=1)
    return w.astype(BF16), dg.T.astype(BF16)


def _inproj_odd_kernel(x_ref, g_ref, w_ref, wgt_ref,
                       cq_ref, ckv_ref, kr_ref, dqk_ref, dv_ref, do_ref, gc_ref, gt_ref):
    h = _rms(x_ref[...], g_ref[...]).astype(BF16)
    off = 0
    for ref, width in ((cq_ref, C_Q_LORA), (ckv_ref, C_KV_LORA), (kr_ref, C_SLOT), (dqk_ref, 2 * D_W),
                       (dv_ref, D_W), (do_ref, D_W), (gc_ref, LANES)):
        ref[...] = _dot(h, w_ref[:, off:off + width]).astype(ref.dtype)
        off += width
    gt_ref[...] = _dot_nt(wgt_ref[...], h)


def _inproj_odd(x2, g, w, wgt, tm, T):
    D = g.shape[-1]
    row = lambda width: pl.BlockSpec((tm, width), lambda i: (i, 0))
    full = lambda a: pl.BlockSpec(a.shape, lambda i: (0, 0))
    widths = (C_Q_LORA, C_KV_LORA, C_SLOT, 2 * D_W, D_W, D_W, LANES)
    dtypes = (F32, F32, F32, F32, BF16, F32, F32)
    out_shape = tuple(jax.ShapeDtypeStruct((T, wd), dt) for wd, dt in zip(widths, dtypes))
    out_shape += (jax.ShapeDtypeStruct((N_GATES, T), F32),)
    return pl.pallas_call(
        _inproj_odd_kernel, out_shape=out_shape, grid=(T // tm,),
        in_specs=[_stream_spec(x2, D, tm), full(g), full(w), full(wgt)],
        out_specs=tuple(row(wd) for wd in widths) + (pl.BlockSpec((N_GATES, tm), lambda i: (0, i)),),
        compiler_params=_params(1), name="inproj_odd")(x2, g, w, wgt)


def _mla_prep_kernel(cq_ref, ckv_ref, kr_ref, nq_ref, nkv_ref, wq_ref, wk_ref, wv_ref, cos_ref, sin_ref,
                     q_ref, k_ref, vt_ref):
    lane = lax.broadcasted_iota(I32, (1, LANES), 1)
    first = lane < C_NOPE + C_ROPE // 2
    cos, sin = cos_ref[...], sin_ref[...]
    rope = lambda z: _rope_slab(z, cos, sin, C_ROPE // 2, first)
    q = _dot(_rms(cq_ref[...], nq_ref[...]).astype(BF16), wq_ref[...])
    hkv = _rms(ckv_ref[...], nkv_ref[...]).astype(BF16)
    kn = _dot(hkv, wk_ref[...])
    kr = rope(kr_ref[...])
    for hh in range(C_HEADS):
        slab = slice(hh * C_SLOT, (hh + 1) * C_SLOT)
        q_ref[:, slab] = (rope(q[:, slab]) * C_LOG2_SCALE).astype(BF16)
        k_ref[:, slab] = (kn[:, slab] + kr).astype(BF16)
    vt = _dot_nt(wv_ref[...], hkv)
    row = lax.broadcasted_iota(I32, vt.shape, 0)
    vt_ref[...] = jnp.where(row % C_VROWS < C_V, vt, 1.0).astype(BF16)


def _mla_prep(cq, ckv, kr, nq, nkv, w_uq, w_ukv, cos, sin, tm):
    T = cq.shape[0]
    pad_q = C_SLOT - C_NOPE - C_ROPE
    wq = jnp.pad(w_uq.reshape(C_Q_LORA, C_HEADS, C_NOPE + C_ROPE), ((0, 0), (0, 0), (0, pad_q)))
    wq = wq.reshape(C_Q_LORA, C_HEADS * C_SLOT).astype(BF16)
    wkv = w_ukv.reshape(C_KV_LORA, C_HEADS, C_NOPE + C_V)
    wk = jnp.pad(wkv[:, :, :C_NOPE], ((0, 0), (0, 0), (0, C_SLOT - C_NOPE)))
    wk = wk.reshape(C_KV_LORA, C_HEADS * C_SLOT).astype(BF16)
    wv = jnp.pad(wkv[:, :, C_NOPE:], ((0, 0), (0, 0), (0, C_VROWS - C_V)))
    wv = wv.reshape(C_KV_LORA, C_HEADS * C_VROWS).T.astype(BF16)
    nq, nkv = nq.reshape(1, -1), nkv.reshape(1, -1)
    row = lambda width: pl.BlockSpec((tm, width), lambda i: (i, 0))
    full = lambda a: pl.BlockSpec(a.shape, lambda i: (0, 0))
    slots = jax.ShapeDtypeStruct((T, C_HEADS * C_SLOT), BF16)
    vt_shape = jax.ShapeDtypeStruct((T // tm, C_HEADS * C_VROWS, tm), BF16)
    return pl.pallas_call(
        _mla_prep_kernel, out_shape=(slots, slots, vt_shape), grid=(T // tm,),
        in_specs=[row(C_Q_LORA), row(C_KV_LORA), row(C_SLOT), full(nq), full(nkv), full(wq), full(wk), full(wv),
                  row(LANES), row(LANES)],
        out_specs=(row(C_HEADS * C_SLOT), row(C_HEADS * C_SLOT),
                   pl.BlockSpec((None, C_HEADS * C_VROWS, tm), lambda i: (i, 0, 0))),
        compiler_params=_params(1), name="mla_prep")(cq, ckv, kr, nq, nkv, wq, wk, wv, cos, sin)


def _mla_attn_kernel(q_ref, k_ref, vt_ref, o_ref):
    tq = q_ref.shape[0]
    n_chunks, _, key_chunk = vt_ref.shape
    slabs = [slice(hh * C_SLOT, (hh + 1) * C_SLOT) for hh in range(C_PAIR)]
    vrows = [slice(hh * C_VROWS, (hh + 1) * C_VROWS) for hh in range(C_PAIR)]
    qs = [q_ref[:, slab] for slab in slabs]

    def scores(hh, c):
        st = _dot_nt(k_ref[c * key_chunk:(c + 1) * key_chunk, slabs[hh]], qs[hh])
        return st, jnp.max(st, axis=0, keepdims=True)

    m = [jnp.full((1, tq), NEG, F32) for _ in range(C_PAIR)]
    acc = [jnp.zeros((C_VROWS, tq), F32) for _ in range(C_PAIR)]
    st = [scores(hh, 0) for hh in range(C_PAIR)]
    for c in range(n_chunks):
        for hh in range(C_PAIR):
            st_next = scores(hh, c + 1) if c + 1 < n_chunks else None
            m_new = jnp.maximum(m[hh], st[hh][1])
            pt = jnp.exp2((st[hh][0] - m_new).astype(BF16))
            acc[hh] = jnp.exp2(m[hh] - m_new) * acc[hh] + _dot(vt_ref[c, vrows[hh], :], pt)
            m[hh], st[hh] = m_new, st_next
    for hh in range(C_PAIR):
        ot = acc[hh][:C_V] / acc[hh][C_V:C_V + 1]
        o = jnp.concatenate([ot, jnp.zeros((LANES - C_V, tq), F32)], axis=0).T
        o_ref[:, hh * C_V:(hh + 1) * C_V] = o[:, :C_V].astype(BF16)


def _mla_attn(q, k, vt, B, S, tq):
    T = B * S
    nq = S // tq
    key_chunk = vt.shape[-1]
    vt = vt.reshape(B, S // key_chunk, C_HEADS * C_VROWS, key_chunk)
    return pl.pallas_call(
        _mla_attn_kernel,
        out_shape=jax.ShapeDtypeStruct((T, C_HEADS * C_V), BF16), grid=(B, C_HEADS // C_PAIR, nq),
        in_specs=[pl.BlockSpec((tq, C_PAIR * C_SLOT), lambda b, p, i: (b * nq + i, p)),
                  pl.BlockSpec((S, C_PAIR * C_SLOT), lambda b, p, i: (b, p)),
                  pl.BlockSpec((None, S // key_chunk, C_PAIR * C_VROWS, key_chunk), lambda b, p, i: (b, 0, p, 0))],
        out_specs=pl.BlockSpec((tq, C_PAIR * C_V), lambda b, p, i: (b * nq + i, p)),
        compiler_params=_params(3), name="mla_attn")(q, k, vt)


HALO = 8


def _conv_kernel(xp_ref, x_ref, xn_ref, w_ref, o_ref, *, n_tiles):
    j = pl.program_id(1)
    tc = x_ref.shape[0]
    prev = jnp.where(j > 0, xp_ref[...], 0.0)
    nxt = jnp.where(j < n_tiles - 1, xn_ref[...], 0.0)
    ext = jnp.concatenate([prev, x_ref[...], nxt], axis=0)
    rows = tc + 2 * HALO
    y = jnp.zeros(x_ref.shape, F32)
    for w in range(D_CONV):
        first = HALO - D_CONV // 2 + w
        y = y + pltpu.roll(ext, (rows - first) % rows, 0)[:tc] * w_ref[w:w + 1, :]
    y = y * jax.nn.sigmoid(y)
    o_ref[:, :D_W] = y[:, :D_W]
    o_ref[:, D_W:] = y[:, D_W:] * (D_HEAD_DIM ** -0.5)


def _conv_prep(dqk, conv_w, B, S, tc):
    T, C = dqk.shape
    n_tiles = S // tc
    per, last = tc // HALO, T // HALO - 1
    cur = lambda b, j: (b * n_tiles + j, 0)
    return pl.pallas_call(
        functools.partial(_conv_kernel, n_tiles=n_tiles),
        out_shape=jax.ShapeDtypeStruct((T, C), F32), grid=(B, n_tiles),
        in_specs=[pl.BlockSpec((HALO, C), lambda b, j: (jnp.maximum((b * n_tiles + j) * per - 1, 0), 0)),
                  pl.BlockSpec((tc, C), cur),
                  pl.BlockSpec((HALO, C), lambda b, j: (jnp.minimum((b * n_tiles + j + 1) * per, last), 0)),
                  pl.BlockSpec(conv_w.shape, lambda b, j: (0, 0))],
        out_specs=pl.BlockSpec((tc, C), cur),
        compiler_params=_params(2), name="conv_prep")(dqk, dqk, dqk, conv_w)


def _log_sigmoid(x):
    return -(jnp.maximum(-x, 0.0) + jnp.log1p(jnp.exp(-jnp.abs(x))))


def _tri(lower):
    a = lax.broadcasted_iota(I32, (BLOCK, BLOCK), 0)
    b = lax.broadcasted_iota(I32, (BLOCK, BLOCK), 1)
    return ((b <= a) if lower else (b >= a)).astype(BF16)


def _mlstm_update(s_ref, m_ref, h, g, a, k, v_ones):
    a_max = jnp.max(a, axis=0, keepdims=True)
    w = jnp.exp(a - a_max)
    upd = _dot_tn(v_ones, (k * w).astype(BF16))
    m = m_ref[h:h + 1, 0:1]
    m_new = jnp.maximum(g + m, a_max)
    rows = slice(h * 2 * D_HEAD_DIM, (h + 1) * 2 * D_HEAD_DIM)
    s_ref[rows, :] = jnp.exp(g + m - m_new) * s_ref[rows, :] + jnp.exp(a_max - m_new) * upd
    m_ref[h:h + 1, :] = jnp.broadcast_to(m_new, (1, LANES))


def _mlstm_bwd_state_kernel(bias_c_ref, qk_ref, v_ref, gc_ref, s_out_ref, m_out_ref, s_ref, m_ref):
    @pl.when(pl.program_id(1) == 0)
    def _():
        s_ref[...] = jnp.zeros_like(s_ref)
        m_ref[...] = jnp.zeros_like(m_ref)

    s_out_ref[...] = s_ref[...]
    m_out_ref[...] = m_ref[...]
    gc = gc_ref[...] + bias_c_ref[...]
    suffix = sum(_dot(_tri(False), p) for p in _split3(_log_sigmoid(gc)))
    ones = jnp.ones((BLOCK, D_HEAD_DIM), BF16)
    for h in range(D_HEADS):
        hs = slice(h * D_HEAD_DIM, (h + 1) * D_HEAD_DIM)
        sb = suffix[:, F_BWD + h:F_BWD + h + 1]
        g = sb[0:1, :]
        a = g - sb + gc[:, I_BWD + h:I_BWD + h + 1]
        k = qk_ref[:, D_W + h * D_HEAD_DIM:D_W + (h + 1) * D_HEAD_DIM]
        _mlstm_update(s_ref, m_ref, h, g, a, k, jnp.concatenate([v_ref[:, hs], ones], axis=1))


def _mlstm_bwd_state(bias_c, qk, dv, gc, B, S):
    N = S // BLOCK
    rev = lambda b, n: (b * N + N - 1 - n, 0)
    out_shape = (jax.ShapeDtypeStruct((B, N, 2 * D_W, D_HEAD_DIM), F32), jax.ShapeDtypeStruct((B, N, 8, LANES), F32))
    return pl.pallas_call(
        _mlstm_bwd_state_kernel, out_shape=out_shape, grid=(B, N),
        in_specs=[pl.BlockSpec(bias_c.shape, lambda b, n: (0, 0)), pl.BlockSpec((BLOCK, 2 * D_W), rev),
                  pl.BlockSpec((BLOCK, D_W), rev), pl.BlockSpec((BLOCK, LANES), rev)],
        out_specs=(pl.BlockSpec((None, None, 2 * D_W, D_HEAD_DIM), lambda b, n: (b, N - 1 - n, 0, 0)),
                   pl.BlockSpec((None, None, 8, LANES), lambda b, n: (b, N - 1 - n, 0, 0))),
        scratch_shapes=[pltpu.VMEM((2 * D_W, D_HEAD_DIM), F32), pltpu.VMEM((8, LANES), F32)],
        compiler_params=_params(2), name="mlstm_bwd_state")(bias_c, qk, dv, gc)


def _mlstm_main_kernel(bias_c_ref, bias_r_ref, qk_ref, v_ref, og_ref, gc_ref, gt_ref, sb_ref, mb_ref,
                       o_ref, s_ref, m_ref):
    @pl.when(pl.program_id(1) == 0)
    def _():
        s_ref[...] = jnp.zeros_like(s_ref)
        m_ref[...] = jnp.zeros_like(m_ref)

    gc = gc_ref[...] + bias_c_ref[...]
    gr = gt_ref[...] + bias_r_ref[...]
    lower, upper = _tri(True), _tri(False)
    lfc, lfr = _split3(_log_sigmoid(gc)), _split3(_log_sigmoid(gr))
    pre_c = sum(_dot(lower, p) for p in lfc)
    suf_c = sum(_dot(upper, p) for p in lfc)
    pre_r = sum(_dot(p, upper) for p in lfr)
    suf_r = sum(_dot(p, lower) for p in lfr)
    src = lax.broadcasted_iota(I32, (BLOCK, BLOCK), 0)
    qry = lax.broadcasted_iota(I32, (BLOCK, BLOCK), 1)
    ones = jnp.ones((BLOCK, D_HEAD_DIM), BF16)
    og = og_ref[...]
    for h in range(D_HEADS):
        hs = slice(h * D_HEAD_DIM, (h + 1) * D_HEAD_DIM)
        rows = slice(h * 2 * D_HEAD_DIM, (h + 1) * 2 * D_HEAD_DIM)
        k = qk_ref[:, D_W + h * D_HEAD_DIM:D_W + (h + 1) * D_HEAD_DIM]
        qb, v = qk_ref[:, hs].astype(BF16), v_ref[:, hs]
        kq = _dot_nt(k.astype(BF16), qb)

        def direction(b_row, c_col, mask, state_t, m_prev):
            logd = jnp.where(mask, b_row + c_col, NEG)
            log_inter = b_row + m_prev
            m_t = jnp.maximum(jnp.max(logd, axis=0, keepdims=True), log_inter)
            s = kq * jnp.exp(logd - m_t)
            inter_w = jnp.exp(log_inter - m_t)
            qs = _dot_nt(state_t.astype(BF16), qb)
            num = _dot_tn(v, s.astype(BF16)) + inter_w * qs[:D_HEAD_DIM, :]
            den = jnp.sum(s, axis=0, keepdims=True) + inter_w * qs[D_HEAD_DIM:D_HEAD_DIM + 1, :]
            return num / jnp.maximum(jnp.abs(den), jnp.exp(-m_t))

        f, bk = F_FWD + h, F_BWD + h
        h_f = direction(pre_r[f:f + 1, :], gc[:, I_FWD + h:I_FWD + h + 1] - pre_c[:, f:f + 1], src <= qry,
                        s_ref[rows, :], m_ref[h:h + 1, 0:1])
        h_b = direction(suf_r[bk:bk + 1, :], gc[:, I_BWD + h:I_BWD + h + 1] - suf_c[:, bk:bk + 1], src > qry,
                        sb_ref[rows, :], mb_ref[h:h + 1, 0:1])
        o_ref[:, hs] = (jax.nn.sigmoid(og[:, hs]) * (h_f + h_b).T).astype(BF16)
        b_col = pre_c[:, f:f + 1]
        g = b_col[BLOCK - 1:BLOCK, :]
        a = g - b_col + gc[:, I_FWD + h:I_FWD + h + 1]
        _mlstm_update(s_ref, m_ref, h, g, a, k, jnp.concatenate([v, ones], axis=1))


def _mlstm_main(bias_c, bias_r, qk, dv, og, gc, gt, sb, mb, B, S):
    N = S // BLOCK
    T = B * S
    cur = lambda b, n: (b * N + n, 0)
    return pl.pallas_call(
        _mlstm_main_kernel, out_shape=jax.ShapeDtypeStruct((T, D_W), BF16), grid=(B, N),
        in_specs=[pl.BlockSpec(bias_c.shape, lambda b, n: (0, 0)), pl.BlockSpec(bias_r.shape, lambda b, n: (0, 0)),
                  pl.BlockSpec((BLOCK, 2 * D_W), cur), pl.BlockSpec((BLOCK, D_W), cur), pl.BlockSpec((BLOCK, D_W), cur),
                  pl.BlockSpec((BLOCK, LANES), cur), pl.BlockSpec((N_GATES, BLOCK), lambda b, n: (0, b * N + n)),
                  pl.BlockSpec((None, None, 2 * D_W, D_HEAD_DIM), lambda b, n: (b, n, 0, 0)),
                  pl.BlockSpec((None, None, 8, LANES), lambda b, n: (b, n, 0, 0))],
        out_specs=pl.BlockSpec((BLOCK, D_W), cur),
        scratch_shapes=[pltpu.VMEM((2 * D_W, D_HEAD_DIM), F32), pltpu.VMEM((8, LANES), F32)],
        compiler_params=_params(2), name="mlstm_main")(bias_c, bias_r, qk, dv, og, gc, gt, sb, mb)


def odd_mixer_parts(x2, positions, g_mix, w_in, norm_q, norm_kv, w_uq, w_ukv, conv_w, gate_bias, B, S,
                    tm=512, tq=256, key_chunk=1024):
    w, wgt = _odd_weights(w_in)
    cq, ckv, kr, dqk, dv, og, gc, gt = _inproj_odd(x2, g_mix.reshape(1, -1), w, wgt, tm, B * S)
    cos, sin = _rope_tables(positions, MLA_THETA, C_SLOT, C_NOPE, C_ROPE)
    q, k, vt = _mla_prep(cq, ckv, kr, norm_q, norm_kv, w_uq, w_ukv, cos, sin, min(key_chunk, S))
    o_c = _mla_attn(q, k, vt, B, S, min(tq, S))
    qk = _conv_prep(dqk, conv_w.astype(F32), B, S, min(tm, S))
    bias = gate_bias.astype(F32).reshape(1, N_GATES)
    bias_c = jnp.pad(bias, ((0, 0), (0, LANES - N_GATES)))
    bias_r = bias.reshape(N_GATES, 1)
    sb, mb = _mlstm_bwd_state(bias_c, qk, dv, gc, B, S)
    o_d = _mlstm_main(bias_c, bias_r, qk, dv, og, gc, gt, sb, mb, B, S)
    return o_c, o_d


def kernel(x, positions, norm_mix, norm_ffn, norm_final, ev_w_in, ev_w_out, attn_sink, ret_decay_logit,
           od_w_in, od_w_out, mla_norm_q, mla_norm_kv, mla_w_uq, mla_w_ukv, mlstm_conv, mlstm_gate_bias,
           moe_router, moe_w_gate, moe_w_up, moe_w_down):
    B, S, D = x.shape
    tm = min(512, S)
    x2 = x.reshape(B * S, D)
    for layer in range(norm_mix.shape[0]):
        j = layer // 2
        if layer % 2 == 0:
            o1, o2 = even_mixer_parts(x2, positions, norm_mix[layer], ev_w_in[j], attn_sink[j],
                                      ret_decay_logit[j], B, S, tm)
            w_out = ev_w_out[j]
        else:
            o1, o2 = odd_mixer_parts(x2, positions, norm_mix[layer], od_w_in[j], mla_norm_q[j], mla_norm_kv[j],
                                     mla_w_uq[j], mla_w_ukv[j], mlstm_conv[j], mlstm_gate_bias[j], B, S, tm)
            w_out = od_w_out[j]
        x2 = moe_layer(o1, o2, w_out, x2, norm_ffn[layer], moe_router[layer],
                       _expert_weights_bf16(layer, moe_w_gate, moe_w_up, moe_w_down), B, S, tm)
    return _final_norm(x2, norm_final, tm, B * S).reshape(B, S, D)
```

```python
import functools

import jax
import jax.numpy as jnp
import numpy as np
from jax import lax
from jax.experimental import pallas as pl
from jax.experimental.pallas import tpu as pltpu

F32 = jnp.float32
BF16 = jnp.bfloat16
I32 = jnp.int32

LANES = 128
BLOCK = 128
RMS_EPS = 1e-6
GN_EPS = 1e-5
NEG = -1e30
VMEM_LIMIT = 56 * 1024 * 1024

A_HEADS, A_KV_HEADS, A_HEAD_DIM = 8, 2, 64
ROPE_THETA = 500000.0
ROPE_DIM = A_HEAD_DIM // 4
B_HEADS, B_HEAD_DIM = 8, 64
RET_THETA = 10000.0
C_HEADS, C_NOPE, C_ROPE, C_V = 8, 64, 32, 64
C_Q_LORA, C_KV_LORA = 512, 256
MLA_THETA = 10000.0
D_HEADS, D_HEAD_DIM, D_CONV = 4, 128, 5
N_EXPERTS = 16
EC_CAPACITY_FACTOR = 2

A_Q_W = A_HEADS * A_HEAD_DIM
A_KV_W = A_KV_HEADS * A_HEAD_DIM
B_W = B_HEADS * B_HEAD_DIM
D_W = D_HEADS * D_HEAD_DIM
C_SLOT = 128
C_PAIR = 4
BF16_ROWS = 16
C_VROWS = C_V + BF16_ROWS
C_LOG2_SCALE = float((C_NOPE + C_ROPE) ** -0.5 * np.log2(np.e))


def _params(n_axes, vmem=VMEM_LIMIT):
    return pltpu.CompilerParams(dimension_semantics=("arbitrary",) * n_axes,
                                vmem_limit_bytes=vmem)


def _rms(x, g):
    return x * lax.rsqrt(jnp.mean(x * x, axis=-1, keepdims=True) + RMS_EPS) * g


def _dot(a, b):
    return jnp.dot(a, b, preferred_element_type=F32)


def _dot_nt(a, b):
    return lax.dot_general(a, b, (((1,), (1,)), ((), ())), preferred_element_type=F32)


def _dot_tn(a, b):
    return lax.dot_general(a, b, (((0,), (0,)), ((), ())), preferred_element_type=F32)


def _rope_slab(z, cos, sin, half, first_half):
    partner = jnp.where(first_half, pltpu.roll(z, LANES - half, 1), pltpu.roll(z, half, 1))
    return z * cos + partner * sin


def _rope_tables(positions, theta, head_dim, rot_start, rot_dim):
    half = rot_dim // 2
    assert head_dim % half == 0 and rot_start % half == 0
    inv_freq = theta ** (-jnp.arange(half, dtype=F32) * 2.0 / rot_dim)
    d = np.arange(LANES) % head_dim - rot_start
    rot = (d >= 0) & (d < rot_dim)
    sign = np.where(d < half, -1.0, 1.0).astype(np.float32)
    ang = positions.astype(F32).reshape(-1, 1) * inv_freq
    ang = jnp.tile(ang, (1, LANES // half))
    cos = jnp.where(rot[None, :], jnp.cos(ang), 1.0)
    sin = jnp.where(rot[None, :], jnp.sin(ang) * sign[None, :], 0.0)
    return cos, sin


def _inproj_even_kernel(x_ref, g_ref, w_ref, ca_ref, sa_ref, cb_ref, sb_ref,
                        aq_ref, akv_ref, bq_ref, bk_ref, bv_ref, bg_ref):
    h = _rms(x_ref[...], g_ref[...]).astype(BF16)
    lane = lax.broadcasted_iota(I32, (1, LANES), 1)
    first_a = (lane % A_HEAD_DIM) < (ROPE_DIM // 2)
    first_b = (lane % B_HEAD_DIM) < (B_HEAD_DIM // 2)
    ca, sa, cb, sb = ca_ref[...], sa_ref[...], cb_ref[...], sb_ref[...]

    def rope_a(z):
        return _rope_slab(z, ca, sa, ROPE_DIM // 2, first_a)

    def rope_b(z):
        return _rope_slab(z, cb, sb, B_HEAD_DIM // 2, first_b)

    off = 0
    z = _dot(h, w_ref[:, off:off + A_Q_W])
    for s in range(A_Q_W // LANES):
        aq_ref[:, s * LANES:(s + 1) * LANES] = rope_a(z[:, s * LANES:(s + 1) * LANES]).astype(BF16)
    off += A_Q_W
    z = _dot(h, w_ref[:, off:off + 2 * A_KV_W])
    akv_ref[:, :A_KV_W] = rope_a(z[:, :A_KV_W]).astype(BF16)
    akv_ref[:, A_KV_W:] = z[:, A_KV_W:].astype(BF16)
    off += 2 * A_KV_W
    z = _dot(h, w_ref[:, off:off + B_W])
    for s in range(B_W // LANES):
        bq_ref[:, s * LANES:(s + 1) * LANES] = rope_b(z[:, s * LANES:(s + 1) * LANES])
    off += B_W
    z = _dot(h, w_ref[:, off:off + B_W])
    for s in range(B_W // LANES):
        bk_ref[:, s * LANES:(s + 1) * LANES] = rope_b(z[:, s * LANES:(s + 1) * LANES]) * (B_HEAD_DIM ** -0.5)
    off += B_W
    bv_ref[...] = _dot(h, w_ref[:, off:off + B_W]).astype(BF16)
    off += B_W
    bg_ref[...] = _dot(h, w_ref[:, off:off + B_W])


def _stream_spec(xs, D, tm):
    col = xs.shape[1] // D - 1
    return pl.BlockSpec((tm, D), lambda i: (i, col))


def _inproj_even(x2, g, w, tabs, tm):
    T, D = tabs[0].shape[0], g.shape[-1]
    ncol = w.shape[1]
    row = lambda width: pl.BlockSpec((tm, width), lambda i: (i, 0))
    full = lambda a: pl.BlockSpec(a.shape, lambda i: (0, 0))
    out_shape = (jax.ShapeDtypeStruct((T, A_Q_W), BF16), jax.ShapeDtypeStruct((T, 2 * A_KV_W), BF16),
                 jax.ShapeDtypeStruct((T, B_W), F32), jax.ShapeDtypeStruct((T, B_W), F32),
                 jax.ShapeDtypeStruct((T, B_W), BF16), jax.ShapeDtypeStruct((T, B_W), F32))
    return pl.pallas_call(
        _inproj_even_kernel, out_shape=out_shape, grid=(T // tm,),
        in_specs=[_stream_spec(x2, D, tm), full(g), pl.BlockSpec((D, ncol), lambda i: (0, 0))] + [row(LANES)] * 4,
        out_specs=(row(A_Q_W), row(2 * A_KV_W), row(B_W), row(B_W), row(B_W), row(B_W)),
        compiler_params=_params(1), name="inproj_even")(x2, g, w, *tabs)


def _win_attn_kernel(sink_ref, q_ref, kvp_ref, kvc_ref, kvn_ref, o_ref, *, n_blocks):
    n = pl.program_id(1)
    group = A_HEADS // A_KV_HEADS
    cols = group * BLOCK
    kj = lax.broadcasted_iota(I32, (BLOCK, cols), 0)
    qi = lax.broadcasted_iota(I32, (BLOCK, cols), 1) % BLOCK
    ok_prev = (kj >= qi) & (n > 0)
    ok_next = (kj <= qi) & (n < n_blocks - 1)
    scale = A_HEAD_DIM ** -0.5
    for s_i, g in [(s_i, g) for g in range(A_KV_HEADS) for s_i in range(q_ref.shape[0])]:
        q, kvp, kvc, kvn = q_ref[s_i], kvp_ref.at[s_i], kvc_ref.at[s_i], kvn_ref.at[s_i]
        ks = slice(g * A_HEAD_DIM, (g + 1) * A_HEAD_DIM)
        vs = slice(A_KV_W + g * A_HEAD_DIM, A_KV_W + (g + 1) * A_HEAD_DIM)
        qg = jnp.concatenate([q[:, (g * group + i) * A_HEAD_DIM:(g * group + i + 1) * A_HEAD_DIM]
                              for i in range(group)], axis=0)
        sink = jnp.concatenate([jnp.full((1, BLOCK), sink_ref[g * group + i], F32)
                                for i in range(group)], axis=1)
        sp = jnp.where(ok_prev, _dot_nt(kvp[:, ks], qg) * scale, NEG)
        sc = _dot_nt(kvc[:, ks], qg) * scale
        sn = jnp.where(ok_next, _dot_nt(kvn[:, ks], qg) * scale, NEG)
        m = jnp.maximum(jnp.maximum(jnp.max(sp, 0, keepdims=True), jnp.max(sc, 0, keepdims=True)),
                        jnp.maximum(jnp.max(sn, 0, keepdims=True), sink))
        pp, pc, pn = jnp.exp(sp - m), jnp.exp(sc - m), jnp.exp(sn - m)
        den = (jnp.sum(pp, 0, keepdims=True) + jnp.sum(pc, 0, keepdims=True)
               + jnp.sum(pn, 0, keepdims=True) + jnp.exp(sink - m))
        ot = (_dot_tn(kvp[:, vs], pp.astype(BF16)) + _dot_tn(kvc[:, vs], pc.astype(BF16))
              + _dot_tn(kvn[:, vs], pn.astype(BF16))) / den
        o = ot.T
        for i in range(group):
            hh = g * group + i
            o_ref[s_i, :, hh * A_HEAD_DIM:(hh + 1) * A_HEAD_DIM] = o[i * BLOCK:(i + 1) * BLOCK].astype(BF16)


def _win_attn(aq, akv, sink, B, S):
    N = S // BLOCK
    nb = SEQS_PER_STEP if B % SEQS_PER_STEP == 0 else 1
    per_seq = lambda a: a.reshape(B, S, a.shape[-1])
    kv = lambda shift: pl.BlockSpec(
        (nb, BLOCK, 2 * A_KV_W), lambda b, n: (b, jnp.clip(n + shift, 0, N - 1), 0))
    qo = lambda: pl.BlockSpec((nb, BLOCK, A_Q_W), lambda b, n: (b, n, 0))
    out = pl.pallas_call(
        functools.partial(_win_attn_kernel, n_blocks=N),
        out_shape=jax.ShapeDtypeStruct((B, S, A_Q_W), BF16), grid=(B // nb, N),
        in_specs=[pl.BlockSpec(memory_space=pltpu.SMEM), qo(), kv(-1), kv(0), kv(1)],
        out_specs=qo(),
        compiler_params=_params(2), name="win_attn")(sink, per_seq(aq), per_seq(akv), per_seq(akv), per_seq(akv))
    return out.reshape(B * S, A_Q_W)


def _ret_tables(lg_ref, lgl_ref, d_ref, xif_ref, xib_ref, zf_ref, zb_ref):
    j = lax.broadcasted_iota(I32, (BLOCK, BLOCK), 0)
    i = lax.broadcasted_iota(I32, (BLOCK, BLOCK), 1)
    rel = (i - j).astype(F32)
    for h in range(B_HEADS):
        d_ref[h] = jnp.where(i >= j, jnp.exp(rel * lg_ref[0, h]), jnp.exp(-rel * lg_ref[1, h]))
    t = lax.broadcasted_iota(I32, (BLOCK, B_W), 0).astype(F32)
    lgf, lgb = lgl_ref[0:1, :], lgl_ref[1:2, :]
    xif_ref[...] = jnp.exp((t + 1.0) * lgf)
    zf_ref[...] = jnp.exp((BLOCK - 1.0 - t) * lgf)
    xib_ref[...] = jnp.exp((BLOCK - t) * lgb)
    zb_ref[...] = jnp.exp(t * lgb)


def _ret_bwd_state_kernel(lgl_ref, k_ref, v_ref, r_out_ref, r_ref):
    n = pl.program_id(1)

    @pl.when(n == 0)
    def _():
        r_ref[...] = jnp.zeros_like(r_ref)

    r_out_ref[...] = r_ref[...]
    lgb = lgl_ref[1:2, :]
    t = lax.broadcasted_iota(I32, (BLOCK, B_W), 0).astype(F32)
    zeta = jnp.exp(t * lgb)
    cd = jnp.exp(BLOCK * lgb)
    for i in range(k_ref.shape[0]):
        _ret_state_update(r_ref.at[i], cd, v_ref[i], (k_ref[i] * zeta).astype(BF16))


def _ret_state_update(r_ref, chunk_decay, v, kz):
    upd = _dot_tn(v, kz)
    for h in range(B_HEADS):
        hs = slice(h * B_HEAD_DIM, (h + 1) * B_HEAD_DIM)
        r_ref[hs, :] = chunk_decay[:, hs] * r_ref[hs, :] + upd[hs, hs]


def _ret_bwd_state(lgl, bk, bv, B, S):
    N = S // BLOCK
    nb = SEQS_PER_STEP if B % SEQS_PER_STEP == 0 else 1
    per_seq = lambda a: a.reshape(B, S, a.shape[-1])
    blk = lambda: pl.BlockSpec((nb, BLOCK, B_W), lambda b, n: (b, N - 1 - n, 0))
    return pl.pallas_call(
        _ret_bwd_state_kernel,
        out_shape=jax.ShapeDtypeStruct((B, N, B_W, B_HEAD_DIM), F32), grid=(B // nb, N),
        in_specs=[pl.BlockSpec(lgl.shape, lambda b, n: (0, 0)), blk(), blk()],
        out_specs=pl.BlockSpec((nb, None, B_W, B_HEAD_DIM), lambda b, n: (b, N - 1 - n, 0, 0)),
        scratch_shapes=[pltpu.VMEM((nb, B_W, B_HEAD_DIM), F32)],
        compiler_params=_params(2), name="ret_bwd_state")(lgl, per_seq(bk), per_seq(bv))


def _ret_main_kernel(lg_ref, lgl_ref, q_ref, k_ref, v_ref, g_ref, rb_ref, o_ref,
                     rf_ref, d_ref, xif_ref, xib_ref, zf_ref, zb_ref):
    b, n = pl.program_id(0), pl.program_id(1)

    @pl.when((b == 0) & (n == 0))
    def _():
        _ret_tables(lg_ref, lgl_ref, d_ref, xif_ref, xib_ref, zf_ref, zb_ref)

    @pl.when(n == 0)
    def _():
        rf_ref[...] = jnp.zeros_like(rf_ref)

    n_seq = q_ref.shape[0]
    cdf = jnp.exp(BLOCK * lgl_ref[0:1, :])
    q, k, v, gate = ([r[i] for i in range(n_seq)] for r in (q_ref, k_ref, v_ref, g_ref))
    qb = [x.astype(BF16) for x in q]
    kb = [x.astype(BF16) for x in k]
    qxf = [(x * xif_ref[...]).astype(BF16) for x in q]
    qxb = [(x * xib_ref[...]).astype(BF16) for x in q]
    kzf = [(x * zf_ref[...]).astype(BF16) for x in k]
    ynt = [[None] * B_HEADS for _ in range(n_seq)]
    for h in range(B_HEADS):
        hs = slice(h * B_HEAD_DIM, (h + 1) * B_HEAD_DIM)
        for i in range(n_seq):
            st = _dot_nt(kb[i][:, hs], qb[i][:, hs]) * d_ref[h]
            states = jnp.concatenate([rf_ref[i, hs, :], rb_ref[i, hs, :]], axis=1).astype(BF16)
            queries = jnp.concatenate([qxf[i][:, hs], qxb[i][:, hs]], axis=1)
            yt = _dot_tn(v[i][:, hs], st.astype(BF16)) + _dot_nt(states, queries)
            yc = yt - jnp.mean(yt, axis=0, keepdims=True)
            ynt[i][h] = yc * lax.rsqrt(jnp.mean(yc * yc, axis=0, keepdims=True) + GN_EPS)
    for i in range(n_seq):
        yn = jnp.concatenate(ynt[i], axis=0).T
        o_ref[i] = (gate[i] * jax.nn.sigmoid(gate[i]) * yn).astype(BF16)
        _ret_state_update(rf_ref.at[i], cdf, v[i], kzf[i])


SEQS_PER_STEP = 2


def _ret_main(lg, lgl, bq, bk, bv, bg, rb, B, S):
    N = S // BLOCK
    nb = SEQS_PER_STEP if B % SEQS_PER_STEP == 0 else 1
    per_seq = lambda a: a.reshape(B, S, a.shape[-1])
    blk = lambda: pl.BlockSpec((nb, BLOCK, B_W), lambda b, n: (b, n, 0))
    tab = lambda: pltpu.VMEM((BLOCK, B_W), F32)
    out = pl.pallas_call(
        _ret_main_kernel, out_shape=jax.ShapeDtypeStruct((B, S, B_W), BF16), grid=(B // nb, N),
        in_specs=[pl.BlockSpec(memory_space=pltpu.SMEM), pl.BlockSpec(lgl.shape, lambda b, n: (0, 0)),
                  blk(), blk(), blk(), blk(),
                  pl.BlockSpec((nb, None, B_W, B_HEAD_DIM), lambda b, n: (b, n, 0, 0))],
        out_specs=blk(),
        scratch_shapes=[pltpu.VMEM((nb, B_W, B_HEAD_DIM), F32), pltpu.VMEM((B_HEADS, BLOCK, BLOCK), F32),
                        tab(), tab(), tab(), tab()],
        compiler_params=_params(2), name="ret_main")(lg, lgl, per_seq(bq), per_seq(bk), per_seq(bv), per_seq(bg), rb)
    return out.reshape(B * S, B_W)


def even_mixer_parts(x2, positions, g_mix, w_in, sink, decay_logit, B, S, tm=512):
    ca, sa = _rope_tables(positions, ROPE_THETA, A_HEAD_DIM, 0, ROPE_DIM)
    cb, sb = _rope_tables(positions, RET_THETA, B_HEAD_DIM, 0, B_HEAD_DIM)
    aq, akv, bq, bk, bv, bg = _inproj_even(x2, g_mix.reshape(1, -1), w_in.astype(BF16), (ca, sa, cb, sb), tm)
    o_a = _win_attn(aq, akv, sink.astype(F32), B, S)
    lg = jax.nn.log_sigmoid(decay_logit.astype(F32))
    lgl = jnp.repeat(lg, B_HEAD_DIM, axis=1)
    rb = _ret_bwd_state(lgl, bk, bv, B, S)
    o_b = _ret_main(lg, lgl, bq, bk, bv, bg, rb, B, S)
    return o_a, o_b


def _outproj_kernel(o1_ref, o2_ref, w_ref, x_ref, g_ref, wr_ref, hx_ref, aff_ref):
    half = o1_ref.shape[1]
    D = x_ref.shape[1]
    x1 = x_ref[...] + (_dot(o1_ref[...], w_ref[:half, :]) + _dot(o2_ref[...], w_ref[half:, :]))
    hn = _rms(x1, g_ref[...])
    hx_ref[:, :D] = hn
    hx_ref[:, D:] = x1
    logits = _dot_nt(wr_ref[...], hn.astype(BF16))
    e = jnp.exp(logits - jnp.max(logits, axis=0, keepdims=True))
    aff_ref[...] = e / jnp.sum(e, axis=0, keepdims=True)


def _outproj_router(o1, o2, w_out, x2, g_ffn, w_router, B, S, tm):
    T, D = o1.shape[0], g_ffn.shape[-1]
    E = w_router.shape[1]
    spare = EC_CAPACITY_FACTOR * S // E
    per = S // tm
    row = lambda width: pl.BlockSpec((tm, width), lambda i: (i, 0))
    full = lambda a: pl.BlockSpec(a.shape, lambda i: (0, 0))
    w = w_out.astype(BF16)
    g = g_ffn.reshape(1, D)
    wr = w_router.T.astype(BF16)
    return pl.pallas_call(
        _outproj_kernel,
        out_shape=(jax.ShapeDtypeStruct((T + spare, 2 * D), F32), jax.ShapeDtypeStruct((B, E, S), F32)),
        grid=(T // tm,),
        in_specs=[row(o1.shape[1]), row(o2.shape[1]), full(w), _stream_spec(x2, D, tm), full(g), full(wr)],
        out_specs=(row(2 * D), pl.BlockSpec((None, E, tm), lambda i: (i // per, 0, i % per))),
        compiler_params=_params(1), name="outproj_router")(o1, o2, w, x2, g, wr)


def _split3(x):
    x1 = x.astype(BF16)
    r = x - x1.astype(F32)
    x2 = r.astype(BF16)
    x3 = (r - x2.astype(F32)).astype(BF16)
    return x1, x2, x3


def _topk_kernel(aff_ref, idx_ref, gate_ref, thr_ref, *, cap):
    E, R, _ = aff_ref.shape
    bits = lax.bitcast_convert_type(aff_ref[...], I32)

    def count(mask):
        return jnp.sum(jnp.sum(mask.astype(F32), axis=2, keepdims=True), axis=1, keepdims=True)

    def bit_body(i, prefix):
        cand = prefix | jnp.left_shift(jnp.int32(1), 30 - i)
        return jnp.where(count(bits >= cand) >= cap, cand, prefix)

    thr = lax.fori_loop(0, 31, bit_body, jnp.zeros((E, 1, 1), I32))
    thr_ref[...] = jnp.broadcast_to(thr, thr_ref.shape)

    li = lax.broadcasted_iota(I32, (LANES, LANES), 0)
    lj = lax.broadcasted_iota(I32, (LANES, LANES), 1)
    tri = (li <= lj).astype(BF16)
    ri = lax.broadcasted_iota(I32, (R, R), 0)
    rj = lax.broadcasted_iota(I32, (R, R), 1)
    below = (rj < ri).astype(BF16)
    slot = lax.broadcasted_iota(I32, (1, cap), 1).astype(F32)
    tok = (lax.broadcasted_iota(I32, (R, LANES), 0) * LANES
           + lax.broadcasted_iota(I32, (R, LANES), 1)).astype(F32)

    def prefix_counts(m):
        within = _dot(m.astype(BF16), tri)
        total = jnp.broadcast_to(within[:, LANES - 1:LANES], (R, LANES))
        before = _dot(below, total.astype(BF16))
        return within, total, before

    def expert(e, carry):
        a = aff_ref[e]
        b = lax.bitcast_convert_type(a, I32)
        t = thr_ref[e]
        gt, eq = b > t, b == t
        n_gt = jnp.sum(jnp.sum(gt.astype(F32), axis=1, keepdims=True), axis=0, keepdims=True)
        eqf = eq.astype(F32)
        within, _, before = prefix_counts(eqf)
        sel = gt | (eq & (before + within - eqf < cap - n_gt))
        self_ = sel.astype(F32)
        within, total, before = prefix_counts(self_)
        rank = before + within
        first, count_r = before[:, 0:1], total[:, 0:1]
        owner = ((first <= slot) & (slot < first + count_r)).astype(BF16)

        pieces = [self_.astype(BF16)] + list(_split3(rank)[:2]) + list(_split3(tok)[:2]) + list(_split3(a))
        rows = _dot_tn(jnp.concatenate(pieces, axis=1), owner)
        part = lambda n: rows[n * LANES:(n + 1) * LANES]
        hit = (part(0) > 0.5) & (part(1) + part(2) == slot + 1.0)
        idx_ref[e] = jnp.sum(jnp.where(hit, part(3) + part(4), 0.0), axis=0, keepdims=True).astype(I32)
        gate_ref[e] = jnp.sum(jnp.where(hit, part(5) + part(6) + part(7), 0.0), axis=0, keepdims=True)
        return carry

    lax.fori_loop(0, E, expert, 0)


def _topk(aff, cap):
    B, E, S = aff.shape
    assert S < 2 ** 16
    R = S // LANES
    aff4 = aff.reshape(B, E, R, LANES)
    out = jax.ShapeDtypeStruct((B, E, 1, cap), I32), jax.ShapeDtypeStruct((B, E, 1, cap), F32)
    spec = pl.BlockSpec((None, E, 1, cap), lambda b: (b, 0, 0, 0))
    return pl.pallas_call(
        functools.partial(_topk_kernel, cap=cap), out_shape=out, grid=(B,),
        in_specs=[pl.BlockSpec((None, E, R, LANES), lambda b: (b, 0, 0, 0))],
        out_specs=(spec, spec), scratch_shapes=[pltpu.VMEM((E, 1, LANES), I32)],
        compiler_params=_params(1), name="topk")(aff4)


ROW_UNROLL = 8
M_BLOCKS = 2
SEM_IN, SEM_OUT = 0, 1


N_SLOTS = 3


def _moe_kernel(idx_prev_ref, idx_ref, idx_next_ref, gate_ref, wg_ref, wu_ref, wd_ref, hx_alias, hx_hbm,
                buf, sems, *, seq, cap, f_chunk, n_tokens):
    del hx_alias
    D = wg_ref.shape[0]
    n_batch = pl.num_programs(1)
    step = pl.program_id(0) * n_batch + pl.program_id(1)
    n_steps = pl.num_programs(0) * n_batch
    cur, nxt, prv = step % N_SLOTS, (step + 1) % N_SLOTS, (step + 2) % N_SLOTS
    res = pl.ds(D, D)

    def for_rows(fn):
        @pl.loop(0, cap // ROW_UNROLL)
        def _(g):
            for u in range(ROW_UNROLL):
                fn(g * ROW_UNROLL + u)

    def gather_row(t, s, j):
        pltpu.make_async_copy(hx_hbm.at[pl.ds(t, 1)], buf.at[s, pl.ds(j, 1)], sems.at[SEM_IN, s]).start()

    def write_row(t, s, j):
        pltpu.make_async_copy(buf.at[s, pl.ds(j, 1), res], hx_hbm.at[pl.ds(t, 1), res], sems.at[SEM_OUT, s]).start()

    def wait_gathers(s):
        pltpu.make_async_copy(hx_hbm.at[pl.ds(0, cap)], buf.at[s], sems.at[SEM_IN, s]).wait()

    def wait_writes(s):
        pltpu.make_async_copy(buf.at[s, :, res], hx_hbm.at[pl.ds(0, cap), res], sems.at[SEM_OUT, s]).wait()

    this_base = pl.program_id(1) * seq
    next_base = ((step + 1) % n_batch) * seq
    prev_base = ((step + n_batch - 1) % n_batch) * seq
    first = step == 0

    @pl.when(first)
    def _():
        for_rows(lambda j: gather_row(this_base + idx_ref[0, j], cur, j))
        buf[prv, :, D:] = jnp.zeros((cap, D), F32)

    @pl.when(step > 0)
    def _():
        wait_writes(nxt)

    wait_gathers(cur)
    xin = buf[cur, :, :D].astype(BF16)
    n_chunks = wg_ref.shape[1] // f_chunk
    n_rows = cap // M_BLOCKS
    ahead = cap // (n_chunks * M_BLOCKS)
    accs = [jnp.zeros((n_rows, D), F32) for _ in range(M_BLOCKS)]
    piece = 0
    for f in range(n_chunks):
        fs = slice(f * f_chunk, (f + 1) * f_chunk)
        for mb in range(M_BLOCKS):
            for j in range(piece * ahead, (piece + 1) * ahead):
                gather_row(next_base + idx_next_ref[0, j], nxt, j)
                write_row(jnp.where(first, n_tokens + j, prev_base + idx_prev_ref[0, j]), prv, j)
            piece += 1
            xs = xin[mb * n_rows:(mb + 1) * n_rows]
            g = _dot(xs, wg_ref[:, fs])
            hid = (g * jax.nn.sigmoid(g) * _dot(xs, wu_ref[:, fs])).astype(BF16)
            accs[mb] = accs[mb] + _dot(hid, wd_ref[fs, :])
    acc = jnp.concatenate(accs, axis=0)
    diag = (lax.broadcasted_iota(I32, (cap, cap), 0) == lax.broadcasted_iota(I32, (cap, cap), 1))
    gate_col = jnp.sum(jnp.where(diag, gate_ref[...], 0.0), axis=1, keepdims=True)
    buf[cur, :, D:] = buf[cur, :, D:] + acc * gate_col

    @pl.when(step == n_steps - 1)
    def _():
        wait_gathers(nxt)
        wait_writes(prv)
        for_rows(lambda j: write_row(this_base + idx_ref[0, j], cur, j))
        wait_writes(cur)


def _moe_ffn(hx, idx, gate, w_gate, w_up, w_down, B, S, f_chunk=512):
    E, D, F = w_gate.shape
    cap = idx.shape[-1]
    T = hx.shape[0] - cap
    f_chunk = min(f_chunk, F)
    assert B >= N_SLOTS and cap % ROW_UNROLL == 0 and cap % (M_BLOCKS * (F // f_chunk)) == 0
    wspec = lambda a: pl.BlockSpec((None,) + a.shape[1:], lambda e, b: (e, 0, 0), pipeline_mode=pl.Buffered(1))
    any_spec = pl.BlockSpec(memory_space=pl.ANY)

    def ids(shift):
        def index_map(e, b):
            s = jnp.clip(e * B + b + shift, 0, E * B - 1)
            return (s % B, s // B, 0, 0)
        return pl.BlockSpec((None, None, 1, cap), index_map, memory_space=pltpu.SMEM)

    return pl.pallas_call(
        functools.partial(_moe_kernel, seq=S, cap=cap, f_chunk=f_chunk, n_tokens=T),
        out_shape=jax.ShapeDtypeStruct(hx.shape, F32), grid=(E, B),
        in_specs=[ids(-1), ids(0), ids(1),
                  pl.BlockSpec((None, None, 1, cap), lambda e, b: (b, e, 0, 0)),
                  wspec(w_gate), wspec(w_up), wspec(w_down), any_spec],
        out_specs=any_spec,
        scratch_shapes=[pltpu.VMEM((N_SLOTS, cap, 2 * D), F32), pltpu.SemaphoreType.DMA((2, N_SLOTS))],
        input_output_aliases={7: 0},
        compiler_params=_params(2), name="moe_ffn")(idx, idx, idx, gate, w_gate, w_up, w_down, hx)


def _cast_kernel(*refs):
    n = len(refs) // 2
    for src, dst in zip(refs[:n], refs[n:]):
        dst[...] = src[...].astype(dst.dtype)


WEIGHT_SPLIT = 2


def _expert_weights_bf16(layer, w_gate, w_up, w_down):
    _, E, D, F = w_gate.shape
    fh = F // WEIGHT_SPLIT
    cols = lambda: pl.BlockSpec((None, None, D, fh), lambda e, h: (layer, e, 0, h))
    rows = lambda: pl.BlockSpec((None, None, fh, D), lambda e, h: (layer, e, h, 0))
    return pl.pallas_call(
        _cast_kernel,
        out_shape=(jax.ShapeDtypeStruct((E, D, F), BF16), jax.ShapeDtypeStruct((E, D, F), BF16),
                   jax.ShapeDtypeStruct((E, F, D), BF16)),
        grid=(E, WEIGHT_SPLIT), in_specs=[cols(), cols(), rows()],
        out_specs=(pl.BlockSpec((None, D, fh), lambda e, h: (e, 0, h)),
                   pl.BlockSpec((None, D, fh), lambda e, h: (e, 0, h)),
                   pl.BlockSpec((None, fh, D), lambda e, h: (e, h, 0))),
        compiler_params=_params(2), name="cast_weights")(w_gate, w_up, w_down)


def moe_layer(o1, o2, w_out, x2, g_ffn, w_router, expert_weights, B, S, tm=512):
    hx, aff = _outproj_router(o1, o2, w_out, x2, g_ffn, w_router, B, S, tm)
    cap = EC_CAPACITY_FACTOR * S // w_router.shape[1]
    idx, gate = _topk(aff, cap)
    return _moe_ffn(hx, idx, gate, *expert_weights, B, S)


def _final_norm_kernel(x_ref, g_ref, o_ref):
    o_ref[...] = _rms(x_ref[...], g_ref[...])


def _final_norm(x2, g, tm, T):
    D = g.shape[-1]
    row = pl.BlockSpec((tm, D), lambda i: (i, 0))
    return pl.pallas_call(
        _final_norm_kernel, out_shape=jax.ShapeDtypeStruct((T, D), F32), grid=(T // tm,),
        in_specs=[_stream_spec(x2, D, tm), pl.BlockSpec((1, D), lambda i: (0, 0))], out_specs=row,
        compiler_params=_params(1), name="final_norm")(x2, g.reshape(1, D))


ODD_SPLITS = (C_Q_LORA, C_KV_LORA, C_ROPE, D_W, D_W, D_W, D_W, 4 * D_HEADS)
N_GATES = 4 * D_HEADS
I_FWD, F_FWD, I_BWD, F_BWD = 0, D_HEADS, 2 * D_HEADS, 3 * D_HEADS


def _odd_weights(w_in):
    cq, ckv, kr, dq, dk, dv, do, dg = jnp.split(w_in, [int(c) for c in np.cumsum(ODD_SPLITS)[:-1]], axis=1)
    D = w_in.shape[0]
    zeros = lambda n: jnp.zeros((D, n), w_in.dtype)
    kr_slot = jnp.concatenate([zeros(C_NOPE), kr, zeros(C_SLOT - C_NOPE - C_ROPE)], axis=1)
    dg_slot = jnp.concatenate([dg, zeros(LANES - N_GATES)], axis=1)
    w = jnp.concatenate([cq, ckv, kr_slot, dq, dk, dv, do, dg_slot], axis=1)
    return w.astype(BF16), dg.T.astype(BF16)


def _inproj_odd_kernel(x_ref, g_ref, w_ref, wgt_ref,
                       cq_ref, ckv_ref, kr_ref, dqk_ref, dv_ref, do_ref, gc_ref, gt_ref):
    h = _rms(x_ref[...], g_ref[...]).astype(BF16)
    off = 0
    for ref, width in ((cq_ref, C_Q_LORA), (ckv_ref, C_KV_LORA), (kr_ref, C_SLOT), (dqk_ref, 2 * D_W),
                       (dv_ref, D_W), (do_ref, D_W), (gc_ref, LANES)):
        ref[...] = _dot(h, w_ref[:, off:off + width]).astype(ref.dtype)
        off += width
    gt_ref[...] = _dot_nt(wgt_ref[...], h)


def _inproj_odd(x2, g, w, wgt, tm, T):
    D = g.shape[-1]
    row = lambda width: pl.BlockSpec((tm, width), lambda i: (i, 0))
    full = lambda a: pl.BlockSpec(a.shape, lambda i: (0, 0))
    widths = (C_Q_LORA, C_KV_LORA, C_SLOT, 2 * D_W, D_W, D_W, LANES)
    dtypes = (F32, F32, F32, F32, BF16, F32, F32)
    out_shape = tuple(jax.ShapeDtypeStruct((T, wd), dt) for wd, dt in zip(widths, dtypes))
    out_shape += (jax.ShapeDtypeStruct((N_GATES, T), F32),)
    return pl.pallas_call(
        _inproj_odd_kernel, out_shape=out_shape, grid=(T // tm,),
        in_specs=[_stream_spec(x2, D, tm), full(g), full(w), full(wgt)],
        out_specs=tuple(row(wd) for wd in widths) + (pl.BlockSpec((N_GATES, tm), lambda i: (0, i)),),
        compiler_params=_params(1), name="inproj_odd")(x2, g, w, wgt)


def _mla_prep_kernel(cq_ref, ckv_ref, kr_ref, nq_ref, nkv_ref, wq_ref, wk_ref, wv_ref, cos_ref, sin_ref,
                     q_ref, k_ref, vt_ref):
    lane = lax.broadcasted_iota(I32, (1, LANES), 1)
    first = lane < C_NOPE + C_ROPE // 2
    cos, sin = cos_ref[...], sin_ref[...]
    rope = lambda z: _rope_slab(z, cos, sin, C_ROPE // 2, first)
    q = _dot(_rms(cq_ref[...], nq_ref[...]).astype(BF16), wq_ref[...])
    hkv = _rms(ckv_ref[...], nkv_ref[...]).astype(BF16)
    kn = _dot(hkv, wk_ref[...])
    kr = rope(kr_ref[...])
    for hh in range(C_HEADS):
        slab = slice(hh * C_SLOT, (hh + 1) * C_SLOT)
        q_ref[:, slab] = (rope(q[:, slab]) * C_LOG2_SCALE).astype(BF16)
        k_ref[:, slab] = (kn[:, slab] + kr).astype(BF16)
    vt = _dot_nt(wv_ref[...], hkv)
    row = lax.broadcasted_iota(I32, vt.shape, 0)
    vt_ref[...] = jnp.where(row % C_VROWS < C_V, vt, 1.0).astype(BF16)


def _mla_prep(cq, ckv, kr, nq, nkv, w_uq, w_ukv, cos, sin, tm):
    T = cq.shape[0]
    pad_q = C_SLOT - C_NOPE - C_ROPE
    wq = jnp.pad(w_uq.reshape(C_Q_LORA, C_HEADS, C_NOPE + C_ROPE), ((0, 0), (0, 0), (0, pad_q)))
    wq = wq.reshape(C_Q_LORA, C_HEADS * C_SLOT).astype(BF16)
    wkv = w_ukv.reshape(C_KV_LORA, C_HEADS, C_NOPE + C_V)
    wk = jnp.pad(wkv[:, :, :C_NOPE], ((0, 0), (0, 0), (0, C_SLOT - C_NOPE)))
    wk = wk.reshape(C_KV_LORA, C_HEADS * C_SLOT).astype(BF16)
    wv = jnp.pad(wkv[:, :, C_NOPE:], ((0, 0), (0, 0), (0, C_VROWS - C_V)))
    wv = wv.reshape(C_KV_LORA, C_HEADS * C_VROWS).T.astype(BF16)
    nq, nkv = nq.reshape(1, -1), nkv.reshape(1, -1)
    row = lambda width: pl.BlockSpec((tm, width), lambda i: (i, 0))
    full = lambda a: pl.BlockSpec(a.shape, lambda i: (0, 0))
    slots = jax.ShapeDtypeStruct((T, C_HEADS * C_SLOT), BF16)
    vt_shape = jax.ShapeDtypeStruct((T // tm, C_HEADS * C_VROWS, tm), BF16)
    return pl.pallas_call(
        _mla_prep_kernel, out_shape=(slots, slots, vt_shape), grid=(T // tm,),
        in_specs=[row(C_Q_LORA), row(C_KV_LORA), row(C_SLOT), full(nq), full(nkv), full(wq), full(wk), full(wv),
                  row(LANES), row(LANES)],
        out_specs=(row(C_HEADS * C_SLOT), row(C_HEADS * C_SLOT),
                   pl.BlockSpec((None, C_HEADS * C_VROWS, tm), lambda i: (i, 0, 0))),
        compiler_params=_params(1), name="mla_prep")(cq, ckv, kr, nq, nkv, wq, wk, wv, cos, sin)


def _mla_attn_kernel(q_ref, k_ref, vt_ref, o_ref):
    tq = q_ref.shape[0]
    n_chunks, _, key_chunk = vt_ref.shape
    slabs = [slice(hh * C_SLOT, (hh + 1) * C_SLOT) for hh in range(C_PAIR)]
    vrows = [slice(hh * C_VROWS, (hh + 1) * C_VROWS) for hh in range(C_PAIR)]
    qs = [q_ref[:, slab] for slab in slabs]

    def scores(hh, c):
        st = _dot_nt(k_ref[c * key_chunk:(c + 1) * key_chunk, slabs[hh]], qs[hh])
        return st, jnp.max(st, axis=0, keepdims=True)

    m = [jnp.full((1, tq), NEG, F32) for _ in range(C_PAIR)]
    acc = [jnp.zeros((C_VROWS, tq), F32) for _ in range(C_PAIR)]
    st = [scores(hh, 0) for hh in range(C_PAIR)]
    for c in range(n_chunks):
        for hh in range(C_PAIR):
            st_next = scores(hh, c + 1) if c + 1 < n_chunks else None
            m_new = jnp.maximum(m[hh], st[hh][1])
            pt = jnp.exp2((st[hh][0] - m_new).astype(BF16))
            acc[hh] = jnp.exp2(m[hh] - m_new) * acc[hh] + _dot(vt_ref[c, vrows[hh], :], pt)
            m[hh], st[hh] = m_new, st_next
    for hh in range(C_PAIR):
        ot = acc[hh][:C_V] / acc[hh][C_V:C_V + 1]
        o = jnp.concatenate([ot, jnp.zeros((LANES - C_V, tq), F32)], axis=0).T
        o_ref[:, hh * C_V:(hh + 1) * C_V] = o[:, :C_V].astype(BF16)


def _mla_attn(q, k, vt, B, S, tq):
    T = B * S
    nq = S // tq
    key_chunk = vt.shape[-1]
    vt = vt.reshape(B, S // key_chunk, C_HEADS * C_VROWS, key_chunk)
    return pl.pallas_call(
        _mla_attn_kernel,
        out_shape=jax.ShapeDtypeStruct((T, C_HEADS * C_V), BF16), grid=(B, C_HEADS // C_PAIR, nq),
        in_specs=[pl.BlockSpec((tq, C_PAIR * C_SLOT), lambda b, p, i: (b * nq + i, p)),
                  pl.BlockSpec((S, C_PAIR * C_SLOT), lambda b, p, i: (b, p)),
                  pl.BlockSpec((None, S // key_chunk, C_PAIR * C_VROWS, key_chunk), lambda b, p, i: (b, 0, p, 0))],
        out_specs=pl.BlockSpec((tq, C_PAIR * C_V), lambda b, p, i: (b * nq + i, p)),
        compiler_params=_params(3), name="mla_attn")(q, k, vt)


HALO = 8


def _conv_kernel(xp_ref, x_ref, xn_ref, w_ref, o_ref, *, n_tiles):
    j = pl.program_id(1)
    tc = x_ref.shape[0]
    prev = jnp.where(j > 0, xp_ref[...], 0.0)
    nxt = jnp.where(j < n_tiles - 1, xn_ref[...], 0.0)
    ext = jnp.concatenate([prev, x_ref[...], nxt], axis=0)
    rows = tc + 2 * HALO
    y = jnp.zeros(x_ref.shape, F32)
    for w in range(D_CONV):
        first = HALO - D_CONV // 2 + w
        y = y + pltpu.roll(ext, (rows - first) % rows, 0)[:tc] * w_ref[w:w + 1, :]
    y = y * jax.nn.sigmoid(y)
    o_ref[:, :D_W] = y[:, :D_W]
    o_ref[:, D_W:] = y[:, D_W:] * (D_HEAD_DIM ** -0.5)


def _conv_prep(dqk, conv_w, B, S, tc):
    T, C = dqk.shape
    n_tiles = S // tc
    per, last = tc // HALO, T // HALO - 1
    cur = lambda b, j: (b * n_tiles + j, 0)
    return pl.pallas_call(
        functools.partial(_conv_kernel, n_tiles=n_tiles),
        out_shape=jax.ShapeDtypeStruct((T, C), F32), grid=(B, n_tiles),
        in_specs=[pl.BlockSpec((HALO, C), lambda b, j: (jnp.maximum((b * n_tiles + j) * per - 1, 0), 0)),
                  pl.BlockSpec((tc, C), cur),
                  pl.BlockSpec((HALO, C), lambda b, j: (jnp.minimum((b * n_tiles + j + 1) * per, last), 0)),
                  pl.BlockSpec(conv_w.shape, lambda b, j: (0, 0))],
        out_specs=pl.BlockSpec((tc, C), cur),
        compiler_params=_params(2), name="conv_prep")(dqk, dqk, dqk, conv_w)


def _log_sigmoid(x):
    return -(jnp.maximum(-x, 0.0) + jnp.log1p(jnp.exp(-jnp.abs(x))))


def _tri(lower):
    a = lax.broadcasted_iota(I32, (BLOCK, BLOCK), 0)
    b = lax.broadcasted_iota(I32, (BLOCK, BLOCK), 1)
    return ((b <= a) if lower else (b >= a)).astype(BF16)


def _mlstm_update(s_ref, m_ref, h, g, a, k, v_ones):
    a_max = jnp.max(a, axis=0, keepdims=True)
    w = jnp.exp(a - a_max)
    upd = _dot_tn(v_ones, (k * w).astype(BF16))
    m = m_ref[h:h + 1, 0:1]
    m_new = jnp.maximum(g + m, a_max)
    rows = slice(h * 2 * D_HEAD_DIM, (h + 1) * 2 * D_HEAD_DIM)
    s_ref[rows, :] = jnp.exp(g + m - m_new) * s_ref[rows, :] + jnp.exp(a_max - m_new) * upd
    m_ref[h:h + 1, :] = jnp.broadcast_to(m_new, (1, LANES))


def _mlstm_bwd_state_kernel(bias_c_ref, qk_ref, v_ref, gc_ref, s_out_ref, m_out_ref, s_ref, m_ref):
    @pl.when(pl.program_id(1) == 0)
    def _():
        s_ref[...] = jnp.zeros_like(s_ref)
        m_ref[...] = jnp.zeros_like(m_ref)

    s_out_ref[...] = s_ref[...]
    m_out_ref[...] = m_ref[...]
    n_seq = qk_ref.shape[0]
    gc = [gc_ref[i] + bias_c_ref[...] for i in range(n_seq)]
    lf = jnp.concatenate([_log_sigmoid(x) for x in gc], axis=1)
    suffix = sum(_dot(_tri(False), p) for p in _split3(lf))
    ones = jnp.ones((BLOCK, D_HEAD_DIM), BF16)
    for h in range(D_HEADS):
        hs = slice(h * D_HEAD_DIM, (h + 1) * D_HEAD_DIM)
        for i in range(n_seq):
            lane = i * LANES + F_BWD + h
            sb = suffix[:, lane:lane + 1]
            g = sb[0:1, :]
            a = g - sb + gc[i][:, I_BWD + h:I_BWD + h + 1]
            k = qk_ref[i, :, D_W + h * D_HEAD_DIM:D_W + (h + 1) * D_HEAD_DIM]
            _mlstm_update(s_ref.at[i], m_ref.at[i], h, g, a, k, jnp.concatenate([v_ref[i, :, hs], ones], axis=1))


def _mlstm_bwd_state(bias_c, qk, dv, gc, B, S):
    N = S // BLOCK
    nb = SEQS_PER_STEP if B % SEQS_PER_STEP == 0 else 1
    per_seq = lambda a: a.reshape(B, S, a.shape[-1])
    rev = lambda width: pl.BlockSpec((nb, BLOCK, width), lambda b, n: (b, N - 1 - n, 0))
    out_shape = (jax.ShapeDtypeStruct((B, N, 2 * D_W, D_HEAD_DIM), F32), jax.ShapeDtypeStruct((B, N, 8, LANES), F32))
    return pl.pallas_call(
        _mlstm_bwd_state_kernel, out_shape=out_shape, grid=(B // nb, N),
        in_specs=[pl.BlockSpec(bias_c.shape, lambda b, n: (0, 0)), rev(2 * D_W), rev(D_W), rev(LANES)],
        out_specs=(pl.BlockSpec((nb, None, 2 * D_W, D_HEAD_DIM), lambda b, n: (b, N - 1 - n, 0, 0)),
                   pl.BlockSpec((nb, None, 8, LANES), lambda b, n: (b, N - 1 - n, 0, 0))),
        scratch_shapes=[pltpu.VMEM((nb, 2 * D_W, D_HEAD_DIM), F32), pltpu.VMEM((nb, 8, LANES), F32)],
        compiler_params=_params(2), name="mlstm_bwd_state")(bias_c, per_seq(qk), per_seq(dv), per_seq(gc))


def _mlstm_main_kernel(bias_c_ref, bias_r_ref, qk_ref, v_ref, og_ref, gc_ref, gt_ref, sb_ref, mb_ref,
                       o_ref, s_ref, m_ref):
    @pl.when(pl.program_id(1) == 0)
    def _():
        s_ref[...] = jnp.zeros_like(s_ref)
        m_ref[...] = jnp.zeros_like(m_ref)

    gc = gc_ref[...] + bias_c_ref[...]
    gr = gt_ref[...] + bias_r_ref[...]
    lower, upper = _tri(True), _tri(False)
    lfc, lfr = _split3(_log_sigmoid(gc)), _split3(_log_sigmoid(gr))
    pre_c = sum(_dot(lower, p) for p in lfc)
    suf_c = sum(_dot(upper, p) for p in lfc)
    pre_r = sum(_dot(p, upper) for p in lfr)
    suf_r = sum(_dot(p, lower) for p in lfr)
    src = lax.broadcasted_iota(I32, (BLOCK, BLOCK), 0)
    qry = lax.broadcasted_iota(I32, (BLOCK, BLOCK), 1)
    ones = jnp.ones((BLOCK, D_HEAD_DIM), BF16)
    og = og_ref[...]
    for h in range(D_HEADS):
        hs = slice(h * D_HEAD_DIM, (h + 1) * D_HEAD_DIM)
        rows = slice(h * 2 * D_HEAD_DIM, (h + 1) * 2 * D_HEAD_DIM)
        k = qk_ref[:, D_W + h * D_HEAD_DIM:D_W + (h + 1) * D_HEAD_DIM]
        qb, v = qk_ref[:, hs].astype(BF16), v_ref[:, hs]
        kq = _dot_nt(k.astype(BF16), qb)

        def direction(b_row, c_col, mask, state_t, m_prev):
            logd = jnp.where(mask, b_row + c_col, NEG)
            log_inter = b_row + m_prev
            m_t = jnp.maximum(jnp.max(logd, axis=0, keepdims=True), log_inter)
            s = kq * jnp.exp(logd - m_t)
            inter_w = jnp.exp(log_inter - m_t)
            qs = _dot_nt(state_t.astype(BF16), qb)
            num = _dot_tn(v, s.astype(BF16)) + inter_w * qs[:D_HEAD_DIM, :]
            den = jnp.sum(s, axis=0, keepdims=True) + inter_w * qs[D_HEAD_DIM:D_HEAD_DIM + 1, :]
            return num / jnp.maximum(jnp.abs(den), jnp.exp(-m_t))

        f, bk = F_FWD + h, F_BWD + h
        h_f = direction(pre_r[f:f + 1, :], gc[:, I_FWD + h:I_FWD + h + 1] - pre_c[:, f:f + 1], src <= qry,
                        s_ref[rows, :], m_ref[h:h + 1, 0:1])
        h_b = direction(suf_r[bk:bk + 1, :], gc[:, I_BWD + h:I_BWD + h + 1] - suf_c[:, bk:bk + 1], src > qry,
                        sb_ref[rows, :], mb_ref[h:h + 1, 0:1])
        o_ref[:, hs] = (jax.nn.sigmoid(og[:, hs]) * (h_f + h_b).T).astype(BF16)
        b_col = pre_c[:, f:f + 1]
        g = b_col[BLOCK - 1:BLOCK, :]
        a = g - b_col + gc[:, I_FWD + h:I_FWD + h + 1]
        _mlstm_update(s_ref, m_ref, h, g, a, k, jnp.concatenate([v, ones], axis=1))


def _mlstm_main(bias_c, bias_r, qk, dv, og, gc, gt, sb, mb, B, S):
    N = S // BLOCK
    T = B * S
    cur = lambda b, n: (b * N + n, 0)
    return pl.pallas_call(
        _mlstm_main_kernel, out_shape=jax.ShapeDtypeStruct((T, D_W), BF16), grid=(B, N),
        in_specs=[pl.BlockSpec(bias_c.shape, lambda b, n: (0, 0)), pl.BlockSpec(bias_r.shape, lambda b, n: (0, 0)),
                  pl.BlockSpec((BLOCK, 2 * D_W), cur), pl.BlockSpec((BLOCK, D_W), cur), pl.BlockSpec((BLOCK, D_W), cur),
                  pl.BlockSpec((BLOCK, LANES), cur), pl.BlockSpec((N_GATES, BLOCK), lambda b, n: (0, b * N + n)),
                  pl.BlockSpec((None, None, 2 * D_W, D_HEAD_DIM), lambda b, n: (b, n, 0, 0)),
                  pl.BlockSpec((None, None, 8, LANES), lambda b, n: (b, n, 0, 0))],
        out_specs=pl.BlockSpec((BLOCK, D_W), cur),
        scratch_shapes=[pltpu.VMEM((2 * D_W, D_HEAD_DIM), F32), pltpu.VMEM((8, LANES), F32)],
        compiler_params=_params(2), name="mlstm_main")(bias_c, bias_r, qk, dv, og, gc, gt, sb, mb)


def odd_mixer_parts(x2, positions, g_mix, w_in, norm_q, norm_kv, w_uq, w_ukv, conv_w, gate_bias, B, S,
                    tm=512, tq=256, key_chunk=1024):
    w, wgt = _odd_weights(w_in)
    cq, ckv, kr, dqk, dv, og, gc, gt = _inproj_odd(x2, g_mix.reshape(1, -1), w, wgt, tm, B * S)
    cos, sin = _rope_tables(positions, MLA_THETA, C_SLOT, C_NOPE, C_ROPE)
    q, k, vt = _mla_prep(cq, ckv, kr, norm_q, norm_kv, w_uq, w_ukv, cos, sin, min(key_chunk, S))
    o_c = _mla_attn(q, k, vt, B, S, min(tq, S))
    qk = _conv_prep(dqk, conv_w.astype(F32), B, S, min(tm, S))
    bias = gate_bias.astype(F32).reshape(1, N_GATES)
    bias_c = jnp.pad(bias, ((0, 0), (0, LANES - N_GATES)))
    bias_r = bias.reshape(N_GATES, 1)
    sb, mb = _mlstm_bwd_state(bias_c, qk, dv, gc, B, S)
    o_d = _mlstm_main(bias_c, bias_r, qk, dv, og, gc, gt, sb, mb, B, S)
    return o_c, o_d


def kernel(x, positions, norm_mix, norm_ffn, norm_final, ev_w_in, ev_w_out, attn_sink, ret_decay_logit,
           od_w_in, od_w_out, mla_norm_q, mla_norm_kv, mla_w_uq, mla_w_ukv, mlstm_conv, mlstm_gate_bias,
           moe_router, moe_w_gate, moe_w_up, moe_w_down):
    B, S, D = x.shape
    tm = min(512, S)
    x2 = x.reshape(B * S, D)
    for layer in range(norm_mix.shape[0]):
        j = layer // 2
        if layer % 2 == 0:
            o1, o2 = even_mixer_parts(x2, positions, norm_mix[layer], ev_w_in[j], attn_sink[j],
                                      ret_decay_logit[j], B, S, tm)
            w_out = ev_w_out[j]
        else:
            o1, o2 = odd_mixer_parts(x2, positions, norm_mix[layer], od_w_in[j], mla_norm_q[j], mla_norm_kv[j],
                                     mla_w_uq[j], mla_w_ukv[j], mlstm_conv[j], mlstm_gate_bias[j], B, S, tm)
            w_out = od_w_out[j]
        x2 = moe_layer(o1, o2, w_out, x2, norm_ffn[layer], moe_router[layer],
                       _expert_weights_bf16(layer, moe_w_gate, moe_w_up, moe_w_down), B, S, tm)
    return _final_norm(x2, norm_final, tm, B * S).reshape(B, S, D)
```

```python
import functools

import jax
import jax.numpy as jnp
import numpy as np
from jax import lax
from jax.experimental import pallas as pl
from jax.experimental.pallas import tpu as pltpu

F32 = jnp.float32
BF16 = jnp.bfloat16
I32 = jnp.int32

LANES = 128
BLOCK = 128
RMS_EPS = 1e-6
GN_EPS = 1e-5
NEG = -1e30
VMEM_LIMIT = 56 * 1024 * 1024

A_HEADS, A_KV_HEADS, A_HEAD_DIM = 8, 2, 64
ROPE_THETA = 500000.0
ROPE_DIM = A_HEAD_DIM // 4
B_HEADS, B_HEAD_DIM = 8, 64
RET_THETA = 10000.0
C_HEADS, C_NOPE, C_ROPE, C_V = 8, 64, 32, 64
C_Q_LORA, C_KV_LORA = 512, 256
MLA_THETA = 10000.0
D_HEADS, D_HEAD_DIM, D_CONV = 4, 128, 5
N_EXPERTS = 16
EC_CAPACITY_FACTOR = 2

A_Q_W = A_HEADS * A_HEAD_DIM
A_KV_W = A_KV_HEADS * A_HEAD_DIM
B_W = B_HEADS * B_HEAD_DIM
D_W = D_HEADS * D_HEAD_DIM
C_SLOT = 128
C_PAIR = 4
BF16_ROWS = 16
C_VROWS = C_V + BF16_ROWS
C_LOG2_SCALE = float((C_NOPE + C_ROPE) ** -0.5 * np.log2(np.e))


def _params(n_axes, vmem=VMEM_LIMIT):
    return pltpu.CompilerParams(dimension_semantics=("arbitrary",) * n_axes,
                                vmem_limit_bytes=vmem)


def _rms(x, g):
    return x * lax.rsqrt(jnp.mean(x * x, axis=-1, keepdims=True) + RMS_EPS) * g


def _dot(a, b):
    return jnp.dot(a, b, preferred_element_type=F32)


def _dot_nt(a, b):
    return lax.dot_general(a, b, (((1,), (1,)), ((), ())), preferred_element_type=F32)


def _dot_tn(a, b):
    return lax.dot_general(a, b, (((0,), (0,)), ((), ())), preferred_element_type=F32)


def _rope_slab(z, cos, sin, half, first_half):
    partner = jnp.where(first_half, pltpu.roll(z, LANES - half, 1), pltpu.roll(z, half, 1))
    return z * cos + partner * sin


def _rope_tables(positions, theta, head_dim, rot_start, rot_dim):
    half = rot_dim // 2
    assert head_dim % half == 0 and rot_start % half == 0
    inv_freq = theta ** (-jnp.arange(half, dtype=F32) * 2.0 / rot_dim)
    d = np.arange(LANES) % head_dim - rot_start
    rot = (d >= 0) & (d < rot_dim)
    sign = np.where(d < half, -1.0, 1.0).astype(np.float32)
    ang = positions.astype(F32).reshape(-1, 1) * inv_freq
    ang = jnp.tile(ang, (1, LANES // half))
    cos = jnp.where(rot[None, :], jnp.cos(ang), 1.0)
    sin = jnp.where(rot[None, :], jnp.sin(ang) * sign[None, :], 0.0)
    return cos, sin


def _inproj_even_kernel(x_ref, g_ref, w_ref, ca_ref, sa_ref, cb_ref, sb_ref,
                        aq_ref, akv_ref, bq_ref, bk_ref, bv_ref, bg_ref):
    h = _rms(x_ref[...], g_ref[...]).astype(BF16)
    lane = lax.broadcasted_iota(I32, (1, LANES), 1)
    first_a = (lane % A_HEAD_DIM) < (ROPE_DIM // 2)
    first_b = (lane % B_HEAD_DIM) < (B_HEAD_DIM // 2)
    ca, sa, cb, sb = ca_ref[...], sa_ref[...], cb_ref[...], sb_ref[...]

    def rope_a(z):
        return _rope_slab(z, ca, sa, ROPE_DIM // 2, first_a)

    def rope_b(z):
        return _rope_slab(z, cb, sb, B_HEAD_DIM // 2, first_b)

    off = 0
    z = _dot(h, w_ref[:, off:off + A_Q_W])
    for s in range(A_Q_W // LANES):
        aq_ref[:, s * LANES:(s + 1) * LANES] = rope_a(z[:, s * LANES:(s + 1) * LANES]).astype(BF16)
    off += A_Q_W
    z = _dot(h, w_ref[:, off:off + 2 * A_KV_W])
    akv_ref[:, :A_KV_W] = rope_a(z[:, :A_KV_W]).astype(BF16)
    akv_ref[:, A_KV_W:] = z[:, A_KV_W:].astype(BF16)
    off += 2 * A_KV_W
    z = _dot(h, w_ref[:, off:off + B_W])
    for s in range(B_W // LANES):
        bq_ref[:, s * LANES:(s + 1) * LANES] = rope_b(z[:, s * LANES:(s + 1) * LANES])
    off += B_W
    z = _dot(h, w_ref[:, off:off + B_W])
    for s in range(B_W // LANES):
        bk_ref[:, s * LANES:(s + 1) * LANES] = rope_b(z[:, s * LANES:(s + 1) * LANES]) * (B_HEAD_DIM ** -0.5)
    off += B_W
    bv_ref[...] = _dot(h, w_ref[:, off:off + B_W]).astype(BF16)
    off += B_W
    bg_ref[...] = _dot(h, w_ref[:, off:off + B_W])


def _stream_spec(xs, D, tm):
    col = xs.shape[1] // D - 1
    return pl.BlockSpec((tm, D), lambda i: (i, col))


def _inproj_even(x2, g, w, tabs, tm):
    T, D = tabs[0].shape[0], g.shape[-1]
    ncol = w.shape[1]
    row = lambda width: pl.BlockSpec((tm, width), lambda i: (i, 0))
    full = lambda a: pl.BlockSpec(a.shape, lambda i: (0, 0))
    out_shape = (jax.ShapeDtypeStruct((T, A_Q_W), BF16), jax.ShapeDtypeStruct((T, 2 * A_KV_W), BF16),
                 jax.ShapeDtypeStruct((T, B_W), F32), jax.ShapeDtypeStruct((T, B_W), F32),
                 jax.ShapeDtypeStruct((T, B_W), BF16), jax.ShapeDtypeStruct((T, B_W), F32))
    return pl.pallas_call(
        _inproj_even_kernel, out_shape=out_shape, grid=(T // tm,),
        in_specs=[_stream_spec(x2, D, tm), full(g), pl.BlockSpec((D, ncol), lambda i: (0, 0))] + [row(LANES)] * 4,
        out_specs=(row(A_Q_W), row(2 * A_KV_W), row(B_W), row(B_W), row(B_W), row(B_W)),
        compiler_params=_params(1), name="inproj_even")(x2, g, w, *tabs)


def _win_attn_kernel(sink_ref, q_ref, kvp_ref, kvc_ref, kvn_ref, o_ref, *, n_blocks):
    n = pl.program_id(1)
    group = A_HEADS // A_KV_HEADS
    cols = group * BLOCK
    kj = lax.broadcasted_iota(I32, (BLOCK, cols), 0)
    qi = lax.broadcasted_iota(I32, (BLOCK, cols), 1) % BLOCK
    ok_prev = (kj >= qi) & (n > 0)
    ok_next = (kj <= qi) & (n < n_blocks - 1)
    scale = A_HEAD_DIM ** -0.5
    for s_i, g in [(s_i, g) for g in range(A_KV_HEADS) for s_i in range(q_ref.shape[0])]:
        q, kvp, kvc, kvn = q_ref[s_i], kvp_ref.at[s_i], kvc_ref.at[s_i], kvn_ref.at[s_i]
        ks = slice(g * A_HEAD_DIM, (g + 1) * A_HEAD_DIM)
        vs = slice(A_KV_W + g * A_HEAD_DIM, A_KV_W + (g + 1) * A_HEAD_DIM)
        qg = jnp.concatenate([q[:, (g * group + i) * A_HEAD_DIM:(g * group + i + 1) * A_HEAD_DIM]
                              for i in range(group)], axis=0)
        sink = jnp.concatenate([jnp.full((1, BLOCK), sink_ref[g * group + i], F32)
                                for i in range(group)], axis=1)
        sp = jnp.where(ok_prev, _dot_nt(kvp[:, ks], qg) * scale, NEG)
        sc = _dot_nt(kvc[:, ks], qg) * scale
        sn = jnp.where(ok_next, _dot_nt(kvn[:, ks], qg) * scale, NEG)
        m = jnp.maximum(jnp.maximum(jnp.max(sp, 0, keepdims=True), jnp.max(sc, 0, keepdims=True)),
                        jnp.maximum(jnp.max(sn, 0, keepdims=True), sink))
        pp, pc, pn = jnp.exp(sp - m), jnp.exp(sc - m), jnp.exp(sn - m)
        den = (jnp.sum(pp, 0, keepdims=True) + jnp.sum(pc, 0, keepdims=True)
               + jnp.sum(pn, 0, keepdims=True) + jnp.exp(sink - m))
        ot = (_dot_tn(kvp[:, vs], pp.astype(BF16)) + _dot_tn(kvc[:, vs], pc.astype(BF16))
              + _dot_tn(kvn[:, vs], pn.astype(BF16))) / den
        o = ot.T
        for i in range(group):
            hh = g * group + i
            o_ref[s_i, :, hh * A_HEAD_DIM:(hh + 1) * A_HEAD_DIM] = o[i * BLOCK:(i + 1) * BLOCK].astype(BF16)


def _win_attn(aq, akv, sink, B, S):
    N = S // BLOCK
    nb = SEQS_PER_STEP if B % SEQS_PER_STEP == 0 else 1
    per_seq = lambda a: a.reshape(B, S, a.shape[-1])
    kv = lambda shift: pl.BlockSpec(
        (nb, BLOCK, 2 * A_KV_W), lambda b, n: (b, jnp.clip(n + shift, 0, N - 1), 0))
    qo = lambda: pl.BlockSpec((nb, BLOCK, A_Q_W), lambda b, n: (b, n, 0))
    out = pl.pallas_call(
        functools.partial(_win_attn_kernel, n_blocks=N),
        out_shape=jax.ShapeDtypeStruct((B, S, A_Q_W), BF16), grid=(B // nb, N),
        in_specs=[pl.BlockSpec(memory_space=pltpu.SMEM), qo(), kv(-1), kv(0), kv(1)],
        out_specs=qo(),
        compiler_params=_params(2), name="win_attn")(sink, per_seq(aq), per_seq(akv), per_seq(akv), per_seq(akv))
    return out.reshape(B * S, A_Q_W)


def _ret_tables(lg_ref, lgl_ref, d_ref, xif_ref, xib_ref, zf_ref, zb_ref):
    j = lax.broadcasted_iota(I32, (BLOCK, BLOCK), 0)
    i = lax.broadcasted_iota(I32, (BLOCK, BLOCK), 1)
    rel = (i - j).astype(F32)
    for h in range(B_HEADS):
        d_ref[h] = jnp.where(i >= j, jnp.exp(rel * lg_ref[0, h]), jnp.exp(-rel * lg_ref[1, h]))
    t = lax.broadcasted_iota(I32, (BLOCK, B_W), 0).astype(F32)
    lgf, lgb = lgl_ref[0:1, :], lgl_ref[1:2, :]
    xif_ref[...] = jnp.exp((t + 1.0) * lgf)
    zf_ref[...] = jnp.exp((BLOCK - 1.0 - t) * lgf)
    xib_ref[...] = jnp.exp((BLOCK - t) * lgb)
    zb_ref[...] = jnp.exp(t * lgb)


def _ret_bwd_state_kernel(lgl_ref, k_ref, v_ref, r_out_ref, r_ref):
    n = pl.program_id(1)

    @pl.when(n == 0)
    def _():
        r_ref[...] = jnp.zeros_like(r_ref)

    r_out_ref[...] = r_ref[...]
    lgb = lgl_ref[1:2, :]
    t = lax.broadcasted_iota(I32, (BLOCK, B_W), 0).astype(F32)
    zeta = jnp.exp(t * lgb)
    cd = jnp.exp(BLOCK * lgb)
    for i in range(k_ref.shape[0]):
        _ret_state_update(r_ref.at[i], cd, v_ref[i], (k_ref[i] * zeta).astype(BF16))


def _ret_state_update(r_ref, chunk_decay, v, kz):
    upd = _dot_tn(v, kz)
    for h in range(B_HEADS):
        hs = slice(h * B_HEAD_DIM, (h + 1) * B_HEAD_DIM)
        r_ref[hs, :] = chunk_decay[:, hs] * r_ref[hs, :] + upd[hs, hs]


def _ret_bwd_state(lgl, bk, bv, B, S):
    N = S // BLOCK
    nb = SEQS_PER_STEP if B % SEQS_PER_STEP == 0 else 1
    per_seq = lambda a: a.reshape(B, S, a.shape[-1])
    blk = lambda: pl.BlockSpec((nb, BLOCK, B_W), lambda b, n: (b, N - 1 - n, 0))
    return pl.pallas_call(
        _ret_bwd_state_kernel,
        out_shape=jax.ShapeDtypeStruct((B, N, B_W, B_HEAD_DIM), F32), grid=(B // nb, N),
        in_specs=[pl.BlockSpec(lgl.shape, lambda b, n: (0, 0)), blk(), blk()],
        out_specs=pl.BlockSpec((nb, None, B_W, B_HEAD_DIM), lambda b, n: (b, N - 1 - n, 0, 0)),
        scratch_shapes=[pltpu.VMEM((nb, B_W, B_HEAD_DIM), F32)],
        compiler_params=_params(2), name="ret_bwd_state")(lgl, per_seq(bk), per_seq(bv))


def _ret_main_kernel(lg_ref, lgl_ref, q_ref, k_ref, v_ref, g_ref, rb_ref, o_ref,
                     rf_ref, d_ref, xif_ref, xib_ref, zf_ref, zb_ref):
    b, n = pl.program_id(0), pl.program_id(1)

    @pl.when((b == 0) & (n == 0))
    def _():
        _ret_tables(lg_ref, lgl_ref, d_ref, xif_ref, xib_ref, zf_ref, zb_ref)

    @pl.when(n == 0)
    def _():
        rf_ref[...] = jnp.zeros_like(rf_ref)

    n_seq = q_ref.shape[0]
    cdf = jnp.exp(BLOCK * lgl_ref[0:1, :])
    q, k, v, gate = ([r[i] for i in range(n_seq)] for r in (q_ref, k_ref, v_ref, g_ref))
    qb = [x.astype(BF16) for x in q]
    kb = [x.astype(BF16) for x in k]
    qxf = [(x * xif_ref[...]).astype(BF16) for x in q]
    qxb = [(x * xib_ref[...]).astype(BF16) for x in q]
    kzf = [(x * zf_ref[...]).astype(BF16) for x in k]
    ynt = [[None] * B_HEADS for _ in range(n_seq)]
    for h in range(B_HEADS):
        hs = slice(h * B_HEAD_DIM, (h + 1) * B_HEAD_DIM)
        for i in range(n_seq):
            st = _dot_nt(kb[i][:, hs], qb[i][:, hs]) * d_ref[h]
            states = jnp.concatenate([rf_ref[i, hs, :], rb_ref[i, hs, :]], axis=1).astype(BF16)
            queries = jnp.concatenate([qxf[i][:, hs], qxb[i][:, hs]], axis=1)
            yt = _dot_tn(v[i][:, hs], st.astype(BF16)) + _dot_nt(states, queries)
            yc = yt - jnp.mean(yt, axis=0, keepdims=True)
            ynt[i][h] = yc * lax.rsqrt(jnp.mean(yc * yc, axis=0, keepdims=True) + GN_EPS)
    for i in range(n_seq):
        yn = jnp.concatenate(ynt[i], axis=0).T
        o_ref[i] = (gate[i] * jax.nn.sigmoid(gate[i]) * yn).astype(BF16)
        _ret_state_update(rf_ref.at[i], cdf, v[i], kzf[i])


SEQS_PER_STEP = 4


def _ret_main(lg, lgl, bq, bk, bv, bg, rb, B, S):
    N = S // BLOCK
    nb = SEQS_PER_STEP if B % SEQS_PER_STEP == 0 else 1
    per_seq = lambda a: a.reshape(B, S, a.shape[-1])
    blk = lambda: pl.BlockSpec((nb, BLOCK, B_W), lambda b, n: (b, n, 0))
    tab = lambda: pltpu.VMEM((BLOCK, B_W), F32)
    out = pl.pallas_call(
        _ret_main_kernel, out_shape=jax.ShapeDtypeStruct((B, S, B_W), BF16), grid=(B // nb, N),
        in_specs=[pl.BlockSpec(memory_space=pltpu.SMEM), pl.BlockSpec(lgl.shape, lambda b, n: (0, 0)),
                  blk(), blk(), blk(), blk(),
                  pl.BlockSpec((nb, None, B_W, B_HEAD_DIM), lambda b, n: (b, n, 0, 0))],
        out_specs=blk(),
        scratch_shapes=[pltpu.VMEM((nb, B_W, B_HEAD_DIM), F32), pltpu.VMEM((B_HEADS, BLOCK, BLOCK), F32),
                        tab(), tab(), tab(), tab()],
        compiler_params=_params(2), name="ret_main")(lg, lgl, per_seq(bq), per_seq(bk), per_seq(bv), per_seq(bg), rb)
    return out.reshape(B * S, B_W)


def even_mixer_parts(x2, positions, g_mix, w_in, sink, decay_logit, B, S, tm=512):
    ca, sa = _rope_tables(positions, ROPE_THETA, A_HEAD_DIM, 0, ROPE_DIM)
    cb, sb = _rope_tables(positions, RET_THETA, B_HEAD_DIM, 0, B_HEAD_DIM)
    aq, akv, bq, bk, bv, bg = _inproj_even(x2, g_mix.reshape(1, -1), w_in.astype(BF16), (ca, sa, cb, sb), tm)
    o_a = _win_attn(aq, akv, sink.astype(F32), B, S)
    lg = jax.nn.log_sigmoid(decay_logit.astype(F32))
    lgl = jnp.repeat(lg, B_HEAD_DIM, axis=1)
    rb = _ret_bwd_state(lgl, bk, bv, B, S)
    o_b = _ret_main(lg, lgl, bq, bk, bv, bg, rb, B, S)
    return o_a, o_b


def _outproj_kernel(o1_ref, o2_ref, w_ref, x_ref, g_ref, wr_ref, hx_ref, aff_ref):
    half = o1_ref.shape[1]
    D = x_ref.shape[1]
    x1 = x_ref[...] + (_dot(o1_ref[...], w_ref[:half, :]) + _dot(o2_ref[...], w_ref[half:, :]))
    hn = _rms(x1, g_ref[...])
    hx_ref[:, :D] = hn
    hx_ref[:, D:] = x1
    logits = _dot_nt(wr_ref[...], hn.astype(BF16))
    e = jnp.exp(logits - jnp.max(logits, axis=0, keepdims=True))
    aff_ref[...] = e / jnp.sum(e, axis=0, keepdims=True)


def _outproj_router(o1, o2, w_out, x2, g_ffn, w_router, B, S, tm):
    T, D = o1.shape[0], g_ffn.shape[-1]
    E = w_router.shape[1]
    spare = EC_CAPACITY_FACTOR * S // E
    per = S // tm
    row = lambda width: pl.BlockSpec((tm, width), lambda i: (i, 0))
    full = lambda a: pl.BlockSpec(a.shape, lambda i: (0, 0))
    w = w_out.astype(BF16)
    g = g_ffn.reshape(1, D)
    wr = w_router.T.astype(BF16)
    return pl.pallas_call(
        _outproj_kernel,
        out_shape=(jax.ShapeDtypeStruct((T + spare, 2 * D), F32), jax.ShapeDtypeStruct((B, E, S), F32)),
        grid=(T // tm,),
        in_specs=[row(o1.shape[1]), row(o2.shape[1]), full(w), _stream_spec(x2, D, tm), full(g), full(wr)],
        out_specs=(row(2 * D), pl.BlockSpec((None, E, tm), lambda i: (i // per, 0, i % per))),
        compiler_params=_params(1), name="outproj_router")(o1, o2, w, x2, g, wr)


def _split3(x):
    x1 = x.astype(BF16)
    r = x - x1.astype(F32)
    x2 = r.astype(BF16)
    x3 = (r - x2.astype(F32)).astype(BF16)
    return x1, x2, x3


def _topk_kernel(aff_ref, idx_ref, gate_ref, thr_ref, *, cap):
    E, R, _ = aff_ref.shape
    bits = lax.bitcast_convert_type(aff_ref[...], I32)

    def count(mask):
        return jnp.sum(jnp.sum(mask.astype(F32), axis=2, keepdims=True), axis=1, keepdims=True)

    def bit_body(i, prefix):
        cand = prefix | jnp.left_shift(jnp.int32(1), 30 - i)
        return jnp.where(count(bits >= cand) >= cap, cand, prefix)

    thr = lax.fori_loop(0, 31, bit_body, jnp.zeros((E, 1, 1), I32))
    thr_ref[...] = jnp.broadcast_to(thr, thr_ref.shape)

    li = lax.broadcasted_iota(I32, (LANES, LANES), 0)
    lj = lax.broadcasted_iota(I32, (LANES, LANES), 1)
    tri = (li <= lj).astype(BF16)
    ri = lax.broadcasted_iota(I32, (R, R), 0)
    rj = lax.broadcasted_iota(I32, (R, R), 1)
    below = (rj < ri).astype(BF16)
    slot = lax.broadcasted_iota(I32, (1, cap), 1).astype(F32)
    tok = (lax.broadcasted_iota(I32, (R, LANES), 0) * LANES
           + lax.broadcasted_iota(I32, (R, LANES), 1)).astype(F32)

    def prefix_counts(m):
        within = _dot(m.astype(BF16), tri)
        total = jnp.broadcast_to(within[:, LANES - 1:LANES], (R, LANES))
        before = _dot(below, total.astype(BF16))
        return within, total, before

    def expert(e, carry):
        a = aff_ref[e]
        b = lax.bitcast_convert_type(a, I32)
        t = thr_ref[e]
        gt, eq = b > t, b == t
        n_gt = jnp.sum(jnp.sum(gt.astype(F32), axis=1, keepdims=True), axis=0, keepdims=True)
        eqf = eq.astype(F32)
        within, _, before = prefix_counts(eqf)
        sel = gt | (eq & (before + within - eqf < cap - n_gt))
        self_ = sel.astype(F32)
        within, total, before = prefix_counts(self_)
        rank = before + within
        first, count_r = before[:, 0:1], total[:, 0:1]
        owner = ((first <= slot) & (slot < first + count_r)).astype(BF16)

        pieces = [self_.astype(BF16)] + list(_split3(rank)[:2]) + list(_split3(tok)[:2]) + list(_split3(a))
        rows = _dot_tn(jnp.concatenate(pieces, axis=1), owner)
        part = lambda n: rows[n * LANES:(n + 1) * LANES]
        hit = (part(0) > 0.5) & (part(1) + part(2) == slot + 1.0)
        idx_ref[e] = jnp.sum(jnp.where(hit, part(3) + part(4), 0.0), axis=0, keepdims=True).astype(I32)
        gate_ref[e] = jnp.sum(jnp.where(hit, part(5) + part(6) + part(7), 0.0), axis=0, keepdims=True)
        return carry

    lax.fori_loop(0, E, expert, 0)


def _topk(aff, cap):
    B, E, S = aff.shape
    assert S < 2 ** 16
    R = S // LANES
    aff4 = aff.reshape(B, E, R, LANES)
    out = jax.ShapeDtypeStruct((B, E, 1, cap), I32), jax.ShapeDtypeStruct((B, E, 1, cap), F32)
    spec = pl.BlockSpec((None, E, 1, cap), lambda b: (b, 0, 0, 0))
    return pl.pallas_call(
        functools.partial(_topk_kernel, cap=cap), out_shape=out, grid=(B,),
        in_specs=[pl.BlockSpec((None, E, R, LANES), lambda b: (b, 0, 0, 0))],
        out_specs=(spec, spec), scratch_shapes=[pltpu.VMEM((E, 1, LANES), I32)],
        compiler_params=_params(1), name="topk")(aff4)


ROW_UNROLL = 8
M_BLOCKS = 2
SEM_IN, SEM_OUT = 0, 1


N_SLOTS = 3


def _moe_kernel(idx_prev_ref, idx_ref, idx_next_ref, gate_ref, wg_ref, wu_ref, wd_ref, hx_alias, hx_hbm,
                buf, sems, *, seq, cap, f_chunk, n_tokens):
    del hx_alias
    D = wg_ref.shape[0]
    n_batch = pl.num_programs(1)
    step = pl.program_id(0) * n_batch + pl.program_id(1)
    n_steps = pl.num_programs(0) * n_batch
    cur, nxt, prv = step % N_SLOTS, (step + 1) % N_SLOTS, (step + 2) % N_SLOTS
    res = pl.ds(D, D)

    def for_rows(fn):
        @pl.loop(0, cap // ROW_UNROLL)
        def _(g):
            for u in range(ROW_UNROLL):
                fn(g * ROW_UNROLL + u)

    def gather_row(t, s, j):
        pltpu.make_async_copy(hx_hbm.at[pl.ds(t, 1)], buf.at[s, pl.ds(j, 1)], sems.at[SEM_IN, s]).start()

    def write_row(t, s, j):
        pltpu.make_async_copy(buf.at[s, pl.ds(j, 1), res], hx_hbm.at[pl.ds(t, 1), res], sems.at[SEM_OUT, s]).start()

    def wait_gathers(s):
        pltpu.make_async_copy(hx_hbm.at[pl.ds(0, cap)], buf.at[s], sems.at[SEM_IN, s]).wait()

    def wait_writes(s):
        pltpu.make_async_copy(buf.at[s, :, res], hx_hbm.at[pl.ds(0, cap), res], sems.at[SEM_OUT, s]).wait()

    this_base = pl.program_id(1) * seq
    next_base = ((step + 1) % n_batch) * seq
    prev_base = ((step + n_batch - 1) % n_batch) * seq
    first = step == 0

    @pl.when(first)
    def _():
        for_rows(lambda j: gather_row(this_base + idx_ref[0, j], cur, j))
        buf[prv, :, D:] = jnp.zeros((cap, D), F32)

    @pl.when(step > 0)
    def _():
        wait_writes(nxt)

    wait_gathers(cur)
    xin = buf[cur, :, :D].astype(BF16)
    n_chunks = wg_ref.shape[1] // f_chunk
    n_rows = cap // M_BLOCKS
    ahead = cap // (n_chunks * M_BLOCKS)
    accs = [jnp.zeros((n_rows, D), F32) for _ in range(M_BLOCKS)]
    piece = 0
    for f in range(n_chunks):
        fs = slice(f * f_chunk, (f + 1) * f_chunk)
        for mb in range(M_BLOCKS):
            for j in range(piece * ahead, (piece + 1) * ahead):
                gather_row(next_base + idx_next_ref[0, j], nxt, j)
                write_row(jnp.where(first, n_tokens + j, prev_base + idx_prev_ref[0, j]), prv, j)
            piece += 1
            xs = xin[mb * n_rows:(mb + 1) * n_rows]
            g = _dot(xs, wg_ref[:, fs])
            hid = (g * jax.nn.sigmoid(g) * _dot(xs, wu_ref[:, fs])).astype(BF16)
            accs[mb] = accs[mb] + _dot(hid, wd_ref[fs, :])
    acc = jnp.concatenate(accs, axis=0)
    diag = (lax.broadcasted_iota(I32, (cap, cap), 0) == lax.broadcasted_iota(I32, (cap, cap), 1))
    gate_col = jnp.sum(jnp.where(diag, gate_ref[...], 0.0), axis=1, keepdims=True)
    buf[cur, :, D:] = buf[cur, :, D:] + acc * gate_col

    @pl.when(step == n_steps - 1)
    def _():
        wait_gathers(nxt)
        wait_writes(prv)
        for_rows(lambda j: write_row(this_base + idx_ref[0, j], cur, j))
        wait_writes(cur)


def _moe_ffn(hx, idx, gate, w_gate, w_up, w_down, B, S, f_chunk=512):
    E, D, F = w_gate.shape
    cap = idx.shape[-1]
    T = hx.shape[0] - cap
    f_chunk = min(f_chunk, F)
    assert B >= N_SLOTS and cap % ROW_UNROLL == 0 and cap % (M_BLOCKS * (F // f_chunk)) == 0
    wspec = lambda a: pl.BlockSpec((None,) + a.shape[1:], lambda e, b: (e, 0, 0), pipeline_mode=pl.Buffered(1))
    any_spec = pl.BlockSpec(memory_space=pl.ANY)

    def ids(shift):
        def index_map(e, b):
            s = jnp.clip(e * B + b + shift, 0, E * B - 1)
            return (s % B, s // B, 0, 0)
        return pl.BlockSpec((None, None, 1, cap), index_map, memory_space=pltpu.SMEM)

    return pl.pallas_call(
        functools.partial(_moe_kernel, seq=S, cap=cap, f_chunk=f_chunk, n_tokens=T),
        out_shape=jax.ShapeDtypeStruct(hx.shape, F32), grid=(E, B),
        in_specs=[ids(-1), ids(0), ids(1),
                  pl.BlockSpec((None, None, 1, cap), lambda e, b: (b, e, 0, 0)),
                  wspec(w_gate), wspec(w_up), wspec(w_down), any_spec],
        out_specs=any_spec,
        scratch_shapes=[pltpu.VMEM((N_SLOTS, cap, 2 * D), F32), pltpu.SemaphoreType.DMA((2, N_SLOTS))],
        input_output_aliases={7: 0},
        compiler_params=_params(2), name="moe_ffn")(idx, idx, idx, gate, w_gate, w_up, w_down, hx)


def _cast_kernel(*refs):
    n = len(refs) // 2
    for src, dst in zip(refs[:n], refs[n:]):
        dst[...] = src[...].astype(dst.dtype)


WEIGHT_SPLIT = 2


def _expert_weights_bf16(layer, w_gate, w_up, w_down):
    _, E, D, F = w_gate.shape
    fh = F // WEIGHT_SPLIT
    cols = lambda: pl.BlockSpec((None, None, D, fh), lambda e, h: (layer, e, 0, h))
    rows = lambda: pl.BlockSpec((None, None, fh, D), lambda e, h: (layer, e, h, 0))
    return pl.pallas_call(
        _cast_kernel,
        out_shape=(jax.ShapeDtypeStruct((E, D, F), BF16), jax.ShapeDtypeStruct((E, D, F), BF16),
                   jax.ShapeDtypeStruct((E, F, D), BF16)),
        grid=(E, WEIGHT_SPLIT), in_specs=[cols(), cols(), rows()],
        out_specs=(pl.BlockSpec((None, D, fh), lambda e, h: (e, 0, h)),
                   pl.BlockSpec((None, D, fh), lambda e, h: (e, 0, h)),
                   pl.BlockSpec((None, fh, D), lambda e, h: (e, h, 0))),
        compiler_params=_params(2), name="cast_weights")(w_gate, w_up, w_down)


def moe_layer(o1, o2, w_out, x2, g_ffn, w_router, expert_weights, B, S, tm=512):
    hx, aff = _outproj_router(o1, o2, w_out, x2, g_ffn, w_router, B, S, tm)
    cap = EC_CAPACITY_FACTOR * S // w_router.shape[1]
    idx, gate = _topk(aff, cap)
    return _moe_ffn(hx, idx, gate, *expert_weights, B, S)


def _final_norm_kernel(x_ref, g_ref, o_ref):
    o_ref[...] = _rms(x_ref[...], g_ref[...])


def _final_norm(x2, g, tm, T):
    D = g.shape[-1]
    row = pl.BlockSpec((tm, D), lambda i: (i, 0))
    return pl.pallas_call(
        _final_norm_kernel, out_shape=jax.ShapeDtypeStruct((T, D), F32), grid=(T // tm,),
        in_specs=[_stream_spec(x2, D, tm), pl.BlockSpec((1, D), lambda i: (0, 0))], out_specs=row,
        compiler_params=_params(1), name="final_norm")(x2, g.reshape(1, D))


ODD_SPLITS = (C_Q_LORA, C_KV_LORA, C_ROPE, D_W, D_W, D_W, D_W, 4 * D_HEADS)
N_GATES = 4 * D_HEADS
I_FWD, F_FWD, I_BWD, F_BWD = 0, D_HEADS, 2 * D_HEADS, 3 * D_HEADS


def _odd_weights(w_in):
    cq, ckv, kr, dq, dk, dv, do, dg = jnp.split(w_in, [int(c) for c in np.cumsum(ODD_SPLITS)[:-1]], axis=1)
    D = w_in.shape[0]
    zeros = lambda n: jnp.zeros((D, n), w_in.dtype)
    kr_slot = jnp.concatenate([zeros(C_NOPE), kr, zeros(C_SLOT - C_NOPE - C_ROPE)], axis=1)
    dg_slot = jnp.concatenate([dg, zeros(LANES - N_GATES)], axis=1)
    w = jnp.concatenate([cq, ckv, kr_slot, dq, dk, dv, do, dg_slot], axis=1)
    return w.astype(BF16), dg.T.astype(BF16)


def _inproj_odd_kernel(x_ref, g_ref, w_ref, wgt_ref,
                       cq_ref, ckv_ref, kr_ref, dqk_ref, dv_ref, do_ref, gc_ref, gt_ref):
    h = _rms(x_ref[...], g_ref[...]).astype(BF16)
    off = 0
    for ref, width in ((cq_ref, C_Q_LORA), (ckv_ref, C_KV_LORA), (kr_ref, C_SLOT), (dqk_ref, 2 * D_W),
                       (dv_ref, D_W), (do_ref, D_W), (gc_ref, LANES)):
        ref[...] = _dot(h, w_ref[:, off:off + width]).astype(ref.dtype)
        off += width
    gt_ref[...] = _dot_nt(wgt_ref[...], h)


def _inproj_odd(x2, g, w, wgt, tm, T):
    D = g.shape[-1]
    row = lambda width: pl.BlockSpec((tm, width), lambda i: (i, 0))
    full = lambda a: pl.BlockSpec(a.shape, lambda i: (0, 0))
    widths = (C_Q_LORA, C_KV_LORA, C_SLOT, 2 * D_W, D_W, D_W, LANES)
    dtypes = (F32, F32, F32, F32, BF16, F32, F32)
    out_shape = tuple(jax.ShapeDtypeStruct((T, wd), dt) for wd, dt in zip(widths, dtypes))
    out_shape += (jax.ShapeDtypeStruct((N_GATES, T), F32),)
    return pl.pallas_call(
        _inproj_odd_kernel, out_shape=out_shape, grid=(T // tm,),
        in_specs=[_stream_spec(x2, D, tm), full(g), full(w), full(wgt)],
        out_specs=tuple(row(wd) for wd in widths) + (pl.BlockSpec((N_GATES, tm), lambda i: (0, i)),),
        compiler_params=_params(1), name="inproj_odd")(x2, g, w, wgt)


def _mla_prep_kernel(cq_ref, ckv_ref, kr_ref, nq_ref, nkv_ref, wq_ref, wk_ref, wv_ref, cos_ref, sin_ref,
                     q_ref, k_ref, vt_ref):
    lane = lax.broadcasted_iota(I32, (1, LANES), 1)
    first = lane < C_NOPE + C_ROPE // 2
    cos, sin = cos_ref[...], sin_ref[...]
    rope = lambda z: _rope_slab(z, cos, sin, C_ROPE // 2, first)
    q = _dot(_rms(cq_ref[...], nq_ref[...]).astype(BF16), wq_ref[...])
    hkv = _rms(ckv_ref[...], nkv_ref[...]).astype(BF16)
    kn = _dot(hkv, wk_ref[...])
    kr = rope(kr_ref[...])
    for hh in range(C_HEADS):
        slab = slice(hh * C_SLOT, (hh + 1) * C_SLOT)
        q_ref[:, slab] = (rope(q[:, slab]) * C_LOG2_SCALE).astype(BF16)
        k_ref[:, slab] = (kn[:, slab] + kr).astype(BF16)
    vt = _dot_nt(wv_ref[...], hkv)
    row = lax.broadcasted_iota(I32, vt.shape, 0)
    vt_ref[...] = jnp.where(row % C_VROWS < C_V, vt, 1.0).astype(BF16)


def _mla_prep(cq, ckv, kr, nq, nkv, w_uq, w_ukv, cos, sin, tm):
    T = cq.shape[0]
    pad_q = C_SLOT - C_NOPE - C_ROPE
    wq = jnp.pad(w_uq.reshape(C_Q_LORA, C_HEADS, C_NOPE + C_ROPE), ((0, 0), (0, 0), (0, pad_q)))
    wq = wq.reshape(C_Q_LORA, C_HEADS * C_SLOT).astype(BF16)
    wkv = w_ukv.reshape(C_KV_LORA, C_HEADS, C_NOPE + C_V)
    wk = jnp.pad(wkv[:, :, :C_NOPE], ((0, 0), (0, 0), (0, C_SLOT - C_NOPE)))
    wk = wk.reshape(C_KV_LORA, C_HEADS * C_SLOT).astype(BF16)
    wv = jnp.pad(wkv[:, :, C_NOPE:], ((0, 0), (0, 0), (0, C_VROWS - C_V)))
    wv = wv.reshape(C_KV_LORA, C_HEADS * C_VROWS).T.astype(BF16)
    nq, nkv = nq.reshape(1, -1), nkv.reshape(1, -1)
    row = lambda width: pl.BlockSpec((tm, width), lambda i: (i, 0))
    full = lambda a: pl.BlockSpec(a.shape, lambda i: (0, 0))
    slots = jax.ShapeDtypeStruct((T, C_HEADS * C_SLOT), BF16)
    vt_shape = jax.ShapeDtypeStruct((T // tm, C_HEADS * C_VROWS, tm), BF16)
    return pl.pallas_call(
        _mla_prep_kernel, out_shape=(slots, slots, vt_shape), grid=(T // tm,),
        in_specs=[row(C_Q_LORA), row(C_KV_LORA), row(C_SLOT), full(nq), full(nkv), full(wq), full(wk), full(wv),
                  row(LANES), row(LANES)],
        out_specs=(row(C_HEADS * C_SLOT), row(C_HEADS * C_SLOT),
                   pl.BlockSpec((None, C_HEADS * C_VROWS, tm), lambda i: (i, 0, 0))),
        compiler_params=_params(1), name="mla_prep")(cq, ckv, kr, nq, nkv, wq, wk, wv, cos, sin)


def _mla_attn_kernel(q_ref, k_ref, vt_ref, o_ref):
    tq = q_ref.shape[0]
    n_chunks, _, key_chunk = vt_ref.shape
    slabs = [slice(hh * C_SLOT, (hh + 1) * C_SLOT) for hh in range(C_PAIR)]
    vrows = [slice(hh * C_VROWS, (hh + 1) * C_VROWS) for hh in range(C_PAIR)]
    qs = [q_ref[:, slab] for slab in slabs]

    def scores(hh, c):
        st = _dot_nt(k_ref[c * key_chunk:(c + 1) * key_chunk, slabs[hh]], qs[hh])
        return st, jnp.max(st, axis=0, keepdims=True)

    m = [jnp.full((1, tq), NEG, F32) for _ in range(C_PAIR)]
    acc = [jnp.zeros((C_VROWS, tq), F32) for _ in range(C_PAIR)]
    st = [scores(hh, 0) for hh in range(C_PAIR)]
    for c in range(n_chunks):
        for hh in range(C_PAIR):
            st_next = scores(hh, c + 1) if c + 1 < n_chunks else None
            m_new = jnp.maximum(m[hh], st[hh][1])
            pt = jnp.exp2((st[hh][0] - m_new).astype(BF16))
            acc[hh] = jnp.exp2(m[hh] - m_new) * acc[hh] + _dot(vt_ref[c, vrows[hh], :], pt)
            m[hh], st[hh] = m_new, st_next
    for hh in range(C_PAIR):
        ot = acc[hh][:C_V] / acc[hh][C_V:C_V + 1]
        o = jnp.concatenate([ot, jnp.zeros((LANES - C_V, tq), F32)], axis=0).T
        o_ref[:, hh * C_V:(hh + 1) * C_V] = o[:, :C_V].astype(BF16)


def _mla_attn(q, k, vt, B, S, tq):
    T = B * S
    nq = S // tq
    key_chunk = vt.shape[-1]
    vt = vt.reshape(B, S // key_chunk, C_HEADS * C_VROWS, key_chunk)
    return pl.pallas_call(
        _mla_attn_kernel,
        out_shape=jax.ShapeDtypeStruct((T, C_HEADS * C_V), BF16), grid=(B, C_HEADS // C_PAIR, nq),
        in_specs=[pl.BlockSpec((tq, C_PAIR * C_SLOT), lambda b, p, i: (b * nq + i, p)),
                  pl.BlockSpec((S, C_PAIR * C_SLOT), lambda b, p, i: (b, p)),
                  pl.BlockSpec((None, S // key_chunk, C_PAIR * C_VROWS, key_chunk), lambda b, p, i: (b, 0, p, 0))],
        out_specs=pl.BlockSpec((tq, C_PAIR * C_V), lambda b, p, i: (b * nq + i, p)),
        compiler_params=_params(3), name="mla_attn")(q, k, vt)


HALO = 8


def _conv_kernel(xp_ref, x_ref, xn_ref, w_ref, o_ref, *, n_tiles):
    j = pl.program_id(1)
    tc = x_ref.shape[0]
    prev = jnp.where(j > 0, xp_ref[...], 0.0)
    nxt = jnp.where(j < n_tiles - 1, xn_ref[...], 0.0)
    ext = jnp.concatenate([prev, x_ref[...], nxt], axis=0)
    rows = tc + 2 * HALO
    y = jnp.zeros(x_ref.shape, F32)
    for w in range(D_CONV):
        first = HALO - D_CONV // 2 + w
        y = y + pltpu.roll(ext, (rows - first) % rows, 0)[:tc] * w_ref[w:w + 1, :]
    y = y * jax.nn.sigmoid(y)
    o_ref[:, :D_W] = y[:, :D_W]
    o_ref[:, D_W:] = y[:, D_W:] * (D_HEAD_DIM ** -0.5)


def _conv_prep(dqk, conv_w, B, S, tc):
    T, C = dqk.shape
    n_tiles = S // tc
    per, last = tc // HALO, T // HALO - 1
    cur = lambda b, j: (b * n_tiles + j, 0)
    return pl.pallas_call(
        functools.partial(_conv_kernel, n_tiles=n_tiles),
        out_shape=jax.ShapeDtypeStruct((T, C), F32), grid=(B, n_tiles),
        in_specs=[pl.BlockSpec((HALO, C), lambda b, j: (jnp.maximum((b * n_tiles + j) * per - 1, 0), 0)),
                  pl.BlockSpec((tc, C), cur),
                  pl.BlockSpec((HALO, C), lambda b, j: (jnp.minimum((b * n_tiles + j + 1) * per, last), 0)),
                  pl.BlockSpec(conv_w.shape, lambda b, j: (0, 0))],
        out_specs=pl.BlockSpec((tc, C), cur),
        compiler_params=_params(2), name="conv_prep")(dqk, dqk, dqk, conv_w)


def _log_sigmoid(x):
    return -(jnp.maximum(-x, 0.0) + jnp.log1p(jnp.exp(-jnp.abs(x))))


def _tri(lower):
    a = lax.broadcasted_iota(I32, (BLOCK, BLOCK), 0)
    b = lax.broadcasted_iota(I32, (BLOCK, BLOCK), 1)
    return ((b <= a) if lower else (b >= a)).astype(BF16)


def _mlstm_update(s_ref, m_ref, h, g, a, k, v_ones):
    a_max = jnp.max(a, axis=0, keepdims=True)
    w = jnp.exp(a - a_max)
    upd = _dot_tn(v_ones, (k * w).astype(BF16))
    m = m_ref[h:h + 1, 0:1]
    m_new = jnp.maximum(g + m, a_max)
    rows = slice(h * 2 * D_HEAD_DIM, (h + 1) * 2 * D_HEAD_DIM)
    s_ref[rows, :] = jnp.exp(g + m - m_new) * s_ref[rows, :] + jnp.exp(a_max - m_new) * upd
    m_ref[h:h + 1, :] = jnp.broadcast_to(m_new, (1, LANES))


def _mlstm_bwd_state_kernel(bias_c_ref, qk_ref, v_ref, gc_ref, s_out_ref, m_out_ref, s_ref, m_ref):
    @pl.when(pl.program_id(1) == 0)
    def _():
        s_ref[...] = jnp.zeros_like(s_ref)
        m_ref[...] = jnp.zeros_like(m_ref)

    s_out_ref[...] = s_ref[...]
    m_out_ref[...] = m_ref[...]
    n_seq = qk_ref.shape[0]
    gc = [gc_ref[i] + bias_c_ref[...] for i in range(n_seq)]
    lf = jnp.concatenate([_log_sigmoid(x) for x in gc], axis=1)
    suffix = sum(_dot(_tri(False), p) for p in _split3(lf))
    ones = jnp.ones((BLOCK, D_HEAD_DIM), BF16)
    for h in range(D_HEADS):
        hs = slice(h * D_HEAD_DIM, (h + 1) * D_HEAD_DIM)
        for i in range(n_seq):
            lane = i * LANES + F_BWD + h
            sb = suffix[:, lane:lane + 1]
            g = sb[0:1, :]
            a = g - sb + gc[i][:, I_BWD + h:I_BWD + h + 1]
            k = qk_ref[i, :, D_W + h * D_HEAD_DIM:D_W + (h + 1) * D_HEAD_DIM]
            _mlstm_update(s_ref.at[i], m_ref.at[i], h, g, a, k, jnp.concatenate([v_ref[i, :, hs], ones], axis=1))


def _mlstm_bwd_state(bias_c, qk, dv, gc, B, S):
    N = S // BLOCK
    nb = SEQS_PER_STEP if B % SEQS_PER_STEP == 0 else 1
    per_seq = lambda a: a.reshape(B, S, a.shape[-1])
    rev = lambda width: pl.BlockSpec((nb, BLOCK, width), lambda b, n: (b, N - 1 - n, 0))
    out_shape = (jax.ShapeDtypeStruct((B, N, 2 * D_W, D_HEAD_DIM), F32), jax.ShapeDtypeStruct((B, N, 8, LANES), F32))
    return pl.pallas_call(
        _mlstm_bwd_state_kernel, out_shape=out_shape, grid=(B // nb, N),
        in_specs=[pl.BlockSpec(bias_c.shape, lambda b, n: (0, 0)), rev(2 * D_W), rev(D_W), rev(LANES)],
        out_specs=(pl.BlockSpec((nb, None, 2 * D_W, D_HEAD_DIM), lambda b, n: (b, N - 1 - n, 0, 0)),
                   pl.BlockSpec((nb, None, 8, LANES), lambda b, n: (b, N - 1 - n, 0, 0))),
        scratch_shapes=[pltpu.VMEM((nb, 2 * D_W, D_HEAD_DIM), F32), pltpu.VMEM((nb, 8, LANES), F32)],
        compiler_params=_params(2), name="mlstm_bwd_state")(bias_c, per_seq(qk), per_seq(dv), per_seq(gc))


def _mlstm_main_kernel(bias_c_ref, bias_r_ref, qk_ref, v_ref, og_ref, gc_ref, gt_ref, sb_ref, mb_ref,
                       o_ref, s_ref, m_ref):
    @pl.when(pl.program_id(1) == 0)
    def _():
        s_ref[...] = jnp.zeros_like(s_ref)
        m_ref[...] = jnp.zeros_like(m_ref)

    gc = gc_ref[...] + bias_c_ref[...]
    gr = gt_ref[...] + bias_r_ref[...]
    lower, upper = _tri(True), _tri(False)
    lfc, lfr = _split3(_log_sigmoid(gc)), _split3(_log_sigmoid(gr))
    pre_c = sum(_dot(lower, p) for p in lfc)
    suf_c = sum(_dot(upper, p) for p in lfc)
    pre_r = sum(_dot(p, upper) for p in lfr)
    suf_r = sum(_dot(p, lower) for p in lfr)
    src = lax.broadcasted_iota(I32, (BLOCK, BLOCK), 0)
    qry = lax.broadcasted_iota(I32, (BLOCK, BLOCK), 1)
    ones = jnp.ones((BLOCK, D_HEAD_DIM), BF16)
    og = og_ref[...]
    for h in range(D_HEADS):
        hs = slice(h * D_HEAD_DIM, (h + 1) * D_HEAD_DIM)
        rows = slice(h * 2 * D_HEAD_DIM, (h + 1) * 2 * D_HEAD_DIM)
        k = qk_ref[:, D_W + h * D_HEAD_DIM:D_W + (h + 1) * D_HEAD_DIM]
        qb, v = qk_ref[:, hs].astype(BF16), v_ref[:, hs]
        kq = _dot_nt(k.astype(BF16), qb)

        def direction(b_row, c_col, mask, state_t, m_prev):
            logd = jnp.where(mask, b_row + c_col, NEG)
            log_inter = b_row + m_prev
            m_t = jnp.maximum(jnp.max(logd, axis=0, keepdims=True), log_inter)
            s = kq * jnp.exp(logd - m_t)
            inter_w = jnp.exp(log_inter - m_t)
            qs = _dot_nt(state_t.astype(BF16), qb)
            num = _dot_tn(v, s.astype(BF16)) + inter_w * qs[:D_HEAD_DIM, :]
            den = jnp.sum(s, axis=0, keepdims=True) + inter_w * qs[D_HEAD_DIM:D_HEAD_DIM + 1, :]
            return num / jnp.maximum(jnp.abs(den), jnp.exp(-m_t))

        f, bk = F_FWD + h, F_BWD + h
        h_f = direction(pre_r[f:f + 1, :], gc[:, I_FWD + h:I_FWD + h + 1] - pre_c[:, f:f + 1], src <= qry,
                        s_ref[rows, :], m_ref[h:h + 1, 0:1])
        h_b = direction(suf_r[bk:bk + 1, :], gc[:, I_BWD + h:I_BWD + h + 1] - suf_c[:, bk:bk + 1], src > qry,
                        sb_ref[rows, :], mb_ref[h:h + 1, 0:1])
        o_ref[:, hs] = (jax.nn.sigmoid(og[:, hs]) * (h_f + h_b).T).astype(BF16)
        b_col = pre_c[:, f:f + 1]
        g = b_col[BLOCK - 1:BLOCK, :]
        a = g - b_col + gc[:, I_FWD + h:I_FWD + h + 1]
        _mlstm_update(s_ref, m_ref, h, g, a, k, jnp.concatenate([v, ones], axis=1))


def _mlstm_main(bias_c, bias_r, qk, dv, og, gc, gt, sb, mb, B, S):
    N = S // BLOCK
    T = B * S
    cur = lambda b, n: (b * N + n, 0)
    return pl.pallas_call(
        _mlstm_main_kernel, out_shape=jax.ShapeDtypeStruct((T, D_W), BF16), grid=(B, N),
        in_specs=[pl.BlockSpec(bias_c.shape, lambda b, n: (0, 0)), pl.BlockSpec(bias_r.shape, lambda b, n: (0, 0)),
                  pl.BlockSpec((BLOCK, 2 * D_W), cur), pl.BlockSpec((BLOCK, D_W), cur), pl.BlockSpec((BLOCK, D_W), cur),
                  pl.BlockSpec((BLOCK, LANES), cur), pl.BlockSpec((N_GATES, BLOCK), lambda b, n: (0, b * N + n)),
                  pl.BlockSpec((None, None, 2 * D_W, D_HEAD_DIM), lambda b, n: (b, n, 0, 0)),
                  pl.BlockSpec((None, None, 8, LANES), lambda b, n: (b, n, 0, 0))],
        out_specs=pl.BlockSpec((BLOCK, D_W), cur),
        scratch_shapes=[pltpu.VMEM((2 * D_W, D_HEAD_DIM), F32), pltpu.VMEM((8, LANES), F32)],
        compiler_params=_params(2), name="mlstm_main")(bias_c, bias_r, qk, dv, og, gc, gt, sb, mb)


def odd_mixer_parts(x2, positions, g_mix, w_in, norm_q, norm_kv, w_uq, w_ukv, conv_w, gate_bias, B, S,
                    tm=512, tq=256, key_chunk=1024):
    w, wgt = _odd_weights(w_in)
    cq, ckv, kr, dqk, dv, og, gc, gt = _inproj_odd(x2, g_mix.reshape(1, -1), w, wgt, tm, B * S)
    cos, sin = _rope_tables(positions, MLA_THETA, C_SLOT, C_NOPE, C_ROPE)
    q, k, vt = _mla_prep(cq, ckv, kr, norm_q, norm_kv, w_uq, w_ukv, cos, sin, min(key_chunk, S))
    o_c = _mla_attn(q, k, vt, B, S, min(tq, S))
    qk = _conv_prep(dqk, conv_w.astype(F32), B, S, min(tm, S))
    bias = gate_bias.astype(F32).reshape(1, N_GATES)
    bias_c = jnp.pad(bias, ((0, 0), (0, LANES - N_GATES)))
    bias_r = bias.reshape(N_GATES, 1)
    sb, mb = _mlstm_bwd_state(bias_c, qk, dv, gc, B, S)
    o_d = _mlstm_main(bias_c, bias_r, qk, dv, og, gc, gt, sb, mb, B, S)
    return o_c, o_d


def kernel(x, positions, norm_mix, norm_ffn, norm_final, ev_w_in, ev_w_out, attn_sink, ret_decay_logit,
           od_w_in, od_w_out, mla_norm_q, mla_norm_kv, mla_w_uq, mla_w_ukv, mlstm_conv, mlstm_gate_bias,
           moe_router, moe_w_gate, moe_w_up, moe_w_down):
    B, S, D = x.shape
    tm = min(512, S)
    x2 = x.reshape(B * S, D)
    for layer in range(norm_mix.shape[0]):
        j = layer // 2
        if layer % 2 == 0:
            o1, o2 = even_mixer_parts(x2, positions, norm_mix[layer], ev_w_in[j], attn_sink[j],
                                      ret_decay_logit[j], B, S, tm)
            w_out = ev_w_out[j]
        else:
            o1, o2 = odd_mixer_parts(x2, positions, norm_mix[layer], od_w_in[j], mla_norm_q[j], mla_norm_kv[j],
                                     mla_w_uq[j], mla_w_ukv[j], mlstm_conv[j], mlstm_gate_bias[j], B, S, tm)
            w_out = od_w_out[j]
        x2 = moe_layer(o1, o2, w_out, x2, norm_ffn[layer], moe_router[layer],
                       _expert_weights_bf16(layer, moe_w_gate, moe_w_up, moe_w_down), B, S, tm)
    return _final_norm(x2, norm_final, tm, B * S).reshape(B, S, D)
```

```python
import functools

import jax
import jax.numpy as jnp
import numpy as np
from jax import lax
from jax.experimental import pallas as pl
from jax.experimental.pallas import tpu as pltpu

F32 = jnp.float32
BF16 = jnp.bfloat16
I32 = jnp.int32

LANES = 128
BLOCK = 128
RMS_EPS = 1e-6
GN_EPS = 1e-5
NEG = -1e30
VMEM_LIMIT = 56 * 1024 * 1024

A_HEADS, A_KV_HEADS, A_HEAD_DIM = 8, 2, 64
ROPE_THETA = 500000.0
ROPE_DIM = A_HEAD_DIM // 4
B_HEADS, B_HEAD_DIM = 8, 64
RET_THETA = 10000.0
C_HEADS, C_NOPE, C_ROPE, C_V = 8, 64, 32, 64
C_Q_LORA, C_KV_LORA = 512, 256
MLA_THETA = 10000.0
D_HEADS, D_HEAD_DIM, D_CONV = 4, 128, 5
N_EXPERTS = 16
EC_CAPACITY_FACTOR = 2

A_Q_W = A_HEADS * A_HEAD_DIM
A_KV_W = A_KV_HEADS * A_HEAD_DIM
B_W = B_HEADS * B_HEAD_DIM
D_W = D_HEADS * D_HEAD_DIM
C_SLOT = 128
C_PAIR = 4
BF16_ROWS = 16
C_VROWS = C_V + BF16_ROWS
C_LOG2_SCALE = float((C_NOPE + C_ROPE) ** -0.5 * np.log2(np.e))


def _params(n_axes, vmem=VMEM_LIMIT):
    return pltpu.CompilerParams(dimension_semantics=("arbitrary",) * n_axes,
                                vmem_limit_bytes=vmem)


def _rms(x, g):
    return x * lax.rsqrt(jnp.mean(x * x, axis=-1, keepdims=True) + RMS_EPS) * g


def _dot(a, b):
    return jnp.dot(a, b, preferred_element_type=F32)


def _dot_nt(a, b):
    return lax.dot_general(a, b, (((1,), (1,)), ((), ())), preferred_element_type=F32)


def _dot_tn(a, b):
    return lax.dot_general(a, b, (((0,), (0,)), ((), ())), preferred_element_type=F32)


def _rope_slab(z, cos, sin, half, first_half):
    partner = jnp.where(first_half, pltpu.roll(z, LANES - half, 1), pltpu.roll(z, half, 1))
    return z * cos + partner * sin


def _rope_tables(positions, theta, head_dim, rot_start, rot_dim):
    half = rot_dim // 2
    assert head_dim % half == 0 and rot_start % half == 0
    inv_freq = theta ** (-jnp.arange(half, dtype=F32) * 2.0 / rot_dim)
    d = np.arange(LANES) % head_dim - rot_start
    rot = (d >= 0) & (d < rot_dim)
    sign = np.where(d < half, -1.0, 1.0).astype(np.float32)
    ang = positions.astype(F32).reshape(-1, 1) * inv_freq
    ang = jnp.tile(ang, (1, LANES // half))
    cos = jnp.where(rot[None, :], jnp.cos(ang), 1.0)
    sin = jnp.where(rot[None, :], jnp.sin(ang) * sign[None, :], 0.0)
    return cos, sin


def _inproj_even_kernel(x_ref, g_ref, w_ref, ca_ref, sa_ref, cb_ref, sb_ref,
                        aq_ref, akv_ref, bq_ref, bk_ref, bv_ref, bg_ref):
    h = _rms(x_ref[...], g_ref[...]).astype(BF16)
    lane = lax.broadcasted_iota(I32, (1, LANES), 1)
    first_a = (lane % A_HEAD_DIM) < (ROPE_DIM // 2)
    first_b = (lane % B_HEAD_DIM) < (B_HEAD_DIM // 2)
    ca, sa, cb, sb = ca_ref[...], sa_ref[...], cb_ref[...], sb_ref[...]

    def rope_a(z):
        return _rope_slab(z, ca, sa, ROPE_DIM // 2, first_a)

    def rope_b(z):
        return _rope_slab(z, cb, sb, B_HEAD_DIM // 2, first_b)

    off = 0
    z = _dot(h, w_ref[:, off:off + A_Q_W])
    for s in range(A_Q_W // LANES):
        aq_ref[:, s * LANES:(s + 1) * LANES] = rope_a(z[:, s * LANES:(s + 1) * LANES]).astype(BF16)
    off += A_Q_W
    z = _dot(h, w_ref[:, off:off + 2 * A_KV_W])
    akv_ref[:, :A_KV_W] = rope_a(z[:, :A_KV_W]).astype(BF16)
    akv_ref[:, A_KV_W:] = z[:, A_KV_W:].astype(BF16)
    off += 2 * A_KV_W
    z = _dot(h, w_ref[:, off:off + B_W])
    for s in range(B_W // LANES):
        bq_ref[:, s * LANES:(s + 1) * LANES] = rope_b(z[:, s * LANES:(s + 1) * LANES])
    off += B_W
    z = _dot(h, w_ref[:, off:off + B_W])
    for s in range(B_W // LANES):
        bk_ref[:, s * LANES:(s + 1) * LANES] = rope_b(z[:, s * LANES:(s + 1) * LANES]) * (B_HEAD_DIM ** -0.5)
    off += B_W
    bv_ref[...] = _dot(h, w_ref[:, off:off + B_W]).astype(BF16)
    off += B_W
    bg_ref[...] = _dot(h, w_ref[:, off:off + B_W])


def _stream_spec(xs, D, tm):
    col = xs.shape[1] // D - 1
    return pl.BlockSpec((tm, D), lambda i: (i, col))


def _inproj_even(x2, g, w, tabs, tm):
    T, D = tabs[0].shape[0], g.shape[-1]
    ncol = w.shape[1]
    row = lambda width: pl.BlockSpec((tm, width), lambda i: (i, 0))
    full = lambda a: pl.BlockSpec(a.shape, lambda i: (0, 0))
    out_shape = (jax.ShapeDtypeStruct((T, A_Q_W), BF16), jax.ShapeDtypeStruct((T, 2 * A_KV_W), BF16),
                 jax.ShapeDtypeStruct((T, B_W), F32), jax.ShapeDtypeStruct((T, B_W), F32),
                 jax.ShapeDtypeStruct((T, B_W), BF16), jax.ShapeDtypeStruct((T, B_W), F32))
    return pl.pallas_call(
        _inproj_even_kernel, out_shape=out_shape, grid=(T // tm,),
        in_specs=[_stream_spec(x2, D, tm), full(g), pl.BlockSpec((D, ncol), lambda i: (0, 0))] + [row(LANES)] * 4,
        out_specs=(row(A_Q_W), row(2 * A_KV_W), row(B_W), row(B_W), row(B_W), row(B_W)),
        compiler_params=_params(1), name="inproj_even")(x2, g, w, *tabs)


def _win_attn_kernel(sink_ref, q_ref, kvp_ref, kvc_ref, kvn_ref, o_ref, *, n_blocks):
    n = pl.program_id(1)
    group = A_HEADS // A_KV_HEADS
    cols = group * BLOCK
    kj = lax.broadcasted_iota(I32, (BLOCK, cols), 0)
    qi = lax.broadcasted_iota(I32, (BLOCK, cols), 1) % BLOCK
    ok_prev = (kj >= qi) & (n > 0)
    ok_next = (kj <= qi) & (n < n_blocks - 1)
    scale = A_HEAD_DIM ** -0.5
    for s_i, g in [(s_i, g) for g in range(A_KV_HEADS) for s_i in range(q_ref.shape[0])]:
        q, kvp, kvc, kvn = q_ref[s_i], kvp_ref.at[s_i], kvc_ref.at[s_i], kvn_ref.at[s_i]
        ks = slice(g * A_HEAD_DIM, (g + 1) * A_HEAD_DIM)
        vs = slice(A_KV_W + g * A_HEAD_DIM, A_KV_W + (g + 1) * A_HEAD_DIM)
        qg = jnp.concatenate([q[:, (g * group + i) * A_HEAD_DIM:(g * group + i + 1) * A_HEAD_DIM]
                              for i in range(group)], axis=0)
        sink = jnp.concatenate([jnp.full((1, BLOCK), sink_ref[g * group + i], F32)
                                for i in range(group)], axis=1)
        sp = jnp.where(ok_prev, _dot_nt(kvp[:, ks], qg) * scale, NEG)
        sc = _dot_nt(kvc[:, ks], qg) * scale
        sn = jnp.where(ok_next, _dot_nt(kvn[:, ks], qg) * scale, NEG)
        m = jnp.maximum(jnp.maximum(jnp.max(sp, 0, keepdims=True), jnp.max(sc, 0, keepdims=True)),
                        jnp.maximum(jnp.max(sn, 0, keepdims=True), sink))
        pp, pc, pn = jnp.exp(sp - m), jnp.exp(sc - m), jnp.exp(sn - m)
        den = (jnp.sum(pp, 0, keepdims=True) + jnp.sum(pc, 0, keepdims=True)
               + jnp.sum(pn, 0, keepdims=True) + jnp.exp(sink - m))
        ot = (_dot_tn(kvp[:, vs], pp.astype(BF16)) + _dot_tn(kvc[:, vs], pc.astype(BF16))
              + _dot_tn(kvn[:, vs], pn.astype(BF16))) / den
        o = ot.T
        for i in range(group):
            hh = g * group + i
            o_ref[s_i, :, hh * A_HEAD_DIM:(hh + 1) * A_HEAD_DIM] = o[i * BLOCK:(i + 1) * BLOCK].astype(BF16)


def _win_attn(aq, akv, sink, B, S):
    N = S // BLOCK
    nb = SEQS_PER_STEP if B % SEQS_PER_STEP == 0 else 1
    per_seq = lambda a: a.reshape(B, S, a.shape[-1])
    kv = lambda shift: pl.BlockSpec(
        (nb, BLOCK, 2 * A_KV_W), lambda b, n: (b, jnp.clip(n + shift, 0, N - 1), 0))
    qo = lambda: pl.BlockSpec((nb, BLOCK, A_Q_W), lambda b, n: (b, n, 0))
    out = pl.pallas_call(
        functools.partial(_win_attn_kernel, n_blocks=N),
        out_shape=jax.ShapeDtypeStruct((B, S, A_Q_W), BF16), grid=(B // nb, N),
        in_specs=[pl.BlockSpec(memory_space=pltpu.SMEM), qo(), kv(-1), kv(0), kv(1)],
        out_specs=qo(),
        compiler_params=_params(2), name="win_attn")(sink, per_seq(aq), per_seq(akv), per_seq(akv), per_seq(akv))
    return out.reshape(B * S, A_Q_W)


def _ret_tables(lg_ref, lgl_ref, d_ref, xif_ref, xib_ref, zf_ref, zb_ref):
    j = lax.broadcasted_iota(I32, (BLOCK, BLOCK), 0)
    i = lax.broadcasted_iota(I32, (BLOCK, BLOCK), 1)
    rel = (i - j).astype(F32)
    for h in range(B_HEADS):
        d_ref[h] = jnp.where(i >= j, jnp.exp(rel * lg_ref[0, h]), jnp.exp(-rel * lg_ref[1, h]))
    t = lax.broadcasted_iota(I32, (BLOCK, B_W), 0).astype(F32)
    lgf, lgb = lgl_ref[0:1, :], lgl_ref[1:2, :]
    xif_ref[...] = jnp.exp((t + 1.0) * lgf)
    zf_ref[...] = jnp.exp((BLOCK - 1.0 - t) * lgf)
    xib_ref[...] = jnp.exp((BLOCK - t) * lgb)
    zb_ref[...] = jnp.exp(t * lgb)


def _ret_bwd_state_kernel(lgl_ref, k_ref, v_ref, r_out_ref, r_ref):
    n = pl.program_id(1)

    @pl.when(n == 0)
    def _():
        r_ref[...] = jnp.zeros_like(r_ref)

    r_out_ref[...] = r_ref[...]
    lgb = lgl_ref[1:2, :]
    t = lax.broadcasted_iota(I32, (BLOCK, B_W), 0).astype(F32)
    zeta = jnp.exp(t * lgb)
    cd = jnp.exp(BLOCK * lgb)
    for i in range(k_ref.shape[0]):
        _ret_state_update(r_ref.at[i], cd, v_ref[i], (k_ref[i] * zeta).astype(BF16))


def _ret_state_update(r_ref, chunk_decay, v, kz):
    upd = _dot_tn(v, kz)
    for h in range(B_HEADS):
        hs = slice(h * B_HEAD_DIM, (h + 1) * B_HEAD_DIM)
        r_ref[hs, :] = chunk_decay[:, hs] * r_ref[hs, :] + upd[hs, hs]


def _ret_bwd_state(lgl, bk, bv, B, S):
    N = S // BLOCK
    nb = SEQS_PER_STEP if B % SEQS_PER_STEP == 0 else 1
    per_seq = lambda a: a.reshape(B, S, a.shape[-1])
    blk = lambda: pl.BlockSpec((nb, BLOCK, B_W), lambda b, n: (b, N - 1 - n, 0))
    return pl.pallas_call(
        _ret_bwd_state_kernel,
        out_shape=jax.ShapeDtypeStruct((B, N, B_W, B_HEAD_DIM), F32), grid=(B // nb, N),
        in_specs=[pl.BlockSpec(lgl.shape, lambda b, n: (0, 0)), blk(), blk()],
        out_specs=pl.BlockSpec((nb, None, B_W, B_HEAD_DIM), lambda b, n: (b, N - 1 - n, 0, 0)),
        scratch_shapes=[pltpu.VMEM((nb, B_W, B_HEAD_DIM), F32)],
        compiler_params=_params(2), name="ret_bwd_state")(lgl, per_seq(bk), per_seq(bv))


def _ret_main_kernel(lg_ref, lgl_ref, q_ref, k_ref, v_ref, g_ref, rb_ref, o_ref,
                     rf_ref, d_ref, xif_ref, xib_ref, zf_ref, zb_ref):
    b, n = pl.program_id(0), pl.program_id(1)

    @pl.when((b == 0) & (n == 0))
    def _():
        _ret_tables(lg_ref, lgl_ref, d_ref, xif_ref, xib_ref, zf_ref, zb_ref)

    @pl.when(n == 0)
    def _():
        rf_ref[...] = jnp.zeros_like(rf_ref)

    n_seq = q_ref.shape[0]
    cdf = jnp.exp(BLOCK * lgl_ref[0:1, :])
    q, k, v, gate = ([r[i] for i in range(n_seq)] for r in (q_ref, k_ref, v_ref, g_ref))
    qb = [x.astype(BF16) for x in q]
    kb = [x.astype(BF16) for x in k]
    qxf = [(x * xif_ref[...]).astype(BF16) for x in q]
    qxb = [(x * xib_ref[...]).astype(BF16) for x in q]
    kzf = [(x * zf_ref[...]).astype(BF16) for x in k]
    ynt = [[None] * B_HEADS for _ in range(n_seq)]
    for h in range(B_HEADS):
        hs = slice(h * B_HEAD_DIM, (h + 1) * B_HEAD_DIM)
        for i in range(n_seq):
            st = _dot_nt(kb[i][:, hs], qb[i][:, hs]) * d_ref[h]
            states = jnp.concatenate([rf_ref[i, hs, :], rb_ref[i, hs, :]], axis=1).astype(BF16)
            queries = jnp.concatenate([qxf[i][:, hs], qxb[i][:, hs]], axis=1)
            yt = _dot_tn(v[i][:, hs], st.astype(BF16)) + _dot_nt(states, queries)
            yc = yt - jnp.mean(yt, axis=0, keepdims=True)
            ynt[i][h] = yc * lax.rsqrt(jnp.mean(yc * yc, axis=0, keepdims=True) + GN_EPS)
    for i in range(n_seq):
        yn = jnp.concatenate(ynt[i], axis=0).T
        o_ref[i] = (gate[i] * jax.nn.sigmoid(gate[i]) * yn).astype(BF16)
        _ret_state_update(rf_ref.at[i], cdf, v[i], kzf[i])


SEQS_PER_STEP = 4


def _ret_main(lg, lgl, bq, bk, bv, bg, rb, B, S):
    N = S // BLOCK
    nb = SEQS_PER_STEP if B % SEQS_PER_STEP == 0 else 1
    per_seq = lambda a: a.reshape(B, S, a.shape[-1])
    blk = lambda: pl.BlockSpec((nb, BLOCK, B_W), lambda b, n: (b, n, 0))
    tab = lambda: pltpu.VMEM((BLOCK, B_W), F32)
    out = pl.pallas_call(
        _ret_main_kernel, out_shape=jax.ShapeDtypeStruct((B, S, B_W), BF16), grid=(B // nb, N),
        in_specs=[pl.BlockSpec(memory_space=pltpu.SMEM), pl.BlockSpec(lgl.shape, lambda b, n: (0, 0)),
                  blk(), blk(), blk(), blk(),
                  pl.BlockSpec((nb, None, B_W, B_HEAD_DIM), lambda b, n: (b, n, 0, 0))],
        out_specs=blk(),
        scratch_shapes=[pltpu.VMEM((nb, B_W, B_HEAD_DIM), F32), pltpu.VMEM((B_HEADS, BLOCK, BLOCK), F32),
                        tab(), tab(), tab(), tab()],
        compiler_params=_params(2), name="ret_main")(lg, lgl, per_seq(bq), per_seq(bk), per_seq(bv), per_seq(bg), rb)
    return out.reshape(B * S, B_W)


def even_mixer_parts(x2, positions, g_mix, w_in, sink, decay_logit, B, S, tm=512):
    ca, sa = _rope_tables(positions, ROPE_THETA, A_HEAD_DIM, 0, ROPE_DIM)
    cb, sb = _rope_tables(positions, RET_THETA, B_HEAD_DIM, 0, B_HEAD_DIM)
    aq, akv, bq, bk, bv, bg = _inproj_even(x2, g_mix.reshape(1, -1), w_in.astype(BF16), (ca, sa, cb, sb), tm)
    o_a = _win_attn(aq, akv, sink.astype(F32), B, S)
    lg = jax.nn.log_sigmoid(decay_logit.astype(F32))
    lgl = jnp.repeat(lg, B_HEAD_DIM, axis=1)
    rb = _ret_bwd_state(lgl, bk, bv, B, S)
    o_b = _ret_main(lg, lgl, bq, bk, bv, bg, rb, B, S)
    return o_a, o_b


def _outproj_kernel(o1_ref, o2_ref, w_ref, x_ref, g_ref, wr_ref, hx_ref, aff_ref):
    half = o1_ref.shape[1]
    D = x_ref.shape[1]
    x1 = x_ref[...] + (_dot(o1_ref[...], w_ref[:half, :]) + _dot(o2_ref[...], w_ref[half:, :]))
    hn = _rms(x1, g_ref[...])
    hx_ref[:, :D] = hn
    hx_ref[:, D:] = x1
    logits = _dot_nt(wr_ref[...], hn.astype(BF16))
    e = jnp.exp(logits - jnp.max(logits, axis=0, keepdims=True))
    aff_ref[...] = e / jnp.sum(e, axis=0, keepdims=True)


def _outproj_router(o1, o2, w_out, x2, g_ffn, w_router, B, S, tm):
    T, D = o1.shape[0], g_ffn.shape[-1]
    E = w_router.shape[1]
    spare = EC_CAPACITY_FACTOR * S // E
    per = S // tm
    row = lambda width: pl.BlockSpec((tm, width), lambda i: (i, 0))
    full = lambda a: pl.BlockSpec(a.shape, lambda i: (0, 0))
    w = w_out.astype(BF16)
    g = g_ffn.reshape(1, D)
    wr = w_router.T.astype(BF16)
    return pl.pallas_call(
        _outproj_kernel,
        out_shape=(jax.ShapeDtypeStruct((T + spare, 2 * D), F32), jax.ShapeDtypeStruct((B, E, S), F32)),
        grid=(T // tm,),
        in_specs=[row(o1.shape[1]), row(o2.shape[1]), full(w), _stream_spec(x2, D, tm), full(g), full(wr)],
        out_specs=(row(2 * D), pl.BlockSpec((None, E, tm), lambda i: (i // per, 0, i % per))),
        compiler_params=_params(1), name="outproj_router")(o1, o2, w, x2, g, wr)


def _split3(x):
    x1 = x.astype(BF16)
    r = x - x1.astype(F32)
    x2 = r.astype(BF16)
    x3 = (r - x2.astype(F32)).astype(BF16)
    return x1, x2, x3


def _topk_kernel(aff_ref, idx_ref, gate_ref, thr_ref, *, cap):
    E, R, _ = aff_ref.shape
    bits = lax.bitcast_convert_type(aff_ref[...], I32)

    def count(mask):
        return jnp.sum(jnp.sum(mask.astype(F32), axis=2, keepdims=True), axis=1, keepdims=True)

    def bit_body(i, prefix):
        cand = prefix | jnp.left_shift(jnp.int32(1), 30 - i)
        return jnp.where(count(bits >= cand) >= cap, cand, prefix)

    thr = lax.fori_loop(0, 31, bit_body, jnp.zeros((E, 1, 1), I32))
    thr_ref[...] = jnp.broadcast_to(thr, thr_ref.shape)

    li = lax.broadcasted_iota(I32, (LANES, LANES), 0)
    lj = lax.broadcasted_iota(I32, (LANES, LANES), 1)
    tri = (li <= lj).astype(BF16)
    ri = lax.broadcasted_iota(I32, (R, R), 0)
    rj = lax.broadcasted_iota(I32, (R, R), 1)
    below = (rj < ri).astype(BF16)
    slot = lax.broadcasted_iota(I32, (1, cap), 1).astype(F32)
    tok = (lax.broadcasted_iota(I32, (R, LANES), 0) * LANES
           + lax.broadcasted_iota(I32, (R, LANES), 1)).astype(F32)

    def prefix_counts(m):
        within = _dot(m.astype(BF16), tri)
        total = jnp.broadcast_to(within[:, LANES - 1:LANES], (R, LANES))
        before = _dot(below, total.astype(BF16))
        return within, total, before

    def expert(e, carry):
        a = aff_ref[e]
        b = lax.bitcast_convert_type(a, I32)
        t = thr_ref[e]
        gt, eq = b > t, b == t
        n_gt = jnp.sum(jnp.sum(gt.astype(F32), axis=1, keepdims=True), axis=0, keepdims=True)
        eqf = eq.astype(F32)
        within, _, before = prefix_counts(eqf)
        sel = gt | (eq & (before + within - eqf < cap - n_gt))
        self_ = sel.astype(F32)
        within, total, before = prefix_counts(self_)
        rank = before + within
        first, count_r = before[:, 0:1], total[:, 0:1]
        owner = ((first <= slot) & (slot < first + count_r)).astype(BF16)

        pieces = [self_.astype(BF16)] + list(_split3(rank)[:2]) + list(_split3(tok)[:2]) + list(_split3(a))
        rows = _dot_tn(jnp.concatenate(pieces, axis=1), owner)
        part = lambda n: rows[n * LANES:(n + 1) * LANES]
        hit = (part(0) > 0.5) & (part(1) + part(2) == slot + 1.0)
        idx_ref[e] = jnp.sum(jnp.where(hit, part(3) + part(4), 0.0), axis=0, keepdims=True).astype(I32)
        gate_ref[e] = jnp.sum(jnp.where(hit, part(5) + part(6) + part(7), 0.0), axis=0, keepdims=True)
        return carry

    lax.fori_loop(0, E, expert, 0)


def _topk(aff, cap):
    B, E, S = aff.shape
    assert S < 2 ** 16
    R = S // LANES
    aff4 = aff.reshape(B, E, R, LANES)
    out = jax.ShapeDtypeStruct((B, E, 1, cap), I32), jax.ShapeDtypeStruct((B, E, 1, cap), F32)
    spec = pl.BlockSpec((None, E, 1, cap), lambda b: (b, 0, 0, 0))
    return pl.pallas_call(
        functools.partial(_topk_kernel, cap=cap), out_shape=out, grid=(B,),
        in_specs=[pl.BlockSpec((None, E, R, LANES), lambda b: (b, 0, 0, 0))],
        out_specs=(spec, spec), scratch_shapes=[pltpu.VMEM((E, 1, LANES), I32)],
        compiler_params=_params(1), name="topk")(aff4)


ROW_UNROLL = 8
M_BLOCKS = 2
SEM_IN, SEM_OUT = 0, 1


N_SLOTS = 3


def _moe_kernel(idx_prev_ref, idx_ref, idx_next_ref, gate_ref, wg_ref, wu_ref, wd_ref, hx_alias, hx_hbm,
                buf, sems, *, seq, cap, f_chunk, n_tokens):
    del hx_alias
    D = wg_ref.shape[0]
    n_batch = pl.num_programs(1)
    step = pl.program_id(0) * n_batch + pl.program_id(1)
    n_steps = pl.num_programs(0) * n_batch
    cur, nxt, prv = step % N_SLOTS, (step + 1) % N_SLOTS, (step + 2) % N_SLOTS
    res = pl.ds(D, D)

    def for_rows(fn):
        @pl.loop(0, cap // ROW_UNROLL)
        def _(g):
            for u in range(ROW_UNROLL):
                fn(g * ROW_UNROLL + u)

    def gather_row(t, s, j, priority=0):
        pltpu.make_async_copy(hx_hbm.at[pl.ds(t, 1)], buf.at[s, pl.ds(j, 1)], sems.at[SEM_IN, s]).start(priority)

    def write_row(t, s, j, priority=0):
        pltpu.make_async_copy(buf.at[s, pl.ds(j, 1), res], hx_hbm.at[pl.ds(t, 1), res],
                              sems.at[SEM_OUT, s]).start(priority)

    def wait_gathers(s):
        pltpu.make_async_copy(hx_hbm.at[pl.ds(0, cap)], buf.at[s], sems.at[SEM_IN, s]).wait()

    def wait_writes(s):
        pltpu.make_async_copy(buf.at[s, :, res], hx_hbm.at[pl.ds(0, cap), res], sems.at[SEM_OUT, s]).wait()

    this_base = pl.program_id(1) * seq
    next_base = ((step + 1) % n_batch) * seq
    prev_base = ((step + n_batch - 1) % n_batch) * seq
    first = step == 0

    @pl.when(first)
    def _():
        for_rows(lambda j: gather_row(this_base + idx_ref[0, j], cur, j))
        buf[prv, :, D:] = jnp.zeros((cap, D), F32)

    @pl.when(step > 0)
    def _():
        wait_writes(nxt)

    wait_gathers(cur)
    xin = buf[cur, :, :D].astype(BF16)
    n_chunks = wg_ref.shape[1] // f_chunk
    n_rows = cap // M_BLOCKS
    ahead = cap // (n_chunks * M_BLOCKS)
    accs = [jnp.zeros((n_rows, D), F32) for _ in range(M_BLOCKS)]
    piece = 0
    for f in range(n_chunks):
        fs = slice(f * f_chunk, (f + 1) * f_chunk)
        for mb in range(M_BLOCKS):
            for j in range(piece * ahead, (piece + 1) * ahead):
                gather_row(next_base + idx_next_ref[0, j], nxt, j, j % 2)
                write_row(jnp.where(first, n_tokens + j, prev_base + idx_prev_ref[0, j]), prv, j, j % 2)
            piece += 1
            xs = xin[mb * n_rows:(mb + 1) * n_rows]
            g = _dot(xs, wg_ref[:, fs])
            hid = (g * jax.nn.sigmoid(g) * _dot(xs, wu_ref[:, fs])).astype(BF16)
            accs[mb] = accs[mb] + _dot(hid, wd_ref[fs, :])
    acc = jnp.concatenate(accs, axis=0)
    diag = (lax.broadcasted_iota(I32, (cap, cap), 0) == lax.broadcasted_iota(I32, (cap, cap), 1))
    gate_col = jnp.sum(jnp.where(diag, gate_ref[...], 0.0), axis=1, keepdims=True)
    buf[cur, :, D:] = buf[cur, :, D:] + acc * gate_col

    @pl.when(step == n_steps - 1)
    def _():
        wait_gathers(nxt)
        wait_writes(prv)
        for_rows(lambda j: write_row(this_base + idx_ref[0, j], cur, j))
        wait_writes(cur)


def _moe_ffn(hx, idx, gate, w_gate, w_up, w_down, B, S, f_chunk=512):
    E, D, F = w_gate.shape
    cap = idx.shape[-1]
    T = hx.shape[0] - cap
    f_chunk = min(f_chunk, F)
    assert B >= N_SLOTS and cap % ROW_UNROLL == 0 and cap % (M_BLOCKS * (F // f_chunk)) == 0
    wspec = lambda a: pl.BlockSpec((None,) + a.shape[1:], lambda e, b: (e, 0, 0), pipeline_mode=pl.Buffered(1))
    any_spec = pl.BlockSpec(memory_space=pl.ANY)

    def ids(shift):
        def index_map(e, b):
            s = jnp.clip(e * B + b + shift, 0, E * B - 1)
            return (s % B, s // B, 0, 0)
        return pl.BlockSpec((None, None, 1, cap), index_map, memory_space=pltpu.SMEM)

    return pl.pallas_call(
        functools.partial(_moe_kernel, seq=S, cap=cap, f_chunk=f_chunk, n_tokens=T),
        out_shape=jax.ShapeDtypeStruct(hx.shape, F32), grid=(E, B),
        in_specs=[ids(-1), ids(0), ids(1),
                  pl.BlockSpec((None, None, 1, cap), lambda e, b: (b, e, 0, 0)),
                  wspec(w_gate), wspec(w_up), wspec(w_down), any_spec],
        out_specs=any_spec,
        scratch_shapes=[pltpu.VMEM((N_SLOTS, cap, 2 * D), F32), pltpu.SemaphoreType.DMA((2, N_SLOTS))],
        input_output_aliases={7: 0},
        compiler_params=_params(2), name="moe_ffn")(idx, idx, idx, gate, w_gate, w_up, w_down, hx)


def _cast_kernel(*refs):
    n = len(refs) // 2
    for src, dst in zip(refs[:n], refs[n:]):
        dst[...] = src[...].astype(dst.dtype)


WEIGHT_SPLIT = 2


def _expert_weights_bf16(layer, w_gate, w_up, w_down):
    _, E, D, F = w_gate.shape
    fh = F // WEIGHT_SPLIT
    cols = lambda: pl.BlockSpec((None, None, D, fh), lambda e, h: (layer, e, 0, h))
    rows = lambda: pl.BlockSpec((None, None, fh, D), lambda e, h: (layer, e, h, 0))
    return pl.pallas_call(
        _cast_kernel,
        out_shape=(jax.ShapeDtypeStruct((E, D, F), BF16), jax.ShapeDtypeStruct((E, D, F), BF16),
                   jax.ShapeDtypeStruct((E, F, D), BF16)),
        grid=(E, WEIGHT_SPLIT), in_specs=[cols(), cols(), rows()],
        out_specs=(pl.BlockSpec((None, D, fh), lambda e, h: (e, 0, h)),
                   pl.BlockSpec((None, D, fh), lambda e, h: (e, 0, h)),
                   pl.BlockSpec((None, fh, D), lambda e, h: (e, h, 0))),
        compiler_params=_params(2), name="cast_weights")(w_gate, w_up, w_down)


def moe_layer(o1, o2, w_out, x2, g_ffn, w_router, expert_weights, B, S, tm=512):
    hx, aff = _outproj_router(o1, o2, w_out, x2, g_ffn, w_router, B, S, tm)
    cap = EC_CAPACITY_FACTOR * S // w_router.shape[1]
    idx, gate = _topk(aff, cap)
    return _moe_ffn(hx, idx, gate, *expert_weights, B, S)


def _final_norm_kernel(x_ref, g_ref, o_ref):
    o_ref[...] = _rms(x_ref[...], g_ref[...])


def _final_norm(x2, g, tm, T):
    D = g.shape[-1]
    row = pl.BlockSpec((tm, D), lambda i: (i, 0))
    return pl.pallas_call(
        _final_norm_kernel, out_shape=jax.ShapeDtypeStruct((T, D), F32), grid=(T // tm,),
        in_specs=[_stream_spec(x2, D, tm), pl.BlockSpec((1, D), lambda i: (0, 0))], out_specs=row,
        compiler_params=_params(1), name="final_norm")(x2, g.reshape(1, D))


ODD_SPLITS = (C_Q_LORA, C_KV_LORA, C_ROPE, D_W, D_W, D_W, D_W, 4 * D_HEADS)
N_GATES = 4 * D_HEADS
I_FWD, F_FWD, I_BWD, F_BWD = 0, D_HEADS, 2 * D_HEADS, 3 * D_HEADS


def _odd_weights(w_in):
    cq, ckv, kr, dq, dk, dv, do, dg = jnp.split(w_in, [int(c) for c in np.cumsum(ODD_SPLITS)[:-1]], axis=1)
    D = w_in.shape[0]
    zeros = lambda n: jnp.zeros((D, n), w_in.dtype)
    kr_slot = jnp.concatenate([zeros(C_NOPE), kr, zeros(C_SLOT - C_NOPE - C_ROPE)], axis=1)
    dg_slot = jnp.concatenate([dg, zeros(LANES - N_GATES)], axis=1)
    w = jnp.concatenate([cq, ckv, kr_slot, dq, dk, dv, do, dg_slot], axis=1)
    return w.astype(BF16), dg.T.astype(BF16)


def _inproj_odd_kernel(x_ref, g_ref, w_ref, wgt_ref,
                       cq_ref, ckv_ref, kr_ref, dqk_ref, dv_ref, do_ref, gc_ref, gt_ref):
    h = _rms(x_ref[...], g_ref[...]).astype(BF16)
    off = 0
    for ref, width in ((cq_ref, C_Q_LORA), (ckv_ref, C_KV_LORA), (kr_ref, C_SLOT), (dqk_ref, 2 * D_W),
                       (dv_ref, D_W), (do_ref, D_W), (gc_ref, LANES)):
        ref[...] = _dot(h, w_ref[:, off:off + width]).astype(ref.dtype)
        off += width
    gt_ref[...] = _dot_nt(wgt_ref[...], h)


def _inproj_odd(x2, g, w, wgt, tm, T):
    D = g.shape[-1]
    row = lambda width: pl.BlockSpec((tm, width), lambda i: (i, 0))
    full = lambda a: pl.BlockSpec(a.shape, lambda i: (0, 0))
    widths = (C_Q_LORA, C_KV_LORA, C_SLOT, 2 * D_W, D_W, D_W, LANES)
    dtypes = (F32, F32, F32, F32, BF16, F32, F32)
    out_shape = tuple(jax.ShapeDtypeStruct((T, wd), dt) for wd, dt in zip(widths, dtypes))
    out_shape += (jax.ShapeDtypeStruct((N_GATES, T), F32),)
    return pl.pallas_call(
        _inproj_odd_kernel, out_shape=out_shape, grid=(T // tm,),
        in_specs=[_stream_spec(x2, D, tm), full(g), full(w), full(wgt)],
        out_specs=tuple(row(wd) for wd in widths) + (pl.BlockSpec((N_GATES, tm), lambda i: (0, i)),),
        compiler_params=_params(1), name="inproj_odd")(x2, g, w, wgt)


def _mla_prep_kernel(cq_ref, ckv_ref, kr_ref, nq_ref, nkv_ref, wq_ref, wk_ref, wv_ref, cos_ref, sin_ref,
                     q_ref, k_ref, vt_ref):
    lane = lax.broadcasted_iota(I32, (1, LANES), 1)
    first = lane < C_NOPE + C_ROPE // 2
    cos, sin = cos_ref[...], sin_ref[...]
    rope = lambda z: _rope_slab(z, cos, sin, C_ROPE // 2, first)
    q = _dot(_rms(cq_ref[...], nq_ref[...]).astype(BF16), wq_ref[...])
    hkv = _rms(ckv_ref[...], nkv_ref[...]).astype(BF16)
    kn = _dot(hkv, wk_ref[...])
    kr = rope(kr_ref[...])
    for hh in range(C_HEADS):
        slab = slice(hh * C_SLOT, (hh + 1) * C_SLOT)
        q_ref[:, slab] = (rope(q[:, slab]) * C_LOG2_SCALE).astype(BF16)
        k_ref[:, slab] = (kn[:, slab] + kr).astype(BF16)
    vt = _dot_nt(wv_ref[...], hkv)
    row = lax.broadcasted_iota(I32, vt.shape, 0)
    vt_ref[...] = jnp.where(row % C_VROWS < C_V, vt, 1.0).astype(BF16)


def _mla_prep(cq, ckv, kr, nq, nkv, w_uq, w_ukv, cos, sin, tm):
    T = cq.shape[0]
    pad_q = C_SLOT - C_NOPE - C_ROPE
    wq = jnp.pad(w_uq.reshape(C_Q_LORA, C_HEADS, C_NOPE + C_ROPE), ((0, 0), (0, 0), (0, pad_q)))
    wq = wq.reshape(C_Q_LORA, C_HEADS * C_SLOT).astype(BF16)
    wkv = w_ukv.reshape(C_KV_LORA, C_HEADS, C_NOPE + C_V)
    wk = jnp.pad(wkv[:, :, :C_NOPE], ((0, 0), (0, 0), (0, C_SLOT - C_NOPE)))
    wk = wk.reshape(C_KV_LORA, C_HEADS * C_SLOT).astype(BF16)
    wv = jnp.pad(wkv[:, :, C_NOPE:], ((0, 0), (0, 0), (0, C_VROWS - C_V)))
    wv = wv.reshape(C_KV_LORA, C_HEADS * C_VROWS).T.astype(BF16)
    nq, nkv = nq.reshape(1, -1), nkv.reshape(1, -1)
    row = lambda width: pl.BlockSpec((tm, width), lambda i: (i, 0))
    full = lambda a: pl.BlockSpec(a.shape, lambda i: (0, 0))
    slots = jax.ShapeDtypeStruct((T, C_HEADS * C_SLOT), BF16)
    vt_shape = jax.ShapeDtypeStruct((T // tm, C_HEADS * C_VROWS, tm), BF16)
    return pl.pallas_call(
        _mla_prep_kernel, out_shape=(slots, slots, vt_shape), grid=(T // tm,),
        in_specs=[row(C_Q_LORA), row(C_KV_LORA), row(C_SLOT), full(nq), full(nkv), full(wq), full(wk), full(wv),
                  row(LANES), row(LANES)],
        out_specs=(row(C_HEADS * C_SLOT), row(C_HEADS * C_SLOT),
                   pl.BlockSpec((None, C_HEADS * C_VROWS, tm), lambda i: (i, 0, 0))),
        compiler_params=_params(1), name="mla_prep")(cq, ckv, kr, nq, nkv, wq, wk, wv, cos, sin)


def _mla_attn_kernel(q_ref, k_ref, vt_ref, o_ref):
    tq = q_ref.shape[0]
    n_chunks, _, key_chunk = vt_ref.shape
    slabs = [slice(hh * C_SLOT, (hh + 1) * C_SLOT) for hh in range(C_PAIR)]
    vrows = [slice(hh * C_VROWS, (hh + 1) * C_VROWS) for hh in range(C_PAIR)]
    qs = [q_ref[:, slab] for slab in slabs]

    def scores(hh, c):
        st = _dot_nt(k_ref[c * key_chunk:(c + 1) * key_chunk, slabs[hh]], qs[hh])
        return st, jnp.max(st, axis=0, keepdims=True)

    m = [jnp.full((1, tq), NEG, F32) for _ in range(C_PAIR)]
    acc = [jnp.zeros((C_VROWS, tq), F32) for _ in range(C_PAIR)]
    st = [scores(hh, 0) for hh in range(C_PAIR)]
    for c in range(n_chunks):
        for hh in range(C_PAIR):
            st_next = scores(hh, c + 1) if c + 1 < n_chunks else None
            m_new = jnp.maximum(m[hh], st[hh][1])
            pt = jnp.exp2((st[hh][0] - m_new).astype(BF16))
            acc[hh] = jnp.exp2(m[hh] - m_new) * acc[hh] + _dot(vt_ref[c, vrows[hh], :], pt)
            m[hh], st[hh] = m_new, st_next
    for hh in range(C_PAIR):
        ot = acc[hh][:C_V] / acc[hh][C_V:C_V + 1]
        o = jnp.concatenate([ot, jnp.zeros((LANES - C_V, tq), F32)], axis=0).T
        o_ref[:, hh * C_V:(hh + 1) * C_V] = o[:, :C_V].astype(BF16)


def _mla_attn(q, k, vt, B, S, tq):
    T = B * S
    nq = S // tq
    key_chunk = vt.shape[-1]
    vt = vt.reshape(B, S // key_chunk, C_HEADS * C_VROWS, key_chunk)
    return pl.pallas_call(
        _mla_attn_kernel,
        out_shape=jax.ShapeDtypeStruct((T, C_HEADS * C_V), BF16), grid=(B, C_HEADS // C_PAIR, nq),
        in_specs=[pl.BlockSpec((tq, C_PAIR * C_SLOT), lambda b, p, i: (b * nq + i, p)),
                  pl.BlockSpec((S, C_PAIR * C_SLOT), lambda b, p, i: (b, p)),
                  pl.BlockSpec((None, S // key_chunk, C_PAIR * C_VROWS, key_chunk), lambda b, p, i: (b, 0, p, 0))],
        out_specs=pl.BlockSpec((tq, C_PAIR * C_V), lambda b, p, i: (b * nq + i, p)),
        compiler_params=_params(3), name="mla_attn")(q, k, vt)


HALO = 8


def _conv_kernel(xp_ref, x_ref, xn_ref, w_ref, o_ref, *, n_tiles):
    j = pl.program_id(1)
    tc = x_ref.shape[0]
    prev = jnp.where(j > 0, xp_ref[...], 0.0)
    nxt = jnp.where(j < n_tiles - 1, xn_ref[...], 0.0)
    ext = jnp.concatenate([prev, x_ref[...], nxt], axis=0)
    rows = tc + 2 * HALO
    y = jnp.zeros(x_ref.shape, F32)
    for w in range(D_CONV):
        first = HALO - D_CONV // 2 + w
        y = y + pltpu.roll(ext, (rows - first) % rows, 0)[:tc] * w_ref[w:w + 1, :]
    y = y * jax.nn.sigmoid(y)
    o_ref[:, :D_W] = y[:, :D_W]
    o_ref[:, D_W:] = y[:, D_W:] * (D_HEAD_DIM ** -0.5)


def _conv_prep(dqk, conv_w, B, S, tc):
    T, C = dqk.shape
    n_tiles = S // tc
    per, last = tc // HALO, T // HALO - 1
    cur = lambda b, j: (b * n_tiles + j, 0)
    return pl.pallas_call(
        functools.partial(_conv_kernel, n_tiles=n_tiles),
        out_shape=jax.ShapeDtypeStruct((T, C), F32), grid=(B, n_tiles),
        in_specs=[pl.BlockSpec((HALO, C), lambda b, j: (jnp.maximum((b * n_tiles + j) * per - 1, 0), 0)),
                  pl.BlockSpec((tc, C), cur),
                  pl.BlockSpec((HALO, C), lambda b, j: (jnp.minimum((b * n_tiles + j + 1) * per, last), 0)),
                  pl.BlockSpec(conv_w.shape, lambda b, j: (0, 0))],
        out_specs=pl.BlockSpec((tc, C), cur),
        compiler_params=_params(2), name="conv_prep")(dqk, dqk, dqk, conv_w)


def _log_sigmoid(x):
    return -(jnp.maximum(-x, 0.0) + jnp.log1p(jnp.exp(-jnp.abs(x))))


def _tri(lower):
    a = lax.broadcasted_iota(I32, (BLOCK, BLOCK), 0)
    b = lax.broadcasted_iota(I32, (BLOCK, BLOCK), 1)
    return ((b <= a) if lower else (b >= a)).astype(BF16)


def _mlstm_update(s_ref, m_ref, h, g, a, k, v_ones):
    a_max = jnp.max(a, axis=0, keepdims=True)
    w = jnp.exp(a - a_max)
    upd = _dot_tn(v_ones, (k * w).astype(BF16))
    m = m_ref[h:h + 1, 0:1]
    m_new = jnp.maximum(g + m, a_max)
    rows = slice(h * 2 * D_HEAD_DIM, (h + 1) * 2 * D_HEAD_DIM)
    s_ref[rows, :] = jnp.exp(g + m - m_new) * s_ref[rows, :] + jnp.exp(a_max - m_new) * upd
    m_ref[h:h + 1, :] = jnp.broadcast_to(m_new, (1, LANES))


def _mlstm_bwd_state_kernel(bias_c_ref, qk_ref, v_ref, gc_ref, s_out_ref, m_out_ref, s_ref, m_ref):
    @pl.when(pl.program_id(1) == 0)
    def _():
        s_ref[...] = jnp.zeros_like(s_ref)
        m_ref[...] = jnp.zeros_like(m_ref)

    s_out_ref[...] = s_ref[...]
    m_out_ref[...] = m_ref[...]
    n_seq = qk_ref.shape[0]
    gc = [gc_ref[i] + bias_c_ref[...] for i in range(n_seq)]
    lf = jnp.concatenate([_log_sigmoid(x) for x in gc], axis=1)
    suffix = sum(_dot(_tri(False), p) for p in _split3(lf))
    ones = jnp.ones((BLOCK, D_HEAD_DIM), BF16)
    for h in range(D_HEADS):
        hs = slice(h * D_HEAD_DIM, (h + 1) * D_HEAD_DIM)
        for i in range(n_seq):
            lane = i * LANES + F_BWD + h
            sb = suffix[:, lane:lane + 1]
            g = sb[0:1, :]
            a = g - sb + gc[i][:, I_BWD + h:I_BWD + h + 1]
            k = qk_ref[i, :, D_W + h * D_HEAD_DIM:D_W + (h + 1) * D_HEAD_DIM]
            _mlstm_update(s_ref.at[i], m_ref.at[i], h, g, a, k, jnp.concatenate([v_ref[i, :, hs], ones], axis=1))


def _mlstm_bwd_state(bias_c, qk, dv, gc, B, S):
    N = S // BLOCK
    nb = SEQS_PER_STEP if B % SEQS_PER_STEP == 0 else 1
    per_seq = lambda a: a.reshape(B, S, a.shape[-1])
    rev = lambda width: pl.BlockSpec((nb, BLOCK, width), lambda b, n: (b, N - 1 - n, 0))
    out_shape = (jax.ShapeDtypeStruct((B, N, 2 * D_W, D_HEAD_DIM), F32), jax.ShapeDtypeStruct((B, N, 8, LANES), F32))
    return pl.pallas_call(
        _mlstm_bwd_state_kernel, out_shape=out_shape, grid=(B // nb, N),
        in_specs=[pl.BlockSpec(bias_c.shape, lambda b, n: (0, 0)), rev(2 * D_W), rev(D_W), rev(LANES)],
        out_specs=(pl.BlockSpec((nb, None, 2 * D_W, D_HEAD_DIM), lambda b, n: (b, N - 1 - n, 0, 0)),
                   pl.BlockSpec((nb, None, 8, LANES), lambda b, n: (b, N - 1 - n, 0, 0))),
        scratch_shapes=[pltpu.VMEM((nb, 2 * D_W, D_HEAD_DIM), F32), pltpu.VMEM((nb, 8, LANES), F32)],
        compiler_params=_params(2), name="mlstm_bwd_state")(bias_c, per_seq(qk), per_seq(dv), per_seq(gc))


def _mlstm_main_kernel(bias_c_ref, bias_r_ref, qk_ref, v_ref, og_ref, gc_ref, gt_ref, sb_ref, mb_ref,
                       o_ref, s_ref, m_ref):
    @pl.when(pl.program_id(1) == 0)
    def _():
        s_ref[...] = jnp.zeros_like(s_ref)
        m_ref[...] = jnp.zeros_like(m_ref)

    gc = gc_ref[...] + bias_c_ref[...]
    gr = gt_ref[...] + bias_r_ref[...]
    lower, upper = _tri(True), _tri(False)
    lfc, lfr = _split3(_log_sigmoid(gc)), _split3(_log_sigmoid(gr))
    pre_c = sum(_dot(lower, p) for p in lfc)
    suf_c = sum(_dot(upper, p) for p in lfc)
    pre_r = sum(_dot(p, upper) for p in lfr)
    suf_r = sum(_dot(p, lower) for p in lfr)
    src = lax.broadcasted_iota(I32, (BLOCK, BLOCK), 0)
    qry = lax.broadcasted_iota(I32, (BLOCK, BLOCK), 1)
    ones = jnp.ones((BLOCK, D_HEAD_DIM), BF16)
    og = og_ref[...]
    for h in range(D_HEADS):
        hs = slice(h * D_HEAD_DIM, (h + 1) * D_HEAD_DIM)
        rows = slice(h * 2 * D_HEAD_DIM, (h + 1) * 2 * D_HEAD_DIM)
        k = qk_ref[:, D_W + h * D_HEAD_DIM:D_W + (h + 1) * D_HEAD_DIM]
        qb, v = qk_ref[:, hs].astype(BF16), v_ref[:, hs]
        kq = _dot_nt(k.astype(BF16), qb)

        def direction(b_row, c_col, mask, state_t, m_prev):
            logd = jnp.where(mask, b_row + c_col, NEG)
            log_inter = b_row + m_prev
            m_t = jnp.maximum(jnp.max(logd, axis=0, keepdims=True), log_inter)
            s = kq * jnp.exp(logd - m_t)
            inter_w = jnp.exp(log_inter - m_t)
            qs = _dot_nt(state_t.astype(BF16), qb)
            num = _dot_tn(v, s.astype(BF16)) + inter_w * qs[:D_HEAD_DIM, :]
            den = jnp.sum(s, axis=0, keepdims=True) + inter_w * qs[D_HEAD_DIM:D_HEAD_DIM + 1, :]
            return num / jnp.maximum(jnp.abs(den), jnp.exp(-m_t))

        f, bk = F_FWD + h, F_BWD + h
        h_f = direction(pre_r[f:f + 1, :], gc[:, I_FWD + h:I_FWD + h + 1] - pre_c[:, f:f + 1], src <= qry,
                        s_ref[rows, :], m_ref[h:h + 1, 0:1])
        h_b = direction(suf_r[bk:bk + 1, :], gc[:, I_BWD + h:I_BWD + h + 1] - suf_c[:, bk:bk + 1], src > qry,
                        sb_ref[rows, :], mb_ref[h:h + 1, 0:1])
        o_ref[:, hs] = (jax.nn.sigmoid(og[:, hs]) * (h_f + h_b).T).astype(BF16)
        b_col = pre_c[:, f:f + 1]
        g = b_col[BLOCK - 1:BLOCK, :]
        a = g - b_col + gc[:, I_FWD + h:I_FWD + h + 1]
        _mlstm_update(s_ref, m_ref, h, g, a, k, jnp.concatenate([v, ones], axis=1))


def _mlstm_main(bias_c, bias_r, qk, dv, og, gc, gt, sb, mb, B, S):
    N = S // BLOCK
    T = B * S
    cur = lambda b, n: (b * N + n, 0)
    return pl.pallas_call(
        _mlstm_main_kernel, out_shape=jax.ShapeDtypeStruct((T, D_W), BF16), grid=(B, N),
        in_specs=[pl.BlockSpec(bias_c.shape, lambda b, n: (0, 0)), pl.BlockSpec(bias_r.shape, lambda b, n: (0, 0)),
                  pl.BlockSpec((BLOCK, 2 * D_W), cur), pl.BlockSpec((BLOCK, D_W), cur), pl.BlockSpec((BLOCK, D_W), cur),
                  pl.BlockSpec((BLOCK, LANES), cur), pl.BlockSpec((N_GATES, BLOCK), lambda b, n: (0, b * N + n)),
                  pl.BlockSpec((None, None, 2 * D_W, D_HEAD_DIM), lambda b, n: (b, n, 0, 0)),
                  pl.BlockSpec((None, None, 8, LANES), lambda b, n: (b, n, 0, 0))],
        out_specs=pl.BlockSpec((BLOCK, D_W), cur),
        scratch_shapes=[pltpu.VMEM((2 * D_W, D_HEAD_DIM), F32), pltpu.VMEM((8, LANES), F32)],
        compiler_params=_params(2), name="mlstm_main")(bias_c, bias_r, qk, dv, og, gc, gt, sb, mb)


def odd_mixer_parts(x2, positions, g_mix, w_in, norm_q, norm_kv, w_uq, w_ukv, conv_w, gate_bias, B, S,
                    tm=512, tq=256, key_chunk=1024):
    w, wgt = _odd_weights(w_in)
    cq, ckv, kr, dqk, dv, og, gc, gt = _inproj_odd(x2, g_mix.reshape(1, -1), w, wgt, tm, B * S)
    cos, sin = _rope_tables(positions, MLA_THETA, C_SLOT, C_NOPE, C_ROPE)
    q, k, vt = _mla_prep(cq, ckv, kr, norm_q, norm_kv, w_uq, w_ukv, cos, sin, min(key_chunk, S))
    o_c = _mla_attn(q, k, vt, B, S, min(tq, S))
    qk = _conv_prep(dqk, conv_w.astype(F32), B, S, min(tm, S))
    bias = gate_bias.astype(F32).reshape(1, N_GATES)
    bias_c = jnp.pad(bias, ((0, 0), (0, LANES - N_GATES)))
    bias_r = bias.reshape(N_GATES, 1)
    sb, mb = _mlstm_bwd_state(bias_c, qk, dv, gc, B, S)
    o_d = _mlstm_main(bias_c, bias_r, qk, dv, og, gc, gt, sb, mb, B, S)
    return o_c, o_d


def kernel(x, positions, norm_mix, norm_ffn, norm_final, ev_w_in, ev_w_out, attn_sink, ret_decay_logit,
           od_w_in, od_w_out, mla_norm_q, mla_norm_kv, mla_w_uq, mla_w_ukv, mlstm_conv, mlstm_gate_bias,
           moe_router, moe_w_gate, moe_w_up, moe_w_down):
    B, S, D = x.shape
    tm = min(512, S)
    x2 = x.reshape(B * S, D)
    for layer in range(norm_mix.shape[0]):
        j = layer // 2
        if layer % 2 == 0:
            o1, o2 = even_mixer_parts(x2, positions, norm_mix[layer], ev_w_in[j], attn_sink[j],
                                      ret_decay_logit[j], B, S, tm)
            w_out = ev_w_out[j]
        else:
            o1, o2 = odd_mixer_parts(x2, positions, norm_mix[layer], od_w_in[j], mla_norm_q[j], mla_norm_kv[j],
                                     mla_w_uq[j], mla_w_ukv[j], mlstm_conv[j], mlstm_gate_bias[j], B, S, tm)
            w_out = od_w_out[j]
        x2 = moe_layer(o1, o2, w_out, x2, norm_ffn[layer], moe_router[layer],
                       _expert_weights_bf16(layer, moe_w_gate, moe_w_up, moe_w_down), B, S, tm)
    return _final_norm(x2, norm_final, tm, B * S).reshape(B, S, D)
```
